```python
import math
import jax, jax.numpy as jnp
from jax import lax
import numpy as np

D_MODEL = 1024
BATCH = 8
SEQ = 8192
DEPTH = 2

PLE_DIM = 256
RMS_EPS = 1e-6
ROPE_THETA = 10000.0
RW_HEADS = 4
RW_HEAD_DIM = 64
RW_WIDTH = RW_HEADS * RW_HEAD_DIM
RW_DECAY_LORA = 64
RW_AAA_LORA = 64
RW_GATE_LORA = 128
RW_GN_EPS = 64e-5
RW_IN = 3 * RW_WIDTH + RW_DECAY_LORA + RW_AAA_LORA + RW_GATE_LORA
RW_SPLITS = (RW_WIDTH, 2 * RW_WIDTH, 3 * RW_WIDTH, 3 * RW_WIDTH + RW_DECAY_LORA,
             3 * RW_WIDTH + RW_DECAY_LORA + RW_AAA_LORA)
POOL_GROUPS = 4
POOL_GROUP_DIM = 64
POOL_WIDTH = POOL_GROUPS * POOL_GROUP_DIM
POOL_WINDOWS = (2, 4, 8, 16)
NSA_Q_HEADS = 8
NSA_KV_HEADS = 2
NSA_HEAD_DIM = 64
NSA_GQA = NSA_Q_HEADS // NSA_KV_HEADS
NSA_WIDTH = NSA_Q_HEADS * NSA_HEAD_DIM
NSA_KV = NSA_KV_HEADS * NSA_HEAD_DIM
NSA_IN = NSA_WIDTH + 6 * NSA_KV + 3 * NSA_Q_HEADS
NSA_SPLITS = tuple(NSA_WIDTH + i * NSA_KV for i in range(7))
CMP_BLOCK = 32
CMP_STRIDE = 16
SEL_BLOCK = 64
SEL_TOPN = 16
WINDOW = 512
Q_BLOCK = 128
NEG = -1e30
FORCE = 1e9
IN_WIDTH = RW_IN + POOL_WIDTH + NSA_IN
MIX_WIDTH = RW_WIDTH + POOL_WIDTH + NSA_WIDTH
D_FF = 2816
CONV_W = 3

kernel_name = "hybrid_rwkv7_pool_nsa_convffn_ple"


def rmsnorm(x, g):
    xf = x.astype(jnp.float32)
    y = xf * lax.rsqrt(jnp.mean(xf * xf, axis=-1, keepdims=True) + RMS_EPS)
    return (y * g.astype(jnp.float32)).astype(x.dtype)


def token_shift(x):
    return jnp.pad(x, ((0, 0), (1, 0), (0, 0)))[:, :-1]


def rope(x, pos):
    half = x.shape[-1] // 2
    inv = ROPE_THETA ** (-jnp.arange(half, dtype=jnp.float32) / half)
    ang = pos.astype(jnp.float32)[..., None] * inv
    cos = jnp.cos(ang)[:, :, None, :]
    sin = jnp.sin(ang)[:, :, None, :]
    xf = x.astype(jnp.float32)
    x1, x2 = xf[..., :half], xf[..., half:]
    return jnp.concatenate([x1 * cos - x2 * sin, x2 * cos + x1 * sin], axis=-1)


def rwkv7_mixer(z, mu, w0, w_up, a0, a_up, g_up, k_k, k_a, r_k, gn_g, gn_b):
    B, T, _ = z.shape
    zf = z.astype(jnp.float32)
    zf = zf + (token_shift(zf) - zf) * mu.astype(jnp.float32)
    r, k, v, zw, za, zg = jnp.split(zf, RW_SPLITS, axis=-1)
    w = -jax.nn.softplus(-(w0 + jnp.tanh(zw) @ w_up)) - 0.5
    decay = jnp.exp(-jnp.exp(w))
    a = jax.nn.sigmoid(a0 + za @ a_up)
    g = jax.nn.sigmoid(zg) @ g_up
    kk = k * k_k
    k = k * (1.0 + (a - 1.0) * k_a)
    hs = lambda t: t.reshape(B, T, RW_HEADS, RW_HEAD_DIM).astype(jnp.float32)
    r, k, v, a, decay, kk = hs(r), hs(k), hs(v), hs(a), hs(decay), hs(kk)
    kk = kk / jnp.maximum(jnp.linalg.norm(kk, axis=-1, keepdims=True), 1e-12)

    def step(S, inp):
        r_t, w_t, k_t, v_t, kk_t, a_t = inp
        sa = jnp.einsum('bhvk,bhk->bhv', S, -kk_t)
        S = (S * w_t[:, :, None, :] + sa[..., None] * (kk_t * a_t)[:, :, None, :]
             + v_t[..., None] * k_t[:, :, None, :])
        return S, jnp.einsum('bhvk,bhk->bhv', S, r_t)

    xs = tuple(jnp.moveaxis(t, 1, 0) for t in (r, decay, k, v, kk, a))
    S0 = jnp.zeros((B, RW_HEADS, RW_HEAD_DIM, RW_HEAD_DIM), jnp.float32)
    _, y = lax.scan(step, S0, xs)
    y = jnp.moveaxis(y, 0, 1)
    mean = jnp.mean(y, axis=-1, keepdims=True)
    var = jnp.mean(jnp.square(y - mean), axis=-1, keepdims=True)
    y = ((y - mean) * lax.rsqrt(var + RW_GN_EPS)).reshape(B, T, RW_WIDTH) * gn_g + gn_b
    bonus = jnp.sum(r * k * r_k.astype(jnp.float32), axis=-1, keepdims=True) * v
    out = (y + bonus.reshape(B, T, RW_WIDTH)) * g
    return out.astype(z.dtype)


def pool_mixer(z, w_pool, scale):
    B, T, _ = z.shape
    zf = z.astype(jnp.float32).reshape(B, T, POOL_GROUPS, POOL_GROUP_DIM)
    c = jnp.pad(jnp.cumsum(zf, axis=1), ((0, 0), (1, 0), (0, 0), (0, 0)))
    t_idx = jnp.arange(T)
    outs = []
    for gi, win in enumerate(POOL_WINDOWS):
        lo = jnp.maximum(t_idx + 1 - win, 0)
        s = c[:, 1:, gi] - c[:, lo, gi]
        cnt = jnp.minimum(t_idx + 1, win).astype(jnp.float32)[:, None]
        outs.append(s / cnt - zf[:, :, gi])
    pooled = jnp.stack(outs, axis=2)
    y = jnp.einsum('btgc,gcd->btgd', pooled, w_pool.astype(jnp.float32))
    return (y.reshape(B, T, POOL_WIDTH) * scale).astype(z.dtype)


def nsa_mixer(z, pos, pe_k, pe_v, w_ck, w_cv):
    B, T, _ = z.shape
    f32 = jnp.float32
    q, kc, vc, ks, vs, kw, vw, gates = jnp.split(z, NSA_SPLITS, axis=-1)
    kvh = lambda t: t.reshape(B, T, NSA_KV_HEADS, NSA_HEAD_DIM)
    q = rope(q.reshape(B, T, NSA_Q_HEADS, NSA_HEAD_DIM), pos) * (NSA_HEAD_DIM ** -0.5)
    qg = q.reshape(B, T, NSA_KV_HEADS, NSA_GQA, NSA_HEAD_DIM)
    kc, ks, kw = rope(kvh(kc), pos), rope(kvh(ks), pos), rope(kvh(kw), pos)
    vc, vs, vw = kvh(vc).astype(f32), kvh(vs).astype(f32), kvh(vw).astype(f32)
    gates = jax.nn.sigmoid(gates.astype(f32)).reshape(B, T, NSA_KV_HEADS, NSA_GQA, 3)

    n_cmp = (T - CMP_BLOCK) // CMP_STRIDE + 1
    cmp_start = CMP_STRIDE * jnp.arange(n_cmp)
    idx = cmp_start[:, None] + jnp.arange(CMP_BLOCK)[None]
    k_cmp = jnp.einsum('bnlhd,lde->bnhe', kc[:, idx] + pe_k.astype(f32)[:, None, :], w_ck.astype(f32))
    v_cmp = jnp.einsum('bnlhd,lde->bnhe', vc[:, idx] + pe_v.astype(f32)[:, None, :], w_cv.astype(f32))
    cmp_end = cmp_start + CMP_BLOCK - 1

    n_sel = T // SEL_BLOCK
    top_n = min(SEL_TOPN, n_sel)
    sel_start = SEL_BLOCK * jnp.arange(n_sel)
    overlap = (jnp.minimum(cmp_end[:, None], sel_start[None] + SEL_BLOCK - 1)
               - jnp.maximum(cmp_start[:, None], sel_start[None]) + 1)
    m_cs = jnp.clip(overlap, 0, CMP_BLOCK).astype(f32) / CMP_BLOCK
    ks_blk = ks.reshape(B, n_sel, SEL_BLOCK, NSA_KV_HEADS, NSA_HEAD_DIM).transpose(0, 3, 1, 2, 4)
    vs_blk = vs.reshape(B, n_sel, SEL_BLOCK, NSA_KV_HEADS, NSA_HEAD_DIM).transpose(0, 3, 1, 2, 4)
    bi = jnp.arange(B)[:, None, None, None]
    hi = jnp.arange(NSA_KV_HEADS)[None, :, None, None]
    jj = jnp.arange(n_sel)

    kw_pad = jnp.pad(kw, ((0, 0), (WINDOW, 0), (0, 0), (0, 0)))
    vw_pad = jnp.pad(vw, ((0, 0), (WINDOW, 0), (0, 0), (0, 0)))

    def q_block(jb):
        s = jb * Q_BLOCK
        t = s + jnp.arange(Q_BLOCK)
        qb = lax.dynamic_slice_in_dim(qg, s, Q_BLOCK, axis=1)
        ok_c = cmp_end[None, :] <= t[:, None]
        sc = jnp.where(ok_c, jnp.einsum('bqhgd,bnhd->bhgqn', qb, k_cmp), NEG)
        pc = jax.nn.softmax(sc, axis=-1) * jnp.any(ok_c, axis=-1)[:, None].astype(f32)
        o_cmp = jnp.einsum('bhgqn,bnhd->bqhgd', pc, v_cmp)
        imp = jnp.einsum('bhqn,nj->bhqj', jnp.sum(pc, axis=2), m_cs)
        jc = t // SEL_BLOCK
        causal = jj[None] <= jc[:, None]
        force = (jj[None] == 0) | (jj[None] == jc[:, None]) | (jj[None] == jc[:, None] - 1)
        imp = jnp.where(causal, jnp.where(force, FORCE, imp), NEG)
        _, sel = lax.top_k(imp, top_n)
        kg = ks_blk[bi, hi, sel]
        vg = vs_blk[bi, hi, sel]
        kpos = sel[..., None] * SEL_BLOCK + jnp.arange(SEL_BLOCK)
        ok_s = (sel[..., None] <= jc[None, None, :, None, None]) & (kpos <= t[None, None, :, None, None])
        ss = jnp.where(ok_s[:, :, None], jnp.einsum('bqhgd,bhqnld->bhgqnl', qb, kg), NEG)
        ps = jax.nn.softmax(ss, axis=(-2, -1))
        o_sel = jnp.einsum('bhgqnl,bhqnld->bqhgd', ps, vg)
        kwb = lax.dynamic_slice_in_dim(kw_pad, s, WINDOW + Q_BLOCK, axis=1)
        vwb = lax.dynamic_slice_in_dim(vw_pad, s, WINDOW + Q_BLOCK, axis=1)
        kidx = s - WINDOW + jnp.arange(WINDOW + Q_BLOCK)
        ok_w = (kidx[None] <= t[:, None]) & (kidx[None] > t[:, None] - WINDOW) & (kidx[None] >= 0)
        sw = jnp.where(ok_w, jnp.einsum('bqhgd,bmhd->bhgqm', qb, kwb), NEG)
        o_win = jnp.einsum('bhgqm,bmhd->bqhgd', jax.nn.softmax(sw, axis=-1), vwb)
        gb = lax.dynamic_slice_in_dim(gates, s, Q_BLOCK, axis=1)
        o = gb[..., 0:1] * o_cmp + gb[..., 1:2] * o_sel + gb[..., 2:3] * o_win
        return o.reshape(B, Q_BLOCK, NSA_WIDTH)

    out = lax.map(q_block, jnp.arange(T // Q_BLOCK))
    return jnp.moveaxis(out, 0, 1).reshape(B, T, NSA_WIDTH).astype(z.dtype)


def conv_ffn(h, w_up, conv_w, conv_b, w_down):
    u = h @ w_up
    c = lax.conv_general_dilated(u, conv_w.astype(u.dtype), window_strides=(1,),
                                 padding=[(CONV_W - 1, 0)],
                                 dimension_numbers=('NWC', 'WIO', 'NWC'),
                                 feature_group_count=u.shape[-1]) + conv_b
    gate, val = jnp.split(c, 2, axis=-1)
    return (jax.nn.gelu(gate) * val) @ w_down


def setup_inputs(seed: int = 0) -> dict:
    key = jax.random.key(seed)
    ks = jax.random.split(key, 40)
    f32 = jnp.float32
    nrm = lambda k, shape, s: jax.random.normal(k, shape, f32) * s
    gain = lambda k, shape: 1.0 + nrm(k, shape, 0.02)
    L = DEPTH
    positions = (jax.random.randint(ks[2], (BATCH, 1), 0, 4096, jnp.int32)
                 + jnp.arange(SEQ, dtype=jnp.int32)[None])
    return {
        "x": nrm(ks[0], (BATCH, SEQ, D_MODEL), 1.0),
        "p": nrm(ks[1], (DEPTH, BATCH, SEQ, PLE_DIM), 1.0),
        "positions": positions,
        "g_mix": gain(ks[3], (L, D_MODEL)),
        "w_in": nrm(ks[4], (L, D_MODEL, IN_WIDTH), D_MODEL ** -0.5),
        "rw_mu": jax.random.uniform(ks[5], (L, RW_IN), f32, 0.2, 0.8),
        "rw_w0": jax.random.uniform(ks[6], (L, RW_WIDTH), f32, -6.0, -1.0),
        "rw_w_up": nrm(ks[7], (L, RW_DECAY_LORA, RW_WIDTH), 0.1),
        "rw_a0": nrm(ks[8], (L, RW_WIDTH), 0.1),
        "rw_a_up": nrm(ks[9], (L, RW_AAA_LORA, RW_WIDTH), 0.5 * RW_AAA_LORA ** -0.5),
        "rw_g_up": nrm(ks[10], (L, RW_GATE_LORA, RW_WIDTH), RW_GATE_LORA ** -0.5),
        "rw_k_k": 0.85 + nrm(ks[11], (L, RW_WIDTH), 0.02),
        "rw_k_a": gain(ks[12], (L, RW_WIDTH)),
        "rw_r_k": nrm(ks[13], (L, RW_HEADS, RW_HEAD_DIM), 0.1),
        "rw_gn_g": gain(ks[14], (L, RW_WIDTH)),
        "rw_gn_b": nrm(ks[15], (L, RW_WIDTH), 0.02),
        "pool_w": nrm(ks[16], (L, POOL_GROUPS, POOL_GROUP_DIM, POOL_GROUP_DIM), POOL_GROUP_DIM ** -0.5),
        "pool_scale": 1.0 + nrm(ks[17], (L, POOL_WIDTH), 0.1),
        "nsa_pe_k": nrm(ks[18], (L, CMP_BLOCK, NSA_HEAD_DIM), 0.02),
        "nsa_pe_v": nrm(ks[19], (L, CMP_BLOCK, NSA_HEAD_DIM), 0.02),
        "nsa_w_ck": nrm(ks[20], (L, CMP_BLOCK, NSA_HEAD_DIM, NSA_HEAD_DIM), (CMP_BLOCK * NSA_HEAD_DIM) ** -0.5),
        "nsa_w_cv": nrm(ks[21], (L, CMP_BLOCK, NSA_HEAD_DIM, NSA_HEAD_DIM), (CMP_BLOCK * NSA_HEAD_DIM) ** -0.5),
        "w_out": nrm(ks[22], (L, MIX_WIDTH, D_MODEL), MIX_WIDTH ** -0.5),
        "g_ffn": gain(ks[23], (L, D_MODEL)),
        "ffn_w_up": nrm(ks[24], (L, D_MODEL, 2 * D_FF), D_MODEL ** -0.5),
        "ffn_conv_w": nrm(ks[25], (L, CONV_W, 1, 2 * D_FF), CONV_W ** -0.5),
        "ffn_conv_b": nrm(ks[26], (L, 2 * D_FF), 0.02),
        "ffn_w_down": nrm(ks[27], (L, D_FF, D_MODEL), D_FF ** -0.5),
        "g_ple": gain(ks[28], (L, D_MODEL)),
        "ple_w_gate": nrm(ks[29], (L, D_MODEL, D_MODEL), D_MODEL ** -0.5),
        "ple_w_proj": nrm(ks[30], (L, PLE_DIM, D_MODEL), PLE_DIM ** -0.5),
        "g_final": gain(ks[31], (D_MODEL,)),
    }


def reference(x, p, positions, g_mix, w_in, rw_mu, rw_w0, rw_w_up, rw_a0, rw_a_up, rw_g_up,
              rw_k_k, rw_k_a, rw_r_k, rw_gn_g, rw_gn_b, pool_w, pool_scale,
              nsa_pe_k, nsa_pe_v, nsa_w_ck, nsa_w_cv, w_out, g_ffn, ffn_w_up, ffn_conv_w,
              ffn_conv_b, ffn_w_down, g_ple, ple_w_gate, ple_w_proj, g_final):
    for i in range(DEPTH):
        h = rmsnorm(x, g_mix[i])
        z = h @ w_in[i]
        z_a, z_b, z_c = jnp.split(z, (RW_IN, RW_IN + POOL_WIDTH), axis=-1)
        y_a = rwkv7_mixer(z_a, rw_mu[i], rw_w0[i], rw_w_up[i], rw_a0[i], rw_a_up[i], rw_g_up[i],
                          rw_k_k[i], rw_k_a[i], rw_r_k[i], rw_gn_g[i], rw_gn_b[i])
        y_b = pool_mixer(z_b, pool_w[i], pool_scale[i])
        y_c = nsa_mixer(z_c, positions, nsa_pe_k[i], nsa_pe_v[i], nsa_w_ck[i], nsa_w_cv[i])
        x = x + jnp.concatenate([y_a, y_b, y_c], axis=-1) @ w_out[i]
        x = x + conv_ffn(rmsnorm(x, g_ffn[i]), ffn_w_up[i], ffn_conv_w[i], ffn_conv_b[i], ffn_w_down[i])
        gate = jax.nn.sigmoid(rmsnorm(x, g_ple[i]) @ ple_w_gate[i])
        x = x + (p[i] @ ple_w_proj[i]) * gate
    return rmsnorm(x, g_final)
```

```python
import functools
import math

import numpy as np
import jax
import jax.numpy as jnp
from jax import lax
from jax.experimental import pallas as pl
from jax.experimental.pallas import tpu as pltpu

F32 = jnp.float32
BF16 = jnp.bfloat16

RMS_EPS = 1e-6
ROPE_THETA = 10000.0
HEAD = 64
RW_HEADS = 4
RW_WIDTH = RW_HEADS * HEAD
RW_DECAY_LORA = 64
RW_AAA_LORA = 64
RW_GATE_LORA = 128
RW_GN_EPS = 64e-5
RW_IN = 3 * RW_WIDTH + RW_DECAY_LORA + RW_AAA_LORA + RW_GATE_LORA
POOL_WIDTH = 256
POOL_WINDOWS = (2, 4, 8, 16)
POOL_HALO = 16
NSA_Q_HEADS = 8
NSA_KV_HEADS = 2
NSA_GQA = NSA_Q_HEADS // NSA_KV_HEADS
NSA_WIDTH = NSA_Q_HEADS * HEAD
NSA_KV = NSA_KV_HEADS * HEAD
CMP_BLOCK = 32
CMP_STRIDE = 16
SEL_BLOCK = 64
SEL_TOPN = 16
WINDOW = 512
Q_BLOCK = 128
KEY_TILE = 128
NSEL_PAD = 128
NEG = -1e30
FORCE = 1e9
CONV_W = 3
LANE = 128
VMEM_LIMIT = 56 * 1024 * 1024

RW_CHUNK = 64

_HI = lax.Precision.HIGHEST


def _cparams(sem):
    return pltpu.CompilerParams(dimension_semantics=sem, vmem_limit_bytes=VMEM_LIMIT)


def _dot(a, b, prec=None):
    return lax.dot_general(a, b, (((1,), (0,)), ((), ())), precision=prec, preferred_element_type=F32)


def _dot_nt(a, b, prec=None):
    return lax.dot_general(a, b, (((1,), (1,)), ((), ())), precision=prec, preferred_element_type=F32)


def _sigmoid(x):
    return 1.0 / (1.0 + jnp.exp(-x))


def _rms(x, g):
    ms = jnp.mean(x * x, axis=-1, keepdims=True)
    return (x * lax.rsqrt(ms + RMS_EPS)) * g


def _norm_mm_body(x_ref, g_ref, w_ref, *o_refs, splits):
    h = _rms(x_ref[...], g_ref[...]).astype(BF16)
    for o_ref, (lo, hi) in zip(o_refs, splits):
        o_ref[...] = _dot(h, w_ref[:, lo:hi])


def _norm_matmul(x2, g, w_bf16, splits, tm=512):
    m, d = x2.shape
    n = w_bf16.shape[1]
    outs = tuple(jax.ShapeDtypeStruct((m, hi - lo), F32) for lo, hi in splits)
    return pl.pallas_call(
        functools.partial(_norm_mm_body, splits=splits),
        grid=(m // tm,),
        in_specs=[pl.BlockSpec((tm, d), lambda i: (i, 0)),
                  pl.BlockSpec((1, d), lambda i: (0, 0)),
                  pl.BlockSpec((d, n), lambda i: (0, 0))],
        out_specs=tuple(pl.BlockSpec((tm, hi - lo), lambda i: (i, 0)) for lo, hi in splits),
        out_shape=outs,
        compiler_params=_cparams(("parallel",)),
        name="norm_in_proj",
    )(x2, g.reshape(1, d), w_bf16)


def _rw_chunk_body(z_ref, zp_ref, mu_ref, w0_ref, wup_ref, a0_ref, aup_ref, gup_ref, kk_ref, ka_ref,
                   rk_ref, p_ref, q_ref, rh_ref, y0_ref, g_ref, bonus_ref):
    c = pl.program_id(1)
    C = RW_CHUNK
    z = z_ref[0]
    prev = jnp.where(c == 0, 0.0, zp_ref[0, 7:8, :])
    row = lax.broadcasted_iota(jnp.int32, z.shape, 0)
    zs = jnp.where(row == 0, prev, pltpu.roll(z, 1, axis=0))
    zf = z + (zs - z) * mu_ref[...]
    W = RW_WIDTH
    r = zf[:, 0:W]
    k = zf[:, W:2 * W]
    v = zf[:, 2 * W:3 * W]
    zw = zf[:, 3 * W:3 * W + RW_DECAY_LORA]
    za = zf[:, 3 * W + RW_DECAY_LORA:3 * W + RW_DECAY_LORA + RW_AAA_LORA]
    zg = zf[:, 3 * W + RW_DECAY_LORA + RW_AAA_LORA:]

    xw = -(w0_ref[...] + _dot(jnp.tanh(zw), wup_ref[...], _HI))
    softplus = jnp.maximum(xw, 0.0) + jnp.log(1.0 + jnp.exp(-jnp.abs(xw)))
    ld = -jnp.exp(-softplus - 0.5)
    a = _sigmoid(a0_ref[...] + _dot(za, aup_ref[...], _HI))
    g_ref[0] = _dot(_sigmoid(zg), gup_ref[...], _HI)
    kk = k * kk_ref[...]
    k2 = k * (1.0 + (a - 1.0) * ka_ref[...])

    hr = lax.broadcasted_iota(jnp.int32, (W, W), 0) // HEAD
    hc = lax.broadcasted_iota(jnp.int32, (W, W), 1) // HEAD
    head_ones = jnp.where(hr == hc, 1.0, 0.0)
    kkn = kk / jnp.maximum(jnp.sqrt(_dot(kk * kk, head_ones, _HI)), 1e-12)
    bonus_ref[0] = _dot(r * k2 * rk_ref[...], head_ones, _HI) * v

    tr = lax.broadcasted_iota(jnp.int32, (C, C), 0)
    tc = lax.broadcasted_iota(jnp.int32, (C, C), 1)
    lower_incl = tr >= tc
    lower_strict = tr > tc
    eye = jnp.where(tr == tc, 1.0, 0.0)
    lc = _dot(jnp.where(lower_incl, 1.0, 0.0), ld, _HI)
    lc_end = lc[C - 1:C, :]
    ginv = jnp.exp(-lc)
    dec_end = jnp.exp(lc_end - lc)
    a_t = -kkn * jnp.exp(lc - ld)
    b_t = kkn * a * ginv
    k_t = k2 * ginv
    r_t = r * jnp.exp(lc)
    b_h = kkn * a * dec_end
    k_h = k2 * dec_end
    gam_end = jnp.exp(lc_end)

    for h in range(RW_HEADS):
        sl = slice(h * HEAD, (h + 1) * HEAD)
        A, Bt, Kt, Rt, V = a_t[:, sl], b_t[:, sl], k_t[:, sl], r_t[:, sl], v[:, sl]
        L = jnp.where(lower_strict, _dot_nt(A, Bt, _HI), 0.0)
        Lak = jnp.where(lower_strict, _dot_nt(A, Kt, _HI), 0.0)
        Mrb = jnp.where(lower_incl, _dot_nt(Rt, Bt, _HI), 0.0)
        Mrk = jnp.where(lower_incl, _dot_nt(Rt, Kt, _HI), 0.0)
        Tm = eye + L
        Pw = L
        for _ in range(int(math.log2(C)) - 1):
            Pw = _dot(Pw, Pw, _HI)
            Tm = Tm + _dot(Tm, Pw, _HI)
        Wm = _dot(Tm, A, _HI)
        U0 = _dot(Tm, _dot(Lak, V, _HI), _HI)
        BhT = b_h[:, sl].T
        KhT = k_h[:, sl].T
        gdiag = jnp.where(tr == tc, jnp.broadcast_to(gam_end[:, sl], (C, HEAD)), 0.0)
        p_ref[0, 0, h] = gdiag + _dot(BhT, Wm, _HI)
        q_ref[0, 0, h] = _dot(BhT, U0, _HI) + _dot(KhT, V, _HI)
        rh_ref[0, :, sl] = Rt + _dot(Mrb, Wm, _HI)
        y0_ref[0, :, sl] = _dot(Mrb, U0, _HI) + _dot(Mrk, V, _HI)


def _rw_chunks(z_rw, mu, w0, w_up, a0, a_up, g_up, k_k, k_a, r_k):
    B, T, _ = z_rw.shape
    C = RW_CHUNK
    assert C == HEAD, "the chunk build shares one (C, C) index grid between time and head-dim masks"
    nc = T // C
    row = lambda a: a.reshape(1, -1)
    full = lambda shp: pl.BlockSpec(shp, lambda b, c: (0,) * len(shp))
    seq = lambda w: pl.BlockSpec((1, C, w), lambda b, c: (b, c, 0))
    pq = pl.BlockSpec((1, 1, RW_HEADS, HEAD, HEAD), lambda b, c: (b, c, 0, 0, 0))
    pq_shape = jax.ShapeDtypeStruct((B, nc, RW_HEADS, HEAD, HEAD), F32)
    seq_shape = jax.ShapeDtypeStruct((B, T, RW_WIDTH), F32)
    return pl.pallas_call(
        _rw_chunk_body,
        grid=(B, nc),
        in_specs=[seq(RW_IN),
                  pl.BlockSpec((1, 8, RW_IN), lambda b, c: (b, jnp.maximum(c * (C // 8) - 1, 0), 0)),
                  full((1, RW_IN)), full((1, RW_WIDTH)), full((RW_DECAY_LORA, RW_WIDTH)),
                  full((1, RW_WIDTH)), full((RW_AAA_LORA, RW_WIDTH)), full((RW_GATE_LORA, RW_WIDTH)),
                  full((1, RW_WIDTH)), full((1, RW_WIDTH)), full((1, RW_WIDTH))],
        out_specs=(pq, pq, seq(RW_WIDTH), seq(RW_WIDTH), seq(RW_WIDTH), seq(RW_WIDTH)),
        out_shape=(pq_shape, pq_shape, seq_shape, seq_shape, seq_shape, seq_shape),
        compiler_params=_cparams(("parallel", "parallel")),
        name="rwkv_chunk_build",
    )(z_rw, z_rw, row(mu), row(w0), w_up, row(a0), a_up, g_up, row(k_k), row(k_a), row(r_k))


def _rw_scan_body(p_ref, q_ref, rh_ref, y0_ref, g_ref, bonus_ref, gng_ref, gnb_ref, o_ref, h_ref):
    @pl.when(pl.program_id(0) == 0)
    def _():
        h_ref[...] = jnp.zeros_like(h_ref)

    nb = h_ref.shape[0]
    for b in range(nb):
        for h in range(RW_HEADS):
            sl = slice(h * HEAD, (h + 1) * HEAD)
            H = h_ref[b, h]
            y = _dot(rh_ref[b, :, sl], H, _HI) + y0_ref[b, :, sl]
            h_ref[b, h] = _dot(p_ref[b, 0, h], H, _HI) + q_ref[b, 0, h]
            mean = jnp.mean(y, axis=-1, keepdims=True)
            var = jnp.mean(jnp.square(y - mean), axis=-1, keepdims=True)
            yn = (y - mean) * lax.rsqrt(var + RW_GN_EPS) * gng_ref[:, sl] + gnb_ref[:, sl]
            o_ref[b, :, sl] = (yn + bonus_ref[b, :, sl]) * g_ref[b, :, sl]


def _rw_scan(p, q, rh, y0, g, bonus, gn_g, gn_b):
    B, T, _ = rh.shape
    C = RW_CHUNK
    nc = T // C
    pq = pl.BlockSpec((B, 1, RW_HEADS, HEAD, HEAD), lambda c: (0, c, 0, 0, 0))
    seq = pl.BlockSpec((B, C, RW_WIDTH), lambda c: (0, c, 0))
    vec = pl.BlockSpec((1, RW_WIDTH), lambda c: (0, 0))
    return pl.pallas_call(
        _rw_scan_body,
        grid=(nc,),
        in_specs=[pq, pq, seq, seq, seq, seq, vec, vec],
        out_specs=seq,
        out_shape=jax.ShapeDtypeStruct((B, T, RW_WIDTH), F32),
        scratch_shapes=[pltpu.VMEM((B, RW_HEADS, HEAD, HEAD), F32)],
        compiler_params=_cparams(("arbitrary",)),
        name="rwkv_chunk_scan",
    )(p, q, rh, y0, g, bonus, gn_g.reshape(1, -1), gn_b.reshape(1, -1))


def _pool_body(z_ref, zp_ref, w_ref, scale_ref, o_ref):
    i = pl.program_id(1)
    z = z_ref[0]
    tt = z.shape[0]
    halo = jnp.where(i == 0, 0.0, zp_ref[0])
    e = jnp.concatenate([halo, z], axis=0)
    sums = []
    s = e
    for shift in (1, 2, 4, 8):
        s = s + pltpu.roll(s, shift, axis=0)
        sums.append(s[POOL_HALO:, :])
    t_idx = i * tt + lax.broadcasted_iota(jnp.int32, (tt, POOL_WIDTH), 0)
    lane_group = lax.broadcasted_iota(jnp.int32, (tt, POOL_WIDTH), 1) // HEAD
    pooled = jnp.zeros_like(z)
    for gi, win in enumerate(POOL_WINDOWS):
        cnt = jnp.minimum(t_idx + 1, win).astype(F32)
        pooled = jnp.where(lane_group == gi, sums[gi] / cnt - z, pooled)
    o_ref[0] = _dot(pooled.astype(BF16), w_ref[...]) * scale_ref[...]


def _pool(z_pool, w_blockdiag_bf16, scale, tt=512):
    B, T, _ = z_pool.shape
    return pl.pallas_call(
        _pool_body,
        grid=(B, T // tt),
        in_specs=[pl.BlockSpec((1, tt, POOL_WIDTH), lambda b, i: (b, i, 0)),
                  pl.BlockSpec((1, POOL_HALO, POOL_WIDTH),
                               lambda b, i: (b, jnp.maximum(i * (tt // POOL_HALO) - 1, 0), 0)),
                  pl.BlockSpec((POOL_WIDTH, POOL_WIDTH), lambda b, i: (0, 0)),
                  pl.BlockSpec((1, POOL_WIDTH), lambda b, i: (0, 0))],
        out_specs=pl.BlockSpec((1, tt, POOL_WIDTH), lambda b, i: (b, i, 0)),
        out_shape=jax.ShapeDtypeStruct((B, T, POOL_WIDTH), F32),
        compiler_params=_cparams(("parallel", "parallel")),
        name="pool_mixer",
    )(z_pool, z_pool, w_blockdiag_bf16, scale.reshape(1, -1))


def _rope(x, cos_w, sin_w):
    w = x.shape[1]
    lane = lax.broadcasted_iota(jnp.int32, x.shape, 1) % HEAD
    partner = jnp.where(lane < HEAD // 2, pltpu.roll(x, w - HEAD // 2, axis=1), pltpu.roll(x, HEAD // 2, axis=1))
    return x * cos_w + partner * sin_w


def _nsa_prep_body(zq_ref, zkc_ref, zks_ref, zvs_ref, zkw_ref, zvw_ref, zg_ref, cos_ref, sin_ref,
                   q_ref, kc_ref, ks_ref, vst_ref, kw_ref, vwt_ref, gt_ref):
    cos = cos_ref[0]
    sin = sin_ref[0]
    cos_q = jnp.concatenate([cos] * (NSA_WIDTH // LANE), axis=1)
    sin_q = jnp.concatenate([sin] * (NSA_WIDTH // LANE), axis=1)
    qr = _rope(zq_ref[0], cos_q, sin_q) * (HEAD ** -0.5)
    for h in range(NSA_Q_HEADS):
        q_ref[0, h] = qr[:, h * HEAD:(h + 1) * HEAD].astype(BF16)
    kc_ref[0] = _rope(zkc_ref[0], cos, sin)
    ks = _rope(zks_ref[0], cos, sin)
    kw = _rope(zkw_ref[0], cos, sin)
    vst = zvs_ref[0].T
    vwt = zvw_ref[0].T
    for h in range(NSA_KV_HEADS):
        sl = slice(h * HEAD, (h + 1) * HEAD)
        ks_ref[0, h] = ks[:, sl].astype(BF16)
        kw_ref[0, h] = kw[:, sl].astype(BF16)
        vst_ref[0, h] = vst[sl, :].astype(BF16)
        vwt_ref[0, h] = vwt[sl, :].astype(BF16)
    gt = _sigmoid(zg_ref[0]).T
    for h in range(NSA_KV_HEADS):
        gt_ref[0, h] = gt[16 * h:16 * (h + 1), :]


def _nsa_prep(z_q, z_kc, z_ks, z_vs, z_kw, z_vw, z_g, cos_t, sin_t, tt=512):
    B, T, _ = z_q.shape
    seq = lambda w: pl.BlockSpec((1, tt, w), lambda b, i: (b, i, 0))
    hm = lambda nh: pl.BlockSpec((1, nh, tt, HEAD), lambda b, i: (b, 0, i, 0))
    tr = lambda rows: pl.BlockSpec((1, NSA_KV_HEADS, rows, tt), lambda b, i: (b, 0, 0, i))
    sds = jax.ShapeDtypeStruct
    return pl.pallas_call(
        _nsa_prep_body,
        grid=(B, T // tt),
        in_specs=[seq(NSA_WIDTH)] + [seq(NSA_KV)] * 6 + [seq(LANE), seq(LANE)],
        out_specs=(hm(NSA_Q_HEADS), seq(NSA_KV), hm(NSA_KV_HEADS), tr(HEAD), hm(NSA_KV_HEADS), tr(HEAD), tr(16)),
        out_shape=(sds((B, NSA_Q_HEADS, T, HEAD), BF16), sds((B, T, NSA_KV), F32),
                   sds((B, NSA_KV_HEADS, T, HEAD), BF16), sds((B, NSA_KV_HEADS, HEAD, T), BF16),
                   sds((B, NSA_KV_HEADS, T, HEAD), BF16), sds((B, NSA_KV_HEADS, HEAD, T), BF16),
                   sds((B, NSA_KV_HEADS, 16, T), F32)),
        compiler_params=_cparams(("parallel", "parallel")),
        name="nsa_prep",
    )(z_q, z_kc, z_ks, z_vs, z_kw, z_vw, z_g, cos_t, sin_t)


def _nsa_cmp_body(gk_ref, gv_ref, pek_ref, pev_ref, wk_ref, wv_ref, kc_ref, vct_ref):
    half = NSA_KV

    def compress(g, pe_ref, w_ref):
        n = g.shape[0]
        lo = _dot((g + pe_ref[0:1, :]).astype(BF16), w_ref[:, 0:half])
        hi = _dot((g + pe_ref[1:2, :]).astype(BF16), w_ref[:, half:2 * half])
        hi_next = pltpu.roll(hi, n - 1, axis=0)
        row = lax.broadcasted_iota(jnp.int32, lo.shape, 0)
        return lo + jnp.where(row == n - 1, 0.0, hi_next)

    kc = compress(gk_ref[0], pek_ref, wk_ref)
    vct = compress(gv_ref[0], pev_ref, wv_ref).T
    for h in range(NSA_KV_HEADS):
        sl = slice(h * HEAD, (h + 1) * HEAD)
        kc_ref[0, h] = kc[:, sl].astype(BF16)
        vct_ref[0, h] = vct[sl, :].astype(BF16)


def _nsa_compress(kc_groups, vc_groups, pe_k2, pe_v2, wk2, wv2):
    B, ncp, gw = kc_groups.shape
    grp = pl.BlockSpec((1, ncp, gw), lambda b: (b, 0, 0))
    pe = pl.BlockSpec((2, gw), lambda b: (0, 0))
    wspec = pl.BlockSpec((gw, 2 * NSA_KV), lambda b: (0, 0))
    sds = jax.ShapeDtypeStruct
    return pl.pallas_call(
        _nsa_cmp_body,
        grid=(B,),
        in_specs=[grp, grp, pe, pe, wspec, wspec],
        out_specs=(pl.BlockSpec((1, NSA_KV_HEADS, ncp, HEAD), lambda b: (b, 0, 0, 0)),
                   pl.BlockSpec((1, NSA_KV_HEADS, HEAD, ncp), lambda b: (b, 0, 0, 0))),
        out_shape=(sds((B, NSA_KV_HEADS, ncp, HEAD), BF16), sds((B, NSA_KV_HEADS, HEAD, ncp), BF16)),
        compiler_params=_cparams(("parallel",)),
        name="nsa_compress",
    )(kc_groups, vc_groups, pe_k2, pe_v2, wk2, wv2)


def _nsa_attn_body(q_ref, kc_ref, vct_ref, ks_ref, vst_ref, kw_ref, vwt_ref, gt_ref, mcs_ref,
                   o_ref, val_ref, bias_ref):
    jb = pl.program_id(2)
    s0 = jb * Q_BLOCK
    cols = NSA_GQA * Q_BLOCK
    q = q_ref[0].reshape(cols, HEAD)
    ncp = kc_ref.shape[2]

    sc = _dot_nt(kc_ref[0, 0], q)
    n_idx = lax.broadcasted_iota(jnp.int32, (ncp, cols), 0)
    t_c = s0 + lax.broadcasted_iota(jnp.int32, (ncp, cols), 1) % Q_BLOCK
    sc = jnp.where(CMP_STRIDE * n_idx + (CMP_BLOCK - 1) <= t_c, sc, NEG)
    e = jnp.exp(sc - jnp.max(sc, axis=0, keepdims=True))
    any_valid = jnp.where(t_c[0:1, :] >= CMP_BLOCK - 1, 1.0, 0.0)
    pc = e * (any_valid / jnp.sum(e, axis=0, keepdims=True))
    o_cmp = _dot(vct_ref[0, 0], pc.astype(BF16))
    psum = pc[:, 0:Q_BLOCK]
    for g in range(1, NSA_GQA):
        psum = psum + pc[:, g * Q_BLOCK:(g + 1) * Q_BLOCK]
    imp = _dot(mcs_ref[...], psum, _HI)

    j_idx = lax.broadcasted_iota(jnp.int32, (NSEL_PAD, Q_BLOCK), 0)
    jc = (s0 + lax.broadcasted_iota(jnp.int32, (NSEL_PAD, Q_BLOCK), 1)) // SEL_BLOCK
    causal = j_idx <= jc
    forced_imp = jnp.where(j_idx == 0, FORCE, jnp.where(j_idx == jc, FORCE, jnp.where(j_idx == jc - 1, FORCE, imp)))
    val = jnp.where(causal, forced_imp, NEG)
    val_ref[...] = val
    n_causal_blocks = (s0 + Q_BLOCK) // SEL_BLOCK

    def rank_step(jp, cnt):
        other = jnp.broadcast_to(val_ref[pl.ds(jp, 1), :], (NSEL_PAD, Q_BLOCK))
        ahead = jnp.where(other > val, 1.0, jnp.where(other == val, jnp.where(j_idx > jp, 1.0, 0.0), 0.0))
        return cnt + ahead

    rank = lax.fori_loop(0, n_causal_blocks, rank_step, jnp.zeros((NSEL_PAD, Q_BLOCK), F32))
    bias = jnp.where(causal, jnp.where(rank < SEL_TOPN, 0.0, NEG), NEG)
    bias_ref[...] = jnp.concatenate([bias] * NSA_GQA, axis=1)

    t_k = s0 + lax.broadcasted_iota(jnp.int32, (KEY_TILE, cols), 1) % Q_BLOCK
    k_off = lax.broadcasted_iota(jnp.int32, (KEY_TILE, cols), 0)
    half = KEY_TILE // 2

    def flash_step(v_tile_t, st, carry):
        m, l, acc = carry
        m_new = jnp.maximum(m, jnp.max(st, axis=0, keepdims=True))
        alpha = jnp.exp(m - m_new)
        p = jnp.exp(st - m_new)
        l = l * alpha + jnp.sum(p, axis=0, keepdims=True)
        acc = acc * alpha + _dot(v_tile_t, p.astype(BF16))
        return m_new, l, acc

    init = (jnp.full((1, cols), NEG, F32), jnp.zeros((1, cols), F32), jnp.zeros((HEAD, cols), F32))

    def sel_step(kt, carry):
        off = pl.multiple_of(kt * KEY_TILE, KEY_TILE)
        k_tile = ks_ref[0, 0, pl.ds(off, KEY_TILE), :]
        st = _dot_nt(k_tile, q)
        b0 = jnp.broadcast_to(bias_ref[pl.ds(2 * kt, 1), :], (half, cols))
        b1 = jnp.broadcast_to(bias_ref[pl.ds(2 * kt + 1, 1), :], (half, cols))
        st = jnp.where(off + k_off <= t_k, st + jnp.concatenate([b0, b1], axis=0), NEG)
        return flash_step(vst_ref[0, 0, :, pl.ds(off, KEY_TILE)], st, carry)

    _, l_s, acc_s = lax.fori_loop(0, jb + 1, sel_step, init)

    def win_step(kt, carry):
        off = pl.multiple_of(kt * KEY_TILE, KEY_TILE)
        k_tile = kw_ref[0, 0, pl.ds(off, KEY_TILE), :]
        st = _dot_nt(k_tile, q)
        kpos = off + k_off
        st = jnp.where(kpos <= t_k, jnp.where(kpos > t_k - WINDOW, st, NEG), NEG)
        return flash_step(vwt_ref[0, 0, :, pl.ds(off, KEY_TILE)], st, carry)

    _, l_w, acc_w = lax.fori_loop(jnp.maximum(jb - WINDOW // KEY_TILE, 0), jb + 1, win_step, init)

    def gate_row(branch):
        return jnp.concatenate([gt_ref[0, 0, 3 * g + branch:3 * g + branch + 1, :] for g in range(NSA_GQA)], axis=1)

    o_t = gate_row(0) * o_cmp + (gate_row(1) / l_s) * acc_s + (gate_row(2) / l_w) * acc_w
    stacked = jnp.concatenate([o_t[:, g * Q_BLOCK:(g + 1) * Q_BLOCK] for g in range(NSA_GQA)], axis=0)
    o_ref[0] = stacked.T


def _nsa_attention(q_r, kcmp, vcmp_t, ks_r, vs_t, kw_r, vw_t, g_t, mcs_t):
    B, _, T, _ = q_r.shape
    ncp = kcmp.shape[2]
    nqb = T // Q_BLOCK
    kv_rows = lambda n: pl.BlockSpec((1, 1, n, HEAD), lambda b, h, j: (b, h, 0, 0))
    kv_cols = lambda n: pl.BlockSpec((1, 1, HEAD, n), lambda b, h, j: (b, h, 0, 0))
    return pl.pallas_call(
        _nsa_attn_body,
        grid=(B, NSA_KV_HEADS, nqb),
        in_specs=[pl.BlockSpec((1, NSA_GQA, Q_BLOCK, HEAD), lambda b, h, j: (b, h, j, 0)),
                  kv_rows(ncp), kv_cols(ncp), kv_rows(T), kv_cols(T), kv_rows(T), kv_cols(T),
                  pl.BlockSpec((1, 1, 16, Q_BLOCK), lambda b, h, j: (b, h, 0, j)),
                  pl.BlockSpec((NSEL_PAD, ncp), lambda b, h, j: (0, 0))],
        out_specs=pl.BlockSpec((1, Q_BLOCK, NSA_GQA * HEAD), lambda b, h, j: (b, j, h)),
        out_shape=jax.ShapeDtypeStruct((B, T, NSA_WIDTH), F32),
        scratch_shapes=[pltpu.VMEM((NSEL_PAD, Q_BLOCK), F32), pltpu.VMEM((NSEL_PAD, NSA_GQA * Q_BLOCK), F32)],
        compiler_params=_cparams(("parallel", "parallel", "arbitrary")),
        name="nsa_attention",
    )(q_r, kcmp, vcmp_t, ks_r, vs_t, kw_r, vw_t, g_t, mcs_t)


def _out_proj_body(x_ref, ya_ref, yb_ref, yc_ref, wa_ref, wb_ref, wc_ref, o_ref):
    acc = _dot(ya_ref[...].astype(BF16), wa_ref[...])
    acc = acc + _dot(yb_ref[...].astype(BF16), wb_ref[...])
    acc = acc + _dot(yc_ref[...].astype(BF16), wc_ref[...])
    o_ref[...] = x_ref[...] + acc


def _out_proj(x2, ya, yb, yc, w_out_bf16, tm=512):
    m, d = x2.shape
    wa, wb, wc = w_out_bf16[:RW_WIDTH], w_out_bf16[RW_WIDTH:RW_WIDTH + POOL_WIDTH], w_out_bf16[RW_WIDTH + POOL_WIDTH:]
    tile = lambda w: pl.BlockSpec((tm, w), lambda i: (i, 0))
    wfull = lambda w: pl.BlockSpec((w, d), lambda i: (0, 0))
    return pl.pallas_call(
        _out_proj_body,
        grid=(m // tm,),
        in_specs=[tile(d), tile(RW_WIDTH), tile(POOL_WIDTH), tile(NSA_WIDTH),
                  wfull(RW_WIDTH), wfull(POOL_WIDTH), wfull(NSA_WIDTH)],
        out_specs=tile(d),
        out_shape=jax.ShapeDtypeStruct((m, d), F32),
        compiler_params=_cparams(("parallel",)),
        name="out_proj_residual",
    )(x2, ya, yb, yc, wa, wb, wc)


def _gelu_tanh(x):
    return 0.5 * x * (1.0 + jnp.tanh(math.sqrt(2.0 / math.pi) * (x + 0.044715 * (x * x * x))))


def _ffn_body(x_ref, g_ref, wup_ref, cw_ref, cb_ref, wdn_ref, o_ref, carry_ref, *, tiles_per_seq, f_chunk):
    i = pl.program_id(0)
    x = x_ref[...]
    tm = x.shape[0]
    d_ff = wdn_ref.shape[0]
    h = _rms(x, g_ref[...]).astype(BF16)
    first = (i % tiles_per_seq) == 0
    row = lax.broadcasted_iota(jnp.int32, (tm, f_chunk), 0)

    def conv(col0):
        u = _dot(h, wup_ref[:, col0:col0 + f_chunk])
        p2 = jnp.where(first, 0.0, carry_ref[6:7, col0:col0 + f_chunk])
        p1 = jnp.where(first, 0.0, carry_ref[7:8, col0:col0 + f_chunk])
        u1 = jnp.where(row == 0, p1, pltpu.roll(u, 1, axis=0))
        u2 = jnp.where(row == 0, p2, jnp.where(row == 1, p1, pltpu.roll(u, 2, axis=0)))
        carry_ref[:, col0:col0 + f_chunk] = u[tm - 8:tm, :]
        cw = cw_ref[:, col0:col0 + f_chunk]
        return cw[0:1] * u2 + cw[1:2] * u1 + cw[2:3] * u + cb_ref[:, col0:col0 + f_chunk]

    acc = x
    for c0 in range(0, d_ff, f_chunk):
        act = _gelu_tanh(conv(c0)) * conv(d_ff + c0)
        acc = acc + _dot(act.astype(BF16), wdn_ref[c0:c0 + f_chunk, :])
    o_ref[...] = acc


def _conv_ffn(x2, g, w_up_bf16, conv_w, conv_b, w_down_bf16, seq_len, tm=512, f_chunk=1408):
    m, d = x2.shape
    f2 = w_up_bf16.shape[1]
    d_ff = f2 // 2
    assert d_ff % f_chunk == 0 and f_chunk % LANE == 0 and seq_len % tm == 0
    full = lambda shp: pl.BlockSpec(shp, lambda i: (0,) * len(shp))
    return pl.pallas_call(
        functools.partial(_ffn_body, tiles_per_seq=seq_len // tm, f_chunk=f_chunk),
        grid=(m // tm,),
        in_specs=[pl.BlockSpec((tm, d), lambda i: (i, 0)), full((1, d)), full((d, f2)),
                  full((CONV_W, f2)), full((1, f2)), full((d_ff, d))],
        out_specs=pl.BlockSpec((tm, d), lambda i: (i, 0)),
        out_shape=jax.ShapeDtypeStruct((m, d), F32),
        scratch_shapes=[pltpu.VMEM((8, f2), F32)],
        compiler_params=_cparams(("arbitrary",)),
        name="conv_ffn",
    )(x2, g.reshape(1, d), w_up_bf16, conv_w.reshape(CONV_W, f2), conv_b.reshape(1, f2), w_down_bf16)


def _ple_body(x_ref, p_ref, g_ref, wg_ref, wp_ref, gf_ref, o_ref, *, final):
    x = x_ref[...]
    gate = _sigmoid(_dot(_rms(x, g_ref[...]).astype(BF16), wg_ref[...]))
    y = x + _dot(p_ref[...].astype(BF16), wp_ref[...]) * gate
    o_ref[...] = _rms(y, gf_ref[...]) if final else y


def _ple(x2, p2, g, w_gate_bf16, w_proj_bf16, g_final, final, tm=512):
    m, d = x2.shape
    pd = p2.shape[1]
    full = lambda shp: pl.BlockSpec(shp, lambda i: (0,) * len(shp))
    return pl.pallas_call(
        functools.partial(_ple_body, final=final),
        grid=(m // tm,),
        in_specs=[pl.BlockSpec((tm, d), lambda i: (i, 0)), pl.BlockSpec((tm, pd), lambda i: (i, 0)),
                  full((1, d)), full((d, d)), full((pd, d)), full((1, d))],
        out_specs=pl.BlockSpec((tm, d), lambda i: (i, 0)),
        out_shape=jax.ShapeDtypeStruct((m, d), F32),
        compiler_params=_cparams(("parallel",)),
        name="ple_final_norm" if final else "ple",
    )(x2, p2, g.reshape(1, d), w_gate_bf16, w_proj_bf16, g_final.reshape(1, d))


def _overlap_matrix(seq_len):
    ncp = seq_len // CMP_STRIDE
    n_cmp = (seq_len - CMP_BLOCK) // CMP_STRIDE + 1
    cs = CMP_STRIDE * np.arange(n_cmp)
    ss = SEL_BLOCK * np.arange(seq_len // SEL_BLOCK)
    ov = (np.minimum(cs[:, None] + CMP_BLOCK - 1, ss[None] + SEL_BLOCK - 1) - np.maximum(cs[:, None], ss[None]) + 1)
    m = np.zeros((NSEL_PAD, ncp), np.float32)
    m[:ss.size, :n_cmp] = (np.clip(ov, 0, CMP_BLOCK).astype(np.float32) / CMP_BLOCK).T
    return jnp.asarray(m)


def _compress_weight(w):
    halves = w.reshape(2, CMP_STRIDE, HEAD, HEAD)
    eye = jnp.eye(NSA_KV_HEADS, dtype=w.dtype)
    w2 = jnp.einsum('sldf,hg->lhdsgf', halves, eye)
    return w2.reshape(CMP_STRIDE * NSA_KV, 2 * NSA_KV).astype(BF16)


def _compress_pe(pe):
    halves = pe.reshape(2, CMP_STRIDE, 1, HEAD)
    return jnp.broadcast_to(halves, (2, CMP_STRIDE, NSA_KV_HEADS, HEAD)).reshape(2, CMP_STRIDE * NSA_KV)


def _in_proj_layout(w_in):
    d = w_in.shape[0]
    gate0 = RW_IN + POOL_WIDTH + NSA_WIDTH + 6 * NSA_KV
    per_head = NSA_GQA * 3
    gates = jnp.zeros((d, LANE), w_in.dtype)
    for h in range(NSA_KV_HEADS):
        gates = gates.at[:, 16 * h:16 * h + per_head].set(w_in[:, gate0 + per_head * h:gate0 + per_head * (h + 1)])
    return jnp.concatenate([w_in[:, :gate0], gates], axis=1).astype(BF16)


_IN_SPLITS = ((0, RW_IN), (RW_IN, RW_IN + POOL_WIDTH), (RW_IN + POOL_WIDTH, RW_IN + POOL_WIDTH + NSA_WIDTH)) + tuple(
    (RW_IN + POOL_WIDTH + NSA_WIDTH + i * NSA_KV, RW_IN + POOL_WIDTH + NSA_WIDTH + (i + 1) * NSA_KV) for i in range(7))


def kernel(x, p, positions, g_mix, w_in, rw_mu, rw_w0, rw_w_up, rw_a0, rw_a_up, rw_g_up, rw_k_k, rw_k_a, rw_r_k, rw_gn_g, rw_gn_b, pool_w, pool_scale, nsa_pe_k, nsa_pe_v, nsa_w_ck, nsa_w_cv, w_out, g_ffn, ffn_w_up, ffn_conv_w, ffn_conv_b, ffn_w_down, g_ple, ple_w_gate, ple_w_proj, g_final):
    B, T, D = x.shape
    depth = w_in.shape[0]
    M = B * T

    half = HEAD // 2
    inv = ROPE_THETA ** (-jnp.arange(half, dtype=F32) / half)
    ang = positions.astype(F32)[..., None] * inv
    cos, sin = jnp.cos(ang), jnp.sin(ang)
    cos_t = jnp.concatenate([cos, cos] * (LANE // HEAD), axis=-1)
    sin_t = jnp.concatenate([-sin, sin] * (LANE // HEAD), axis=-1)
    mcs_t = _overlap_matrix(T)

    x2 = x.reshape(M, D)
    for i in range(depth):
        zs = _norm_matmul(x2, g_mix[i], _in_proj_layout(w_in[i]), _IN_SPLITS)
        z_rw, z_pool, z_q, z_kc, z_vc, z_ks, z_vs, z_kw, z_vw, z_g = (
            z.reshape(B, T, z.shape[1]) for z in zs)

        pm, qm, rh, y0, gate, bonus = _rw_chunks(z_rw, rw_mu[i], rw_w0[i], rw_w_up[i], rw_a0[i], rw_a_up[i],
                                                 rw_g_up[i], rw_k_k[i], rw_k_a[i], rw_r_k[i].reshape(-1))
        y_a = _rw_scan(pm, qm, rh, y0, gate, bonus, rw_gn_g[i], rw_gn_b[i])

        w_pool_bd = jax.scipy.linalg.block_diag(*[pool_w[i, gi] for gi in range(pool_w.shape[1])]).astype(BF16)
        y_b = _pool(z_pool, w_pool_bd, pool_scale[i])

        q_r, kc_r, ks_r, vs_t, kw_r, vw_t, g_t = _nsa_prep(z_q, z_kc, z_ks, z_vs, z_kw, z_vw, z_g, cos_t, sin_t)
        grp = (B, T // CMP_STRIDE, CMP_STRIDE * NSA_KV)
        kcmp, vcmp_t = _nsa_compress(kc_r.reshape(grp), z_vc.reshape(grp),
                                     _compress_pe(nsa_pe_k[i]), _compress_pe(nsa_pe_v[i]),
                                     _compress_weight(nsa_w_ck[i]), _compress_weight(nsa_w_cv[i]))
        y_c = _nsa_attention(q_r, kcmp, vcmp_t, ks_r, vs_t, kw_r, vw_t, g_t, mcs_t)

        x2 = _out_proj(x2, y_a.reshape(M, -1), y_b.reshape(M, -1), y_c.reshape(M, -1), w_out[i].astype(BF16))
        x2 = _conv_ffn(x2, g_ffn[i], ffn_w_up[i].astype(BF16), ffn_conv_w[i], ffn_conv_b[i],
                       ffn_w_down[i].astype(BF16), T)
        x2 = _ple(x2, p[i].reshape(M, -1), g_ple[i], ple_w_gate[i].astype(BF16), ple_w_proj[i].astype(BF16),
                  g_final, final=(i == depth - 1))
    return x2.reshape(B, T, D)
```

```python
import functools
import math

import numpy as np
import jax
import jax.numpy as jnp
from jax import lax
from jax.experimental import pallas as pl
from jax.experimental.pallas import tpu as pltpu

F32 = jnp.float32
BF16 = jnp.bfloat16

RMS_EPS = 1e-6
ROPE_THETA = 10000.0
HEAD = 64
RW_HEADS = 4
RW_WIDTH = RW_HEADS * HEAD
RW_DECAY_LORA = 64
RW_AAA_LORA = 64
RW_GATE_LORA = 128
RW_GN_EPS = 64e-5
RW_IN = 3 * RW_WIDTH + RW_DECAY_LORA + RW_AAA_LORA + RW_GATE_LORA
POOL_WIDTH = 256
POOL_WINDOWS = (2, 4, 8, 16)
POOL_HALO = 16
NSA_Q_HEADS = 8
NSA_KV_HEADS = 2
NSA_GQA = NSA_Q_HEADS // NSA_KV_HEADS
NSA_WIDTH = NSA_Q_HEADS * HEAD
NSA_KV = NSA_KV_HEADS * HEAD
CMP_BLOCK = 32
CMP_STRIDE = 16
SEL_BLOCK = 64
SEL_TOPN = 16
WINDOW = 512
Q_BLOCK = 128
KEY_GROUP = 512
NSEL_PAD = 128
NEG = -1e30
LOG2E = math.log2(math.e)
FORCE = 1e9
CONV_W = 3
LANE = 128
VMEM_LIMIT = 56 * 1024 * 1024

RW_CHUNK = 64

_HI = lax.Precision.HIGHEST


def _cparams(sem):
    return pltpu.CompilerParams(dimension_semantics=sem, vmem_limit_bytes=VMEM_LIMIT)


def _dot(a, b, prec=None):
    return lax.dot_general(a, b, (((1,), (0,)), ((), ())), precision=prec, preferred_element_type=F32)


def _dot_nt(a, b, prec=None):
    return lax.dot_general(a, b, (((1,), (1,)), ((), ())), precision=prec, preferred_element_type=F32)


def _sigmoid(x):
    return 1.0 / (1.0 + jnp.exp(-x))


def _rms(x, g):
    ms = jnp.mean(x * x, axis=-1, keepdims=True)
    return (x * lax.rsqrt(ms + RMS_EPS)) * g


def _norm_mm_body(x_ref, g_ref, w_ref, *o_refs, splits):
    h = _rms(x_ref[...], g_ref[...]).astype(BF16)
    for o_ref, (lo, hi) in zip(o_refs, splits):
        o_ref[...] = _dot(h, w_ref[:, lo:hi])


def _norm_matmul(x2, g, w_bf16, splits, tm=512):
    m, d = x2.shape
    n = w_bf16.shape[1]
    outs = tuple(jax.ShapeDtypeStruct((m, hi - lo), F32) for lo, hi in splits)
    return pl.pallas_call(
        functools.partial(_norm_mm_body, splits=splits),
        grid=(m // tm,),
        in_specs=[pl.BlockSpec((tm, d), lambda i: (i, 0)),
                  pl.BlockSpec((1, d), lambda i: (0, 0)),
                  pl.BlockSpec((d, n), lambda i: (0, 0))],
        out_specs=tuple(pl.BlockSpec((tm, hi - lo), lambda i: (i, 0)) for lo, hi in splits),
        out_shape=outs,
        compiler_params=_cparams(("parallel",)),
        name="norm_in_proj",
    )(x2, g.reshape(1, d), w_bf16)


def _split(x):
    hi = x.astype(BF16)
    return hi, (x - hi.astype(F32)).astype(BF16)


def _dot3(a, b, dot=_dot):
    return dot(a[0], b[0]) + (dot(a[0], b[1]) + dot(a[1], b[0]))


def _dot_ones(x, ones_bf16):
    hi, lo = _split(x)
    return _dot(hi, ones_bf16) + _dot(lo, ones_bf16)


def _head_ones():
    hr = lax.broadcasted_iota(jnp.int32, (RW_WIDTH, RW_WIDTH), 0) // HEAD
    hc = lax.broadcasted_iota(jnp.int32, (RW_WIDTH, RW_WIDTH), 1) // HEAD
    return hr == hc


def _rw_chunk_body(z_ref, zp_ref, mu_ref, w0_ref, wup_ref, a0_ref, aup_ref, gup_ref, kk_ref, ka_ref,
                   rk_ref, p_ref, q_ref, rh_ref, y0_ref, g_ref, bonus_ref):
    i = pl.program_id(1)
    C = RW_CHUNK
    W = RW_WIDTH
    z = z_ref[0]
    rows = z.shape[0]
    n_chunks = rows // C
    prev = jnp.where(i == 0, 0.0, zp_ref[0, 7:8, :])
    row_z = lax.broadcasted_iota(jnp.int32, z.shape, 0)
    zs = jnp.where(row_z == 0, prev, pltpu.roll(z, 1, axis=0))
    zf = z + (zs - z) * mu_ref[...]
    r = zf[:, 0:W]
    k = zf[:, W:2 * W]
    v = zf[:, 2 * W:3 * W]
    zw = zf[:, 3 * W:3 * W + RW_DECAY_LORA]
    za = zf[:, 3 * W + RW_DECAY_LORA:3 * W + RW_DECAY_LORA + RW_AAA_LORA]
    zg = zf[:, 3 * W + RW_DECAY_LORA + RW_AAA_LORA:]

    lora = lambda x, w_ref: _dot3(_split(x), (w_ref[0], w_ref[1]))
    xw = -(w0_ref[...] + lora(jnp.tanh(zw), wup_ref))
    softplus = jnp.maximum(xw, 0.0) + jnp.log(1.0 + jnp.exp(-jnp.abs(xw)))
    ld = -jnp.exp(-softplus - 0.5)
    a = _sigmoid(a0_ref[...] + lora(za, aup_ref))
    g_ref[0] = lora(_sigmoid(zg), gup_ref)
    kk = k * kk_ref[...]
    k2 = k * (1.0 + (a - 1.0) * ka_ref[...])

    same_head = _head_ones()
    head_ones = jnp.where(same_head, 1.0, 0.0).astype(BF16)
    kkn = kk / jnp.maximum(jnp.sqrt(_dot_ones(kk * kk, head_ones)), 1e-12)
    bonus_ref[0] = _dot_ones(r * k2 * rk_ref[...], head_ones) * v

    t_in = lax.broadcasted_iota(jnp.int32, (rows, W), 0) % C
    lc = ld
    shift = 1
    while shift < C:
        lc = lc + jnp.where(t_in >= shift, pltpu.roll(lc, shift, axis=0), 0.0)
        shift *= 2

    tr = lax.broadcasted_iota(jnp.int32, (W, W), 0) % C
    tc = lax.broadcasted_iota(jnp.int32, (W, W), 1) % C
    lower_incl = tr >= tc
    lower_strict = tr > tc
    eye = jnp.where(same_head, jnp.where(tr == tc, 1.0, 0.0), 0.0)

    def embed(x):
        return jnp.where(same_head, jnp.concatenate([x] * RW_HEADS, axis=0), 0.0)

    A, Bt, Kt, Rt, V, BhT, KhT, gdiag, Rt_raw = [], [], [], [], [], [], [], [], []
    for c in range(n_chunks):
        rs = slice(c * C, (c + 1) * C)
        lc_c, ld_c = lc[rs], ld[rs]
        lc_end = lc_c[C - 1:C, :]
        ginv = jnp.exp(-lc_c)
        dec_end = jnp.exp(lc_end - lc_c)
        kkn_c, a_c, k2_c = kkn[rs], a[rs], k2[rs]
        r_t = r[rs] * jnp.exp(lc_c)
        A.append(_split(embed(-kkn_c * jnp.exp(lc_c - ld_c))))
        Bt.append(_split(embed(kkn_c * a_c * ginv)))
        Kt.append(_split(embed(k2_c * ginv)))
        Rt.append(_split(embed(r_t)))
        Rt_raw.append(r_t)
        V.append(_split(embed(v[rs])))
        BhT.append(_split(embed(kkn_c * a_c * dec_end).T))
        KhT.append(_split(embed(k2_c * dec_end).T))
        gdiag.append(jnp.where(eye > 0.5, jnp.broadcast_to(jnp.exp(lc_end), (W, W)), 0.0))

    each = lambda f, *lists: [f(*xs) for xs in zip(*lists)]
    nt = lambda x, y: _dot3(x, y, _dot_nt)
    L = each(lambda x, y: jnp.where(lower_strict, nt(x, y), 0.0), A, Bt)
    Lak = each(lambda x, y: _split(jnp.where(lower_strict, nt(x, y), 0.0)), A, Kt)
    Mrb = each(lambda x, y: _split(jnp.where(lower_incl, nt(x, y), 0.0)), Rt, Bt)
    Mrk = each(lambda x, y: _split(jnp.where(lower_incl, nt(x, y), 0.0)), Rt, Kt)
    Tm = [eye + x for x in L]
    Pw = each(_split, L)
    for _ in range(int(math.log2(C)) - 1):
        Pw = each(lambda x: _split(_dot3(x, x)), Pw)
        Tm = each(lambda t, x: t + _dot3(_split(t), x), Tm, Pw)
    Tm = each(_split, Tm)
    Wm = each(lambda t, x: _split(_dot3(t, x)), Tm, A)
    LakV = each(lambda x, y: _split(_dot3(x, y)), Lak, V)
    U0 = each(lambda t, x: _split(_dot3(t, x)), Tm, LakV)
    Pm = each(lambda d, x, y: d + _dot3(x, y), gdiag, BhT, Wm)
    Qm = each(lambda x, y, s, t: _dot3(x, y) + _dot3(s, t), BhT, U0, KhT, V)
    Rh = each(lambda x, y: _dot3(x, y), Mrb, Wm)
    Y0 = each(lambda x, y, s, t: _dot3(x, y) + _dot3(s, t), Mrb, U0, Mrk, V)

    def flatten(x_bd):
        out = x_bd[0:C]
        for h in range(1, RW_HEADS):
            out = out + x_bd[h * C:(h + 1) * C]
        return out

    for c in range(n_chunks):
        rs = slice(c * C, (c + 1) * C)
        p_hi, p_lo = _split(Pm[c])
        p_ref[0, c, 0] = p_hi
        p_ref[0, c, 1] = p_lo
        q_ref[0, c] = Qm[c]
        rh_ref[0, rs, :] = Rt_raw[c] + flatten(Rh[c])
        y0_ref[0, rs, :] = flatten(Y0[c])


def _rw_chunks(z_rw, mu, w0, w_up, a0, a_up, g_up, k_k, k_a, r_k, chunks_per_step=2):
    B, T, _ = z_rw.shape
    C = RW_CHUNK
    assert C == HEAD, "the chunk build shares one index grid between time and head-dim masks"
    rows = C * chunks_per_step
    nc = T // C
    row = lambda a: a.reshape(1, -1)
    pair = lambda w: jnp.stack(_split(w))
    full = lambda shp: pl.BlockSpec(shp, lambda b, i: (0,) * len(shp))
    seq = lambda w: pl.BlockSpec((1, rows, w), lambda b, i: (b, i, 0))
    sds = jax.ShapeDtypeStruct
    seq_shape = sds((B, T, RW_WIDTH), F32)
    return pl.pallas_call(
        _rw_chunk_body,
        grid=(B, T // rows),
        in_specs=[seq(RW_IN),
                  pl.BlockSpec((1, 8, RW_IN), lambda b, i: (b, jnp.maximum(i * (rows // 8) - 1, 0), 0)),
                  full((1, RW_IN)), full((1, RW_WIDTH)), full((2, RW_DECAY_LORA, RW_WIDTH)),
                  full((1, RW_WIDTH)), full((2, RW_AAA_LORA, RW_WIDTH)), full((2, RW_GATE_LORA, RW_WIDTH)),
                  full((1, RW_WIDTH)), full((1, RW_WIDTH)), full((1, RW_WIDTH))],
        out_specs=(pl.BlockSpec((1, chunks_per_step, 2, RW_WIDTH, RW_WIDTH), lambda b, i: (b, i, 0, 0, 0)),
                   pl.BlockSpec((1, chunks_per_step, RW_WIDTH, RW_WIDTH), lambda b, i: (b, i, 0, 0)),
                   seq(RW_WIDTH), seq(RW_WIDTH), seq(RW_WIDTH), seq(RW_WIDTH)),
        out_shape=(sds((B, nc, 2, RW_WIDTH, RW_WIDTH), BF16), sds((B, nc, RW_WIDTH, RW_WIDTH), F32),
                   seq_shape, seq_shape, seq_shape, seq_shape),
        compiler_params=_cparams(("parallel", "parallel")),
        name="rwkv_chunk_build",
    )(z_rw, z_rw, row(mu), row(w0), pair(w_up), row(a0), pair(a_up), pair(g_up), row(k_k), row(k_a), row(r_k))


def _rw_scan_body(p_ref, q_ref, rh_ref, y0_ref, g_ref, bonus_ref, gng_ref, gnb_ref, o_ref, h_ref):
    @pl.when(pl.program_id(0) == 0)
    def _():
        h_ref[...] = jnp.zeros_like(h_ref)

    nb = h_ref.shape[0]
    head_ones = jnp.where(_head_ones(), 1.0, 0.0).astype(BF16)
    hs = [_split(h_ref[b]) for b in range(nb)]
    ys = [_dot3(_split(rh_ref[b]), hs[b]) + y0_ref[b] for b in range(nb)]
    for b in range(nb):
        h_ref[b] = _dot3((p_ref[b, 0, 0], p_ref[b, 0, 1]), hs[b]) + q_ref[b, 0]
    inv_n = 1.0 / HEAD
    for b in range(nb):
        y = ys[b]
        d = y - _dot_ones(y, head_ones) * inv_n
        var = _dot_ones(d * d, head_ones) * inv_n
        yn = d * lax.rsqrt(var + RW_GN_EPS) * gng_ref[...] + gnb_ref[...]
        o_ref[b] = (yn + bonus_ref[b]) * g_ref[b]


def _rw_scan(p, q, rh, y0, g, bonus, gn_g, gn_b):
    B, T, _ = rh.shape
    C = RW_CHUNK
    nc = T // C
    seq = pl.BlockSpec((B, C, RW_WIDTH), lambda c: (0, c, 0))
    vec = pl.BlockSpec((1, RW_WIDTH), lambda c: (0, 0))
    return pl.pallas_call(
        _rw_scan_body,
        grid=(nc,),
        in_specs=[pl.BlockSpec((B, 1, 2, RW_WIDTH, RW_WIDTH), lambda c: (0, c, 0, 0, 0)),
                  pl.BlockSpec((B, 1, RW_WIDTH, RW_WIDTH), lambda c: (0, c, 0, 0)),
                  seq, seq, seq, seq, vec, vec],
        out_specs=seq,
        out_shape=jax.ShapeDtypeStruct((B, T, RW_WIDTH), F32),
        scratch_shapes=[pltpu.VMEM((B, RW_WIDTH, RW_WIDTH), F32)],
        compiler_params=_cparams(("arbitrary",)),
        name="rwkv_chunk_scan",
    )(p, q, rh, y0, g, bonus, gn_g.reshape(1, -1), gn_b.reshape(1, -1))


def _pool_body(z_ref, zp_ref, w_ref, scale_ref, o_ref):
    i = pl.program_id(1)
    z = z_ref[0]
    tt = z.shape[0]
    halo = jnp.where(i == 0, 0.0, zp_ref[0])
    e = jnp.concatenate([halo, z], axis=0)
    sums = []
    s = e
    for shift in (1, 2, 4, 8):
        s = s + pltpu.roll(s, shift, axis=0)
        sums.append(s[POOL_HALO:, :])
    t_idx = i * tt + lax.broadcasted_iota(jnp.int32, (tt, POOL_WIDTH), 0)
    lane_group = lax.broadcasted_iota(jnp.int32, (tt, POOL_WIDTH), 1) // HEAD
    pooled = jnp.zeros_like(z)
    for gi, win in enumerate(POOL_WINDOWS):
        cnt = jnp.minimum(t_idx + 1, win).astype(F32)
        pooled = jnp.where(lane_group == gi, sums[gi] / cnt - z, pooled)
    o_ref[0] = _dot(pooled.astype(BF16), w_ref[...]) * scale_ref[...]


def _pool(z_pool, w_blockdiag_bf16, scale, tt=512):
    B, T, _ = z_pool.shape
    return pl.pallas_call(
        _pool_body,
        grid=(B, T // tt),
        in_specs=[pl.BlockSpec((1, tt, POOL_WIDTH), lambda b, i: (b, i, 0)),
                  pl.BlockSpec((1, POOL_HALO, POOL_WIDTH),
                               lambda b, i: (b, jnp.maximum(i * (tt // POOL_HALO) - 1, 0), 0)),
                  pl.BlockSpec((POOL_WIDTH, POOL_WIDTH), lambda b, i: (0, 0)),
                  pl.BlockSpec((1, POOL_WIDTH), lambda b, i: (0, 0))],
        out_specs=pl.BlockSpec((1, tt, POOL_WIDTH), lambda b, i: (b, i, 0)),
        out_shape=jax.ShapeDtypeStruct((B, T, POOL_WIDTH), F32),
        compiler_params=_cparams(("parallel", "parallel")),
        name="pool_mixer",
    )(z_pool, z_pool, w_blockdiag_bf16, scale.reshape(1, -1))


def _rope(x, cos_w, sin_w):
    w = x.shape[1]
    lane = lax.broadcasted_iota(jnp.int32, x.shape, 1) % HEAD
    partner = jnp.where(lane < HEAD // 2, pltpu.roll(x, w - HEAD // 2, axis=1), pltpu.roll(x, HEAD // 2, axis=1))
    return x * cos_w + partner * sin_w


def _head_lanes(a, h, dtype):
    part = a[:, h * HEAD:(h + 1) * HEAD]
    return jnp.concatenate([part, jnp.zeros_like(part)], axis=1).astype(dtype)


def _nsa_prep_body(zq_ref, zkc_ref, zks_ref, zvs_ref, zkw_ref, zvw_ref, zg_ref, cos_ref, sin_ref,
                   q_ref, kc_ref, ks_ref, vst_ref, kw_ref, vwt_ref, gt_ref):
    cos = cos_ref[0]
    sin = sin_ref[0]
    cos_q = jnp.concatenate([cos] * (NSA_WIDTH // LANE), axis=1)
    sin_q = jnp.concatenate([sin] * (NSA_WIDTH // LANE), axis=1)
    qr = _rope(zq_ref[0], cos_q, sin_q) * (HEAD ** -0.5 * LOG2E)
    for h in range(NSA_Q_HEADS):
        q_ref[0, h] = _head_lanes(qr, h, BF16)
    kc_ref[0] = _rope(zkc_ref[0], cos, sin)
    ks = _rope(zks_ref[0], cos, sin)
    kw = _rope(zkw_ref[0], cos, sin)
    vst = zvs_ref[0].T
    vwt = zvw_ref[0].T
    for h in range(NSA_KV_HEADS):
        sl = slice(h * HEAD, (h + 1) * HEAD)
        ks_ref[0, h] = _head_lanes(ks, h, BF16)
        kw_ref[0, h] = _head_lanes(kw, h, BF16)
        vst_ref[0, h] = vst[sl, :].astype(BF16)
        vwt_ref[0, h] = vwt[sl, :].astype(BF16)
    gt = _sigmoid(zg_ref[0]).T
    for h in range(NSA_KV_HEADS):
        gt_ref[0, h] = gt[16 * h:16 * (h + 1), :]


def _nsa_prep(z_q, z_kc, z_ks, z_vs, z_kw, z_vw, z_g, cos_t, sin_t, tt=512):
    B, T, _ = z_q.shape
    seq = lambda w: pl.BlockSpec((1, tt, w), lambda b, i: (b, i, 0))
    hm = lambda nh: pl.BlockSpec((1, nh, tt, LANE), lambda b, i: (b, 0, i, 0))
    tr = lambda rows: pl.BlockSpec((1, NSA_KV_HEADS, rows, tt), lambda b, i: (b, 0, 0, i))
    sds = jax.ShapeDtypeStruct
    return pl.pallas_call(
        _nsa_prep_body,
        grid=(B, T // tt),
        in_specs=[seq(NSA_WIDTH)] + [seq(NSA_KV)] * 6 + [seq(LANE), seq(LANE)],
        out_specs=(hm(NSA_Q_HEADS), seq(NSA_KV), hm(NSA_KV_HEADS), tr(HEAD), hm(NSA_KV_HEADS), tr(HEAD), tr(16)),
        out_shape=(sds((B, NSA_Q_HEADS, T, LANE), BF16), sds((B, T, NSA_KV), F32),
                   sds((B, NSA_KV_HEADS, T, LANE), BF16), sds((B, NSA_KV_HEADS, HEAD, T), BF16),
                   sds((B, NSA_KV_HEADS, T, LANE), BF16), sds((B, NSA_KV_HEADS, HEAD, T), BF16),
                   sds((B, NSA_KV_HEADS, 16, T), F32)),
        compiler_params=_cparams(("parallel", "parallel")),
        name="nsa_prep",
    )(z_q, z_kc, z_ks, z_vs, z_kw, z_vw, z_g, cos_t, sin_t)


def _nsa_cmp_body(gk_ref, gv_ref, pek_ref, pev_ref, wk_ref, wv_ref, kc_ref, vct_ref):
    half = NSA_KV

    def compress(g, pe_ref, w_ref):
        n = g.shape[0]
        lo = _dot((g + pe_ref[0:1, :]).astype(BF16), w_ref[:, 0:half])
        hi = _dot((g + pe_ref[1:2, :]).astype(BF16), w_ref[:, half:2 * half])
        hi_next = pltpu.roll(hi, n - 1, axis=0)
        row = lax.broadcasted_iota(jnp.int32, lo.shape, 0)
        return lo + jnp.where(row == n - 1, 0.0, hi_next)

    kc = compress(gk_ref[0], pek_ref, wk_ref)
    vct = compress(gv_ref[0], pev_ref, wv_ref).T
    for h in range(NSA_KV_HEADS):
        sl = slice(h * HEAD, (h + 1) * HEAD)
        kc_ref[0, h] = _head_lanes(kc, h, BF16)
        vct_ref[0, h] = vct[sl, :].astype(BF16)


def _nsa_compress(kc_groups, vc_groups, pe_k2, pe_v2, wk2, wv2):
    B, ncp, gw = kc_groups.shape
    grp = pl.BlockSpec((1, ncp, gw), lambda b: (b, 0, 0))
    pe = pl.BlockSpec((2, gw), lambda b: (0, 0))
    wspec = pl.BlockSpec((gw, 2 * NSA_KV), lambda b: (0, 0))
    sds = jax.ShapeDtypeStruct
    return pl.pallas_call(
        _nsa_cmp_body,
        grid=(B,),
        in_specs=[grp, grp, pe, pe, wspec, wspec],
        out_specs=(pl.BlockSpec((1, NSA_KV_HEADS, ncp, LANE), lambda b: (b, 0, 0, 0)),
                   pl.BlockSpec((1, NSA_KV_HEADS, HEAD, ncp), lambda b: (b, 0, 0, 0))),
        out_shape=(sds((B, NSA_KV_HEADS, ncp, LANE), BF16), sds((B, NSA_KV_HEADS, HEAD, ncp), BF16)),
        compiler_params=_cparams(("parallel",)),
        name="nsa_compress",
    )(kc_groups, vc_groups, pe_k2, pe_v2, wk2, wv2)


def _nsa_attn_body(q_ref, kc_ref, vct_ref, ks_ref, vst_ref, kw_ref, vwt_ref, gt_ref, mcs_ref, blk_ref,
                   o_ref, val_ref):
    jb = pl.program_id(2)
    s0 = jb * Q_BLOCK
    cols = NSA_GQA * Q_BLOCK
    q = q_ref[0].reshape(cols, LANE)
    ncp = kc_ref.shape[2]

    sc = _dot_nt(kc_ref[0, 0], q)
    n_idx = lax.broadcasted_iota(jnp.int32, (ncp, cols), 0)
    t_c = s0 + lax.broadcasted_iota(jnp.int32, (ncp, cols), 1) % Q_BLOCK
    sc = jnp.where(CMP_STRIDE * n_idx + (CMP_BLOCK - 1) <= t_c, sc, NEG)
    e = jnp.exp2(sc - jnp.max(sc, axis=0, keepdims=True))
    any_valid = jnp.where(t_c[0:1, :] >= CMP_BLOCK - 1, 1.0, 0.0)
    pc = e * (any_valid / jnp.sum(e, axis=0, keepdims=True))
    o_cmp = _dot(vct_ref[0, 0], pc.astype(BF16))
    psum = pc[:, 0:Q_BLOCK]
    for g in range(1, NSA_GQA):
        psum = psum + pc[:, g * Q_BLOCK:(g + 1) * Q_BLOCK]
    imp = _dot(mcs_ref[...], psum, _HI)

    j_idx = lax.broadcasted_iota(jnp.int32, (NSEL_PAD, Q_BLOCK), 0)
    jc = (s0 + lax.broadcasted_iota(jnp.int32, (NSEL_PAD, Q_BLOCK), 1)) // SEL_BLOCK
    causal = j_idx <= jc
    forced_imp = jnp.where(j_idx == 0, FORCE, jnp.where(j_idx == jc, FORCE, jnp.where(j_idx == jc - 1, FORCE, imp)))
    val = jnp.where(causal, forced_imp, NEG)
    val_ref[...] = val
    n_causal_blocks = (s0 + Q_BLOCK) // SEL_BLOCK

    def rank_step(jp, cnt):
        other = jnp.broadcast_to(val_ref[pl.ds(jp, 1), :], (NSEL_PAD, Q_BLOCK))
        ahead = jnp.where(other > val, 1.0, jnp.where(other == val, jnp.where(j_idx > jp, 1.0, 0.0), 0.0))
        return cnt + ahead

    rank = lax.fori_loop(0, n_causal_blocks, rank_step, jnp.zeros((NSEL_PAD, Q_BLOCK), F32))
    bias = jnp.where(causal, jnp.where(rank < SEL_TOPN, 0.0, NEG), NEG)

    bias_q = bias.T.astype(BF16)
    q_aug = jnp.concatenate([q, jnp.concatenate([bias_q] * NSA_GQA, axis=0)], axis=1)
    t_k = s0 + lax.broadcasted_iota(jnp.int32, (KEY_GROUP, cols), 1) % Q_BLOCK
    k_off = lax.broadcasted_iota(jnp.int32, (KEY_GROUP, cols), 0)

    def sel_scores(off):
        k_aug = jnp.concatenate([ks_ref[0, 0, pl.ds(off, KEY_GROUP), :], blk_ref[pl.ds(off, KEY_GROUP), :]], axis=1)
        return _dot_nt(k_aug, q_aug)

    def flash_update(v_t, st, carry):
        m, l, acc = carry
        m_new = jnp.maximum(m, jnp.max(st, axis=0, keepdims=True))
        alpha = jnp.exp2(m - m_new)
        p = jnp.exp2(st - m_new)
        l = l * alpha + jnp.sum(p, axis=0, keepdims=True)
        acc = acc * alpha + _dot(v_t, p.astype(BF16))
        return m_new, l, acc

    def sel_full_group(g, carry):
        off = pl.multiple_of(g * KEY_GROUP, KEY_GROUP)
        return flash_update(vst_ref[0, 0, :, pl.ds(off, KEY_GROUP)], sel_scores(off), carry)

    init = (jnp.full((1, cols), NEG, F32), jnp.zeros((1, cols), F32), jnp.zeros((HEAD, cols), F32))
    n_full = s0 // KEY_GROUP

    def sel_group_pair(i, carries):
        off_a = pl.multiple_of(2 * i * KEY_GROUP, KEY_GROUP)
        off_b = pl.multiple_of((2 * i + 1) * KEY_GROUP, KEY_GROUP)
        st_a = sel_scores(off_a)
        st_b = sel_scores(off_b)
        return (flash_update(vst_ref[0, 0, :, pl.ds(off_a, KEY_GROUP)], st_a, carries[0]),
                flash_update(vst_ref[0, 0, :, pl.ds(off_b, KEY_GROUP)], st_b, carries[1]))

    c_even, c_odd = lax.fori_loop(0, n_full // 2, sel_group_pair, (init, init))
    c_even = lax.fori_loop(2 * (n_full // 2), n_full, sel_full_group, c_even)
    off_d = pl.multiple_of(n_full * KEY_GROUP, KEY_GROUP)
    st_d = jnp.where(off_d + k_off <= t_k, sel_scores(off_d), NEG)
    m_e, l_e, acc_e = flash_update(vst_ref[0, 0, :, pl.ds(off_d, KEY_GROUP)], st_d, c_even)
    m_o, l_o, acc_o = c_odd
    m_s = jnp.maximum(m_e, m_o)
    w_e = jnp.exp2(m_e - m_s)
    w_o = jnp.exp2(m_o - m_s)
    l_s = l_e * w_e + l_o * w_o
    acc_s = acc_e * w_e + acc_o * w_o

    wkeys = WINDOW + Q_BLOCK
    off_w = pl.multiple_of(jnp.maximum(s0 - WINDOW, 0), Q_BLOCK)
    sw = _dot_nt(kw_ref[0, 0, pl.ds(off_w, wkeys), :], q)
    kpos = off_w + lax.broadcasted_iota(jnp.int32, (wkeys, cols), 0)
    t_w = s0 + lax.broadcasted_iota(jnp.int32, (wkeys, cols), 1) % Q_BLOCK
    sw = jnp.where(kpos <= t_w, jnp.where(kpos > t_w - WINDOW, sw, NEG), NEG)
    pw = jnp.exp2(sw - jnp.max(sw, axis=0, keepdims=True))
    l_w = jnp.sum(pw, axis=0, keepdims=True)
    acc_w = _dot(vwt_ref[0, 0, :, pl.ds(off_w, wkeys)], pw.astype(BF16))

    def gate_row(branch):
        return jnp.concatenate([gt_ref[0, 0, 3 * g + branch:3 * g + branch + 1, :] for g in range(NSA_GQA)], axis=1)

    o_t = gate_row(0) * o_cmp + (gate_row(1) / l_s) * acc_s + (gate_row(2) / l_w) * acc_w
    stacked = jnp.concatenate([o_t[:, g * Q_BLOCK:(g + 1) * Q_BLOCK] for g in range(NSA_GQA)], axis=0)
    o_ref[0] = stacked.T


def _nsa_attention(q_r, kcmp, vcmp_t, ks_r, vs_t, kw_r, vw_t, g_t, mcs_t, blk_onehot):
    B, _, T, _ = q_r.shape
    ncp = kcmp.shape[2]
    nqb = T // Q_BLOCK
    assert T % KEY_GROUP == 0 and T >= WINDOW + Q_BLOCK and T // SEL_BLOCK <= NSEL_PAD
    kv_rows = lambda n: pl.BlockSpec((1, 1, n, LANE), lambda b, h, j: (b, h, 0, 0))
    kv_cols = lambda n: pl.BlockSpec((1, 1, HEAD, n), lambda b, h, j: (b, h, 0, 0))
    return pl.pallas_call(
        _nsa_attn_body,
        grid=(B, NSA_KV_HEADS, nqb),
        in_specs=[pl.BlockSpec((1, NSA_GQA, Q_BLOCK, LANE), lambda b, h, j: (b, h, j, 0)),
                  kv_rows(ncp), kv_cols(ncp), kv_rows(T), kv_cols(T), kv_rows(T), kv_cols(T),
                  pl.BlockSpec((1, 1, 16, Q_BLOCK), lambda b, h, j: (b, h, 0, j)),
                  pl.BlockSpec((NSEL_PAD, ncp), lambda b, h, j: (0, 0)),
                  pl.BlockSpec((T, NSEL_PAD), lambda b, h, j: (0, 0))],
        out_specs=pl.BlockSpec((1, Q_BLOCK, NSA_GQA * HEAD), lambda b, h, j: (b, j, h)),
        out_shape=jax.ShapeDtypeStruct((B, T, NSA_WIDTH), F32),
        scratch_shapes=[pltpu.VMEM((NSEL_PAD, Q_BLOCK), F32)],
        compiler_params=_cparams(("parallel", "parallel", "arbitrary")),
        name="nsa_attention",
    )(q_r, kcmp, vcmp_t, ks_r, vs_t, kw_r, vw_t, g_t, mcs_t, blk_onehot)


def _out_proj_body(x_ref, ya_ref, yb_ref, yc_ref, wa_ref, wb_ref, wc_ref, o_ref):
    acc = _dot(ya_ref[...].astype(BF16), wa_ref[...])
    acc = acc + _dot(yb_ref[...].astype(BF16), wb_ref[...])
    acc = acc + _dot(yc_ref[...].astype(BF16), wc_ref[...])
    o_ref[...] = x_ref[...] + acc


def _out_proj(x2, ya, yb, yc, w_out_bf16, tm=512):
    m, d = x2.shape
    wa, wb, wc = w_out_bf16[:RW_WIDTH], w_out_bf16[RW_WIDTH:RW_WIDTH + POOL_WIDTH], w_out_bf16[RW_WIDTH + POOL_WIDTH:]
    tile = lambda w: pl.BlockSpec((tm, w), lambda i: (i, 0))
    wfull = lambda w: pl.BlockSpec((w, d), lambda i: (0, 0))
    return pl.pallas_call(
        _out_proj_body,
        grid=(m // tm,),
        in_specs=[tile(d), tile(RW_WIDTH), tile(POOL_WIDTH), tile(NSA_WIDTH),
                  wfull(RW_WIDTH), wfull(POOL_WIDTH), wfull(NSA_WIDTH)],
        out_specs=tile(d),
        out_shape=jax.ShapeDtypeStruct((m, d), F32),
        compiler_params=_cparams(("parallel",)),
        name="out_proj_residual",
    )(x2, ya, yb, yc, wa, wb, wc)


def _gelu_tanh(x):
    return 0.5 * x * (1.0 + jnp.tanh(math.sqrt(2.0 / math.pi) * (x + 0.044715 * (x * x * x))))


def _ffn_body(x_ref, g_ref, wup_ref, cw_ref, cb_ref, wdn_ref, o_ref, carry_ref, *, tiles_per_seq, f_chunk):
    i = pl.program_id(0)
    x = x_ref[...]
    tm = x.shape[0]
    d_ff = wdn_ref.shape[0]
    h = _rms(x, g_ref[...]).astype(BF16)
    first = (i % tiles_per_seq) == 0
    row = lax.broadcasted_iota(jnp.int32, (tm, f_chunk), 0)

    def conv(col0):
        u = _dot(h, wup_ref[:, col0:col0 + f_chunk])
        p2 = jnp.where(first, 0.0, carry_ref[6:7, col0:col0 + f_chunk])
        p1 = jnp.where(first, 0.0, carry_ref[7:8, col0:col0 + f_chunk])
        u1 = jnp.where(row == 0, p1, pltpu.roll(u, 1, axis=0))
        u2 = jnp.where(row == 0, p2, jnp.where(row == 1, p1, pltpu.roll(u, 2, axis=0)))
        carry_ref[:, col0:col0 + f_chunk] = u[tm - 8:tm, :]
        cw = cw_ref[:, col0:col0 + f_chunk]
        return cw[0:1] * u2 + cw[1:2] * u1 + cw[2:3] * u + cb_ref[:, col0:col0 + f_chunk]

    acc = x
    for c0 in range(0, d_ff, f_chunk):
        act = _gelu_tanh(conv(c0)) * conv(d_ff + c0)
        acc = acc + _dot(act.astype(BF16), wdn_ref[c0:c0 + f_chunk, :])
    o_ref[...] = acc


def _conv_ffn(x2, g, w_up_bf16, conv_w, conv_b, w_down_bf16, seq_len, tm=512, f_chunk=1408):
    m, d = x2.shape
    f2 = w_up_bf16.shape[1]
    d_ff = f2 // 2
    assert d_ff % f_chunk == 0 and f_chunk % LANE == 0 and seq_len % tm == 0
    full = lambda shp: pl.BlockSpec(shp, lambda i: (0,) * len(shp))
    return pl.pallas_call(
        functools.partial(_ffn_body, tiles_per_seq=seq_len // tm, f_chunk=f_chunk),
        grid=(m // tm,),
        in_specs=[pl.BlockSpec((tm, d), lambda i: (i, 0)), full((1, d)), full((d, f2)),
                  full((CONV_W, f2)), full((1, f2)), full((d_ff, d))],
        out_specs=pl.BlockSpec((tm, d), lambda i: (i, 0)),
        out_shape=jax.ShapeDtypeStruct((m, d), F32),
        scratch_shapes=[pltpu.VMEM((8, f2), F32)],
        compiler_params=_cparams(("arbitrary",)),
        name="conv_ffn",
    )(x2, g.reshape(1, d), w_up_bf16, conv_w.reshape(CONV_W, f2), conv_b.reshape(1, f2), w_down_bf16)


def _ple_body(x_ref, p_ref, g_ref, wg_ref, wp_ref, gf_ref, o_ref, *, final):
    x = x_ref[...]
    gate = _sigmoid(_dot(_rms(x, g_ref[...]).astype(BF16), wg_ref[...]))
    y = x + _dot(p_ref[...].astype(BF16), wp_ref[...]) * gate
    o_ref[...] = _rms(y, gf_ref[...]) if final else y


def _ple(x2, p2, g, w_gate_bf16, w_proj_bf16, g_final, final, tm=512):
    m, d = x2.shape
    pd = p2.shape[1]
    full = lambda shp: pl.BlockSpec(shp, lambda i: (0,) * len(shp))
    return pl.pallas_call(
        functools.partial(_ple_body, final=final),
        grid=(m // tm,),
        in_specs=[pl.BlockSpec((tm, d), lambda i: (i, 0)), pl.BlockSpec((tm, pd), lambda i: (i, 0)),
                  full((1, d)), full((d, d)), full((pd, d)), full((1, d))],
        out_specs=pl.BlockSpec((tm, d), lambda i: (i, 0)),
        out_shape=jax.ShapeDtypeStruct((m, d), F32),
        compiler_params=_cparams(("parallel",)),
        name="ple_final_norm" if final else "ple",
    )(x2, p2, g.reshape(1, d), w_gate_bf16, w_proj_bf16, g_final.reshape(1, d))


def _overlap_matrix(seq_len):
    ncp = seq_len // CMP_STRIDE
    n_cmp = (seq_len - CMP_BLOCK) // CMP_STRIDE + 1
    cs = CMP_STRIDE * np.arange(n_cmp)
    ss = SEL_BLOCK * np.arange(seq_len // SEL_BLOCK)
    ov = (np.minimum(cs[:, None] + CMP_BLOCK - 1, ss[None] + SEL_BLOCK - 1) - np.maximum(cs[:, None], ss[None]) + 1)
    m = np.zeros((NSEL_PAD, ncp), np.float32)
    m[:ss.size, :n_cmp] = (np.clip(ov, 0, CMP_BLOCK).astype(np.float32) / CMP_BLOCK).T
    return jnp.asarray(m)


def _compress_weight(w):
    halves = w.reshape(2, CMP_STRIDE, HEAD, HEAD)
    eye = jnp.eye(NSA_KV_HEADS, dtype=w.dtype)
    w2 = jnp.einsum('sldf,hg->lhdsgf', halves, eye)
    return w2.reshape(CMP_STRIDE * NSA_KV, 2 * NSA_KV).astype(BF16)


def _compress_pe(pe):
    halves = pe.reshape(2, CMP_STRIDE, 1, HEAD)
    return jnp.broadcast_to(halves, (2, CMP_STRIDE, NSA_KV_HEADS, HEAD)).reshape(2, CMP_STRIDE * NSA_KV)


def _in_proj_layout(w_in):
    d = w_in.shape[0]
    gate0 = RW_IN + POOL_WIDTH + NSA_WIDTH + 6 * NSA_KV
    per_head = NSA_GQA * 3
    gates = jnp.zeros((d, LANE), w_in.dtype)
    for h in range(NSA_KV_HEADS):
        gates = gates.at[:, 16 * h:16 * h + per_head].set(w_in[:, gate0 + per_head * h:gate0 + per_head * (h + 1)])
    return jnp.concatenate([w_in[:, :gate0], gates], axis=1).astype(BF16)


_IN_SPLITS = ((0, RW_IN), (RW_IN, RW_IN + POOL_WIDTH), (RW_IN + POOL_WIDTH, RW_IN + POOL_WIDTH + NSA_WIDTH)) + tuple(
    (RW_IN + POOL_WIDTH + NSA_WIDTH + i * NSA_KV, RW_IN + POOL_WIDTH + NSA_WIDTH + (i + 1) * NSA_KV) for i in range(7))


def kernel(x, p, positions, g_mix, w_in, rw_mu, rw_w0, rw_w_up, rw_a0, rw_a_up, rw_g_up, rw_k_k, rw_k_a, rw_r_k, rw_gn_g, rw_gn_b, pool_w, pool_scale, nsa_pe_k, nsa_pe_v, nsa_w_ck, nsa_w_cv, w_out, g_ffn, ffn_w_up, ffn_conv_w, ffn_conv_b, ffn_w_down, g_ple, ple_w_gate, ple_w_proj, g_final):
    B, T, D = x.shape
    depth = w_in.shape[0]
    M = B * T

    half = HEAD // 2
    inv = ROPE_THETA ** (-jnp.arange(half, dtype=F32) / half)
    ang = positions.astype(F32)[..., None] * inv
    cos, sin = jnp.cos(ang), jnp.sin(ang)
    cos_t = jnp.concatenate([cos, cos] * (LANE // HEAD), axis=-1)
    sin_t = jnp.concatenate([-sin, sin] * (LANE // HEAD), axis=-1)
    mcs_t = _overlap_matrix(T)
    blk_onehot = (jnp.arange(T)[:, None] // SEL_BLOCK == jnp.arange(NSEL_PAD)[None, :]).astype(BF16)

    x2 = x.reshape(M, D)
    for i in range(depth):
        zs = _norm_matmul(x2, g_mix[i], _in_proj_layout(w_in[i]), _IN_SPLITS)
        z_rw, z_pool, z_q, z_kc, z_vc, z_ks, z_vs, z_kw, z_vw, z_g = (
            z.reshape(B, T, z.shape[1]) for z in zs)

        pm, qm, rh, y0, gate, bonus = _rw_chunks(z_rw, rw_mu[i], rw_w0[i], rw_w_up[i], rw_a0[i], rw_a_up[i],
                                                 rw_g_up[i], rw_k_k[i], rw_k_a[i], rw_r_k[i].reshape(-1))
        y_a = _rw_scan(pm, qm, rh, y0, gate, bonus, rw_gn_g[i], rw_gn_b[i])

        w_pool_bd = jax.scipy.linalg.block_diag(*[pool_w[i, gi] for gi in range(pool_w.shape[1])]).astype(BF16)
        y_b = _pool(z_pool, w_pool_bd, pool_scale[i])

        q_r, kc_r, ks_r, vs_t, kw_r, vw_t, g_t = _nsa_prep(z_q, z_kc, z_ks, z_vs, z_kw, z_vw, z_g, cos_t, sin_t)
        grp = (B, T // CMP_STRIDE, CMP_STRIDE * NSA_KV)
        kcmp, vcmp_t = _nsa_compress(kc_r.reshape(grp), z_vc.reshape(grp),
                                     _compress_pe(nsa_pe_k[i]), _compress_pe(nsa_pe_v[i]),
                                     _compress_weight(nsa_w_ck[i]), _compress_weight(nsa_w_cv[i]))
        y_c = _nsa_attention(q_r, kcmp, vcmp_t, ks_r, vs_t, kw_r, vw_t, g_t, mcs_t, blk_onehot)

        x2 = _out_proj(x2, y_a.reshape(M, -1), y_b.reshape(M, -1), y_c.reshape(M, -1), w_out[i].astype(BF16))
        x2 = _conv_ffn(x2, g_ffn[i], ffn_w_up[i].astype(BF16), ffn_conv_w[i], ffn_conv_b[i],
                       ffn_w_down[i].astype(BF16), T)
        x2 = _ple(x2, p[i].reshape(M, -1), g_ple[i], ple_w_gate[i].astype(BF16), ple_w_proj[i].astype(BF16),
                  g_final, final=(i == depth - 1))
    return x2.reshape(B, T, D)
```

```python
import functools
import math

import numpy as np
import jax
import jax.numpy as jnp
from jax import lax
from jax.experimental import pallas as pl
from jax.experimental.pallas import tpu as pltpu

F32 = jnp.float32
BF16 = jnp.bfloat16

RMS_EPS = 1e-6
ROPE_THETA = 10000.0
HEAD = 64
RW_HEADS = 4
RW_WIDTH = RW_HEADS * HEAD
RW_DECAY_LORA = 64
RW_AAA_LORA = 64
RW_GATE_LORA = 128
RW_GN_EPS = 64e-5
RW_IN = 3 * RW_WIDTH + RW_DECAY_LORA + RW_AAA_LORA + RW_GATE_LORA
POOL_WIDTH = 256
POOL_WINDOWS = (2, 4, 8, 16)
POOL_HALO = 16
NSA_Q_HEADS = 8
NSA_KV_HEADS = 2
NSA_GQA = NSA_Q_HEADS // NSA_KV_HEADS
NSA_WIDTH = NSA_Q_HEADS * HEAD
NSA_KV = NSA_KV_HEADS * HEAD
CMP_BLOCK = 32
CMP_STRIDE = 16
SEL_BLOCK = 64
SEL_TOPN = 16
WINDOW = 512
Q_BLOCK = 128
KEY_GROUP = 512
V_ROWS = HEAD + 16
NSEL_PAD = 128
NEG = -1e30
LOG2E = math.log2(math.e)
FORCE = 1e9
CONV_W = 3
LANE = 128
VMEM_LIMIT = 56 * 1024 * 1024

RW_CHUNK = 64

def _cparams(sem):
    return pltpu.CompilerParams(dimension_semantics=sem, vmem_limit_bytes=VMEM_LIMIT)


def _dot(a, b, prec=None):
    return lax.dot_general(a, b, (((1,), (0,)), ((), ())), precision=prec, preferred_element_type=F32)


def _dot_nt(a, b, prec=None):
    return lax.dot_general(a, b, (((1,), (1,)), ((), ())), precision=prec, preferred_element_type=F32)


def _sigmoid(x):
    return 1.0 / (1.0 + jnp.exp(-x))


def _rms(x, g):
    ms = jnp.mean(x * x, axis=-1, keepdims=True)
    return (x * lax.rsqrt(ms + RMS_EPS)) * g


def _norm_mm_body(x_ref, g_ref, w_ref, *o_refs, splits):
    h = _rms(x_ref[...], g_ref[...]).astype(BF16)
    for o_ref, (lo, hi) in zip(o_refs, splits):
        o_ref[...] = _dot(h, w_ref[:, lo:hi])


def _norm_matmul(x2, g, w_bf16, splits, tm=512):
    m, d = x2.shape
    n = w_bf16.shape[1]
    outs = tuple(jax.ShapeDtypeStruct((m, hi - lo), F32) for lo, hi in splits)
    return pl.pallas_call(
        functools.partial(_norm_mm_body, splits=splits),
        grid=(m // tm,),
        in_specs=[pl.BlockSpec((tm, d), lambda i: (i, 0)),
                  pl.BlockSpec((1, d), lambda i: (0, 0)),
                  pl.BlockSpec((d, n), lambda i: (0, 0))],
        out_specs=tuple(pl.BlockSpec((tm, hi - lo), lambda i: (i, 0)) for lo, hi in splits),
        out_shape=outs,
        compiler_params=_cparams(("parallel",)),
        name="norm_in_proj",
    )(x2, g.reshape(1, d), w_bf16)


def _split(x):
    hi = x.astype(BF16)
    return hi, (x - hi.astype(F32)).astype(BF16)


def _dot3(a, b, dot=_dot):
    return dot(a[0], b[0]) + (dot(a[0], b[1]) + dot(a[1], b[0]))


def _dot_ones(x, ones_bf16):
    hi, lo = _split(x)
    return _dot(hi, ones_bf16) + _dot(lo, ones_bf16)


def _head_ones():
    hr = lax.broadcasted_iota(jnp.int32, (RW_WIDTH, RW_WIDTH), 0) // HEAD
    hc = lax.broadcasted_iota(jnp.int32, (RW_WIDTH, RW_WIDTH), 1) // HEAD
    return hr == hc


def _rw_chunk_body(z_ref, zp_ref, mu_ref, w0_ref, wup_ref, a0_ref, aup_ref, gup_ref, kk_ref, ka_ref,
                   rk_ref, p_ref, q_ref, rh_ref, y0_ref, g_ref, bonus_ref):
    i = pl.program_id(1)
    C = RW_CHUNK
    W = RW_WIDTH
    z = z_ref[0]
    rows = z.shape[0]
    n_chunks = rows // C
    prev = jnp.where(i == 0, 0.0, zp_ref[0, 7:8, :])
    row_z = lax.broadcasted_iota(jnp.int32, z.shape, 0)
    zs = jnp.where(row_z == 0, prev, pltpu.roll(z, 1, axis=0))
    zf = z + (zs - z) * mu_ref[...]
    r = zf[:, 0:W]
    k = zf[:, W:2 * W]
    v = zf[:, 2 * W:3 * W]
    zw = zf[:, 3 * W:3 * W + RW_DECAY_LORA]
    za = zf[:, 3 * W + RW_DECAY_LORA:3 * W + RW_DECAY_LORA + RW_AAA_LORA]
    zg = zf[:, 3 * W + RW_DECAY_LORA + RW_AAA_LORA:]

    lora = lambda x, w_ref: _dot3(_split(x), (w_ref[0], w_ref[1]))
    xw = -(w0_ref[...] + lora(jnp.tanh(zw), wup_ref))
    softplus = jnp.maximum(xw, 0.0) + jnp.log(1.0 + jnp.exp(-jnp.abs(xw)))
    ld = -jnp.exp(-softplus - 0.5)
    a = _sigmoid(a0_ref[...] + lora(za, aup_ref))
    g_ref[0] = lora(_sigmoid(zg), gup_ref)
    kk = k * kk_ref[...]
    k2 = k * (1.0 + (a - 1.0) * ka_ref[...])

    same_head = _head_ones()
    head_ones = jnp.where(same_head, 1.0, 0.0).astype(BF16)
    kkn = kk / jnp.maximum(jnp.sqrt(_dot_ones(kk * kk, head_ones)), 1e-12)
    bonus_ref[0] = _dot_ones(r * k2 * rk_ref[...], head_ones) * v

    t_in = lax.broadcasted_iota(jnp.int32, (rows, W), 0) % C
    lc = ld
    shift = 1
    while shift < C:
        lc = lc + jnp.where(t_in >= shift, pltpu.roll(lc, shift, axis=0), 0.0)
        shift *= 2

    tr = lax.broadcasted_iota(jnp.int32, (W, W), 0) % C
    tc = lax.broadcasted_iota(jnp.int32, (W, W), 1) % C
    lower_incl = tr >= tc
    lower_strict = tr > tc
    eye = jnp.where(same_head, jnp.where(tr == tc, 1.0, 0.0), 0.0)

    def embed(x):
        return jnp.where(same_head, jnp.concatenate([x] * RW_HEADS, axis=0), 0.0)

    A, Bt, Kt, Rt, V, BhT, KhT, gdiag, Rt_raw = [], [], [], [], [], [], [], [], []
    for c in range(n_chunks):
        rs = slice(c * C, (c + 1) * C)
        lc_c, ld_c = lc[rs], ld[rs]
        lc_end = lc_c[C - 1:C, :]
        ginv = jnp.exp(-lc_c)
        dec_end = jnp.exp(lc_end - lc_c)
        kkn_c, a_c, k2_c = kkn[rs], a[rs], k2[rs]
        r_t = r[rs] * jnp.exp(lc_c)
        A.append(_split(embed(-kkn_c * jnp.exp(lc_c - ld_c))))
        Bt.append(_split(embed(kkn_c * a_c * ginv)))
        Kt.append(_split(embed(k2_c * ginv)))
        Rt.append(_split(embed(r_t)))
        Rt_raw.append(r_t)
        V.append(_split(embed(v[rs])))
        BhT.append(_split(embed(kkn_c * a_c * dec_end).T))
        KhT.append(_split(embed(k2_c * dec_end).T))
        gdiag.append(jnp.where(eye > 0.5, jnp.broadcast_to(jnp.exp(lc_end), (W, W)), 0.0))

    each = lambda f, *lists: [f(*xs) for xs in zip(*lists)]
    nt = lambda x, y: _dot3(x, y, _dot_nt)
    L = each(lambda x, y: jnp.where(lower_strict, nt(x, y), 0.0), A, Bt)
    Lak = each(lambda x, y: _split(jnp.where(lower_strict, nt(x, y), 0.0)), A, Kt)
    Mrb = each(lambda x, y: _split(jnp.where(lower_incl, nt(x, y), 0.0)), Rt, Bt)
    Mrk = each(lambda x, y: _split(jnp.where(lower_incl, nt(x, y), 0.0)), Rt, Kt)
    Tm = [eye + x for x in L]
    Pw = each(_split, L)
    for _ in range(int(math.log2(C)) - 1):
        Pw = each(lambda x: _split(_dot3(x, x)), Pw)
        Tm = each(lambda t, x: t + _dot3(_split(t), x), Tm, Pw)
    Tm = each(_split, Tm)
    Wm = each(lambda t, x: _split(_dot3(t, x)), Tm, A)
    LakV = each(lambda x, y: _split(_dot3(x, y)), Lak, V)
    U0 = each(lambda t, x: _split(_dot3(t, x)), Tm, LakV)
    Pm = each(lambda d, x, y: d + _dot3(x, y), gdiag, BhT, Wm)
    Qm = each(lambda x, y, s, t: _dot3(x, y) + _dot3(s, t), BhT, U0, KhT, V)
    Rh = each(lambda x, y: _dot3(x, y), Mrb, Wm)
    Y0 = each(lambda x, y, s, t: _dot3(x, y) + _dot3(s, t), Mrb, U0, Mrk, V)

    def flatten(x_bd):
        out = x_bd[0:C]
        for h in range(1, RW_HEADS):
            out = out + x_bd[h * C:(h + 1) * C]
        return out

    for c in range(n_chunks):
        rs = slice(c * C, (c + 1) * C)
        p_hi, p_lo = _split(Pm[c])
        p_ref[0, c, 0] = p_hi
        p_ref[0, c, 1] = p_lo
        q_ref[0, c] = Qm[c]
        rh_ref[0, rs, :] = Rt_raw[c] + flatten(Rh[c])
        y0_ref[0, rs, :] = flatten(Y0[c])


def _rw_chunks(z_rw, mu, w0, w_up, a0, a_up, g_up, k_k, k_a, r_k, chunks_per_step=2):
    B, T, _ = z_rw.shape
    C = RW_CHUNK
    assert C == HEAD, "the chunk build shares one index grid between time and head-dim masks"
    rows = C * chunks_per_step
    nc = T // C
    row = lambda a: a.reshape(1, -1)
    pair = lambda w: jnp.stack(_split(w))
    full = lambda shp: pl.BlockSpec(shp, lambda b, i: (0,) * len(shp))
    seq = lambda w: pl.BlockSpec((1, rows, w), lambda b, i: (b, i, 0))
    sds = jax.ShapeDtypeStruct
    seq_shape = sds((B, T, RW_WIDTH), F32)
    return pl.pallas_call(
        _rw_chunk_body,
        grid=(B, T // rows),
        in_specs=[seq(RW_IN),
                  pl.BlockSpec((1, 8, RW_IN), lambda b, i: (b, jnp.maximum(i * (rows // 8) - 1, 0), 0)),
                  full((1, RW_IN)), full((1, RW_WIDTH)), full((2, RW_DECAY_LORA, RW_WIDTH)),
                  full((1, RW_WIDTH)), full((2, RW_AAA_LORA, RW_WIDTH)), full((2, RW_GATE_LORA, RW_WIDTH)),
                  full((1, RW_WIDTH)), full((1, RW_WIDTH)), full((1, RW_WIDTH))],
        out_specs=(pl.BlockSpec((1, chunks_per_step, 2, RW_WIDTH, RW_WIDTH), lambda b, i: (b, i, 0, 0, 0)),
                   pl.BlockSpec((1, chunks_per_step, RW_WIDTH, RW_WIDTH), lambda b, i: (b, i, 0, 0)),
                   seq(RW_WIDTH), seq(RW_WIDTH), seq(RW_WIDTH), seq(RW_WIDTH)),
        out_shape=(sds((B, nc, 2, RW_WIDTH, RW_WIDTH), BF16), sds((B, nc, RW_WIDTH, RW_WIDTH), F32),
                   seq_shape, seq_shape, seq_shape, seq_shape),
        compiler_params=_cparams(("parallel", "parallel")),
        name="rwkv_chunk_build",
    )(z_rw, z_rw, row(mu), row(w0), pair(w_up), row(a0), pair(a_up), pair(g_up), row(k_k), row(k_a), row(r_k))


def _rw_scan_body(p_ref, q_ref, rh_ref, y0_ref, g_ref, bonus_ref, gng_ref, gnb_ref, o_ref, h_ref):
    @pl.when(pl.program_id(0) == 0)
    def _():
        h_ref[...] = jnp.zeros_like(h_ref)

    nb = h_ref.shape[0]
    head_ones = jnp.where(_head_ones(), 1.0, 0.0).astype(BF16)
    hs = [_split(h_ref[b]) for b in range(nb)]
    ys = [_dot3(_split(rh_ref[b]), hs[b]) + y0_ref[b] for b in range(nb)]
    for b in range(nb):
        h_ref[b] = _dot3((p_ref[b, 0, 0], p_ref[b, 0, 1]), hs[b]) + q_ref[b, 0]
    inv_n = 1.0 / HEAD
    for b in range(nb):
        y = ys[b]
        d = y - _dot_ones(y, head_ones) * inv_n
        var = _dot_ones(d * d, head_ones) * inv_n
        yn = d * lax.rsqrt(var + RW_GN_EPS) * gng_ref[...] + gnb_ref[...]
        o_ref[b] = (yn + bonus_ref[b]) * g_ref[b]


def _rw_scan(p, q, rh, y0, g, bonus, gn_g, gn_b):
    B, T, _ = rh.shape
    C = RW_CHUNK
    nc = T // C
    seq = pl.BlockSpec((B, C, RW_WIDTH), lambda c: (0, c, 0))
    vec = pl.BlockSpec((1, RW_WIDTH), lambda c: (0, 0))
    return pl.pallas_call(
        _rw_scan_body,
        grid=(nc,),
        in_specs=[pl.BlockSpec((B, 1, 2, RW_WIDTH, RW_WIDTH), lambda c: (0, c, 0, 0, 0)),
                  pl.BlockSpec((B, 1, RW_WIDTH, RW_WIDTH), lambda c: (0, c, 0, 0)),
                  seq, seq, seq, seq, vec, vec],
        out_specs=seq,
        out_shape=jax.ShapeDtypeStruct((B, T, RW_WIDTH), F32),
        scratch_shapes=[pltpu.VMEM((B, RW_WIDTH, RW_WIDTH), F32)],
        compiler_params=_cparams(("arbitrary",)),
        name="rwkv_chunk_scan",
    )(p, q, rh, y0, g, bonus, gn_g.reshape(1, -1), gn_b.reshape(1, -1))


def _pool_body(z_ref, zp_ref, w_ref, scale_ref, o_ref):
    i = pl.program_id(1)
    z = z_ref[0]
    tt = z.shape[0]
    halo = jnp.where(i == 0, 0.0, zp_ref[0])
    e = jnp.concatenate([halo, z], axis=0)
    sums = []
    s = e
    for shift in (1, 2, 4, 8):
        s = s + pltpu.roll(s, shift, axis=0)
        sums.append(s[POOL_HALO:, :])
    t_idx = i * tt + lax.broadcasted_iota(jnp.int32, (tt, POOL_WIDTH), 0)
    lane_group = lax.broadcasted_iota(jnp.int32, (tt, POOL_WIDTH), 1) // HEAD
    pooled = jnp.zeros_like(z)
    for gi, win in enumerate(POOL_WINDOWS):
        cnt = jnp.minimum(t_idx + 1, win).astype(F32)
        pooled = jnp.where(lane_group == gi, sums[gi] / cnt - z, pooled)
    o_ref[0] = _dot(pooled.astype(BF16), w_ref[...]) * scale_ref[...]


def _pool(z_pool, w_blockdiag_bf16, scale, tt=512):
    B, T, _ = z_pool.shape
    return pl.pallas_call(
        _pool_body,
        grid=(B, T // tt),
        in_specs=[pl.BlockSpec((1, tt, POOL_WIDTH), lambda b, i: (b, i, 0)),
                  pl.BlockSpec((1, POOL_HALO, POOL_WIDTH),
                               lambda b, i: (b, jnp.maximum(i * (tt // POOL_HALO) - 1, 0), 0)),
                  pl.BlockSpec((POOL_WIDTH, POOL_WIDTH), lambda b, i: (0, 0)),
                  pl.BlockSpec((1, POOL_WIDTH), lambda b, i: (0, 0))],
        out_specs=pl.BlockSpec((1, tt, POOL_WIDTH), lambda b, i: (b, i, 0)),
        out_shape=jax.ShapeDtypeStruct((B, T, POOL_WIDTH), F32),
        compiler_params=_cparams(("parallel", "parallel")),
        name="pool_mixer",
    )(z_pool, z_pool, w_blockdiag_bf16, scale.reshape(1, -1))


def _rope(x, cos_w, sin_w):
    w = x.shape[1]
    lane = lax.broadcasted_iota(jnp.int32, x.shape, 1) % HEAD
    partner = jnp.where(lane < HEAD // 2, pltpu.roll(x, w - HEAD // 2, axis=1), pltpu.roll(x, HEAD // 2, axis=1))
    return x * cos_w + partner * sin_w


def _head_lanes(a, h, dtype):
    part = a[:, h * HEAD:(h + 1) * HEAD]
    return jnp.concatenate([part, jnp.zeros_like(part)], axis=1).astype(dtype)


def _nsa_prep_body(zq_ref, zkc_ref, zks_ref, zvs_ref, zkw_ref, zvw_ref, zg_ref, cos_ref, sin_ref,
                   q_ref, kc_ref, ks_ref, vst_ref, kw_ref, vwt_ref, gt_ref):
    cos = cos_ref[0]
    sin = sin_ref[0]
    cos_q = jnp.concatenate([cos] * (NSA_WIDTH // LANE), axis=1)
    sin_q = jnp.concatenate([sin] * (NSA_WIDTH // LANE), axis=1)
    qr = _rope(zq_ref[0], cos_q, sin_q) * (HEAD ** -0.5 * LOG2E)
    for h in range(NSA_Q_HEADS):
        q_ref[0, h] = _head_lanes(qr, h, BF16)
    kc_ref[0] = _rope(zkc_ref[0], cos, sin)
    ks = _rope(zks_ref[0], cos, sin)
    kw = _rope(zkw_ref[0], cos, sin)
    vst = zvs_ref[0].T
    vwt = zvw_ref[0].T
    for h in range(NSA_KV_HEADS):
        sl = slice(h * HEAD, (h + 1) * HEAD)
        ks_ref[0, h] = _head_lanes(ks, h, BF16)
        kw_ref[0, h] = _head_lanes(kw, h, BF16)
        tail = lax.broadcasted_iota(jnp.int32, (V_ROWS - HEAD, vst.shape[1]), 0)
        ones_row = jnp.where(tail == 0, 1.0, 0.0).astype(BF16)
        vst_ref[0, h, 0:HEAD, :] = vst[sl, :].astype(BF16)
        vst_ref[0, h, HEAD:V_ROWS, :] = ones_row
        vwt_ref[0, h, 0:HEAD, :] = vwt[sl, :].astype(BF16)
        vwt_ref[0, h, HEAD:V_ROWS, :] = ones_row
    gt = _sigmoid(zg_ref[0]).T
    for h in range(NSA_KV_HEADS):
        gt_ref[0, h] = gt[16 * h:16 * (h + 1), :]


def _nsa_prep(z_q, z_kc, z_ks, z_vs, z_kw, z_vw, z_g, cos_t, sin_t, tt=512):
    B, T, _ = z_q.shape
    seq = lambda w: pl.BlockSpec((1, tt, w), lambda b, i: (b, i, 0))
    hm = lambda nh: pl.BlockSpec((1, nh, tt, LANE), lambda b, i: (b, 0, i, 0))
    tr = lambda rows: pl.BlockSpec((1, NSA_KV_HEADS, rows, tt), lambda b, i: (b, 0, 0, i))
    sds = jax.ShapeDtypeStruct
    return pl.pallas_call(
        _nsa_prep_body,
        grid=(B, T // tt),
        in_specs=[seq(NSA_WIDTH)] + [seq(NSA_KV)] * 6 + [seq(LANE), seq(LANE)],
        out_specs=(hm(NSA_Q_HEADS), seq(NSA_KV), hm(NSA_KV_HEADS), tr(V_ROWS), hm(NSA_KV_HEADS), tr(V_ROWS), tr(16)),
        out_shape=(sds((B, NSA_Q_HEADS, T, LANE), BF16), sds((B, T, NSA_KV), F32),
                   sds((B, NSA_KV_HEADS, T, LANE), BF16), sds((B, NSA_KV_HEADS, V_ROWS, T), BF16),
                   sds((B, NSA_KV_HEADS, T, LANE), BF16), sds((B, NSA_KV_HEADS, V_ROWS, T), BF16),
                   sds((B, NSA_KV_HEADS, 16, T), F32)),
        compiler_params=_cparams(("parallel", "parallel")),
        name="nsa_prep",
    )(z_q, z_kc, z_ks, z_vs, z_kw, z_vw, z_g, cos_t, sin_t)


def _nsa_cmp_body(gk_ref, gv_ref, pek_ref, pev_ref, wk_ref, wv_ref, kc_ref, vct_ref):
    half = NSA_KV

    def compress(g, pe_ref, w_ref):
        n = g.shape[0]
        lo = _dot((g + pe_ref[0:1, :]).astype(BF16), w_ref[:, 0:half])
        hi = _dot((g + pe_ref[1:2, :]).astype(BF16), w_ref[:, half:2 * half])
        hi_next = pltpu.roll(hi, n - 1, axis=0)
        row = lax.broadcasted_iota(jnp.int32, lo.shape, 0)
        return lo + jnp.where(row == n - 1, 0.0, hi_next)

    kc = compress(gk_ref[0], pek_ref, wk_ref)
    vct = compress(gv_ref[0], pev_ref, wv_ref).T
    for h in range(NSA_KV_HEADS):
        sl = slice(h * HEAD, (h + 1) * HEAD)
        kc_ref[0, h] = _head_lanes(kc, h, BF16)
        vct_ref[0, h] = vct[sl, :].astype(BF16)


def _nsa_compress(kc_groups, vc_groups, pe_k2, pe_v2, wk2, wv2):
    B, ncp, gw = kc_groups.shape
    grp = pl.BlockSpec((1, ncp, gw), lambda b: (b, 0, 0))
    pe = pl.BlockSpec((2, gw), lambda b: (0, 0))
    wspec = pl.BlockSpec((gw, 2 * NSA_KV), lambda b: (0, 0))
    sds = jax.ShapeDtypeStruct
    return pl.pallas_call(
        _nsa_cmp_body,
        grid=(B,),
        in_specs=[grp, grp, pe, pe, wspec, wspec],
        out_specs=(pl.BlockSpec((1, NSA_KV_HEADS, ncp, LANE), lambda b: (b, 0, 0, 0)),
                   pl.BlockSpec((1, NSA_KV_HEADS, HEAD, ncp), lambda b: (b, 0, 0, 0))),
        out_shape=(sds((B, NSA_KV_HEADS, ncp, LANE), BF16), sds((B, NSA_KV_HEADS, HEAD, ncp), BF16)),
        compiler_params=_cparams(("parallel",)),
        name="nsa_compress",
    )(kc_groups, vc_groups, pe_k2, pe_v2, wk2, wv2)


def _nsa_attn_body(q_ref, kc_ref, vct_ref, ks_ref, vst_ref, kw_ref, vwt_ref, gt_ref, mcs_ref, blk_ref,
                   o_ref, val_ref, sa_ref, sb_ref, sc_ref, sd_ref):
    jb = pl.program_id(2)
    s0 = jb * Q_BLOCK
    cols = NSA_GQA * Q_BLOCK
    q = q_ref[0].reshape(cols, LANE)
    ncp = kc_ref.shape[2]

    sc = _dot_nt(kc_ref[0, 0], q)
    n_idx = lax.broadcasted_iota(jnp.int32, (ncp, cols), 0)
    t_c = s0 + lax.broadcasted_iota(jnp.int32, (ncp, cols), 1) % Q_BLOCK
    sc = jnp.where(CMP_STRIDE * n_idx + (CMP_BLOCK - 1) <= t_c, sc, NEG)
    e = jnp.exp2(sc - jnp.max(sc, axis=0, keepdims=True))
    any_valid = jnp.where(t_c[0:1, :] >= CMP_BLOCK - 1, 1.0, 0.0)
    pc = e * (any_valid / jnp.sum(e, axis=0, keepdims=True))
    o_cmp = _dot(vct_ref[0, 0], pc.astype(BF16))
    psum = pc[:, 0:Q_BLOCK]
    for g in range(1, NSA_GQA):
        psum = psum + pc[:, g * Q_BLOCK:(g + 1) * Q_BLOCK]
    p_hi = psum.astype(BF16)
    rest = psum - p_hi.astype(F32)
    p_mid = rest.astype(BF16)
    p_lo = (rest - p_mid.astype(F32)).astype(BF16)
    mcs = mcs_ref[...]
    imp = _dot(mcs, p_hi) + (_dot(mcs, p_mid) + _dot(mcs, p_lo))

    j_idx = lax.broadcasted_iota(jnp.int32, (NSEL_PAD, Q_BLOCK), 0)
    jc = (s0 + lax.broadcasted_iota(jnp.int32, (NSEL_PAD, Q_BLOCK), 1)) // SEL_BLOCK
    causal = j_idx <= jc
    forced_imp = jnp.where(j_idx == 0, FORCE, jnp.where(j_idx == jc, FORCE, jnp.where(j_idx == jc - 1, FORCE, imp)))
    val = jnp.where(causal, forced_imp, NEG)
    val_ref[...] = val
    n_causal_blocks = (s0 + Q_BLOCK) // SEL_BLOCK

    def rank_step(jp, cnt):
        other = jnp.broadcast_to(val_ref[pl.ds(jp, 1), :], (NSEL_PAD, Q_BLOCK))
        ahead = jnp.where(other > val, 1.0, jnp.where(other == val, jnp.where(j_idx > jp, 1.0, 0.0), 0.0))
        return cnt + ahead

    rank = lax.fori_loop(0, n_causal_blocks, rank_step, jnp.zeros((NSEL_PAD, Q_BLOCK), F32))
    bias = jnp.where(causal, jnp.where(rank < SEL_TOPN, 0.0, NEG), NEG)

    bias_q = bias.T.astype(BF16)
    q_aug = jnp.concatenate([q, jnp.concatenate([bias_q] * NSA_GQA, axis=0)], axis=1)

    def sel_scores(off):
        k_aug = jnp.concatenate([ks_ref[0, 0, pl.ds(off, KEY_GROUP), :], blk_ref[pl.ds(off, KEY_GROUP), :]], axis=1)
        return _dot_nt(k_aug, q_aug)

    def flash_update(v_t, st, carry):
        m, acc = carry
        m_new = jnp.maximum(m, jnp.max(st, axis=0, keepdims=True))
        p = jnp.exp2((st - m_new).astype(BF16))
        return m_new, acc * jnp.exp2(m - m_new) + _dot(v_t, p)

    init =(jnp.full((1, cols), NEG, F32), jnp.zeros((V_ROWS, cols), F32))
    n_groups = s0 // KEY_GROUP + 1
    n_pairs = n_groups // 2
    t_q = s0 + lax.broadcasted_iota(jnp.int32, (Q_BLOCK, cols), 1) % Q_BLOCK
    r_q = lax.broadcasted_iota(jnp.int32, (Q_BLOCK, cols), 0)

    def put_scores(ref, g):
        off = pl.multiple_of(g * KEY_GROUP, KEY_GROUP)
        ref[...] = sel_scores(off)
        r = pl.multiple_of(jnp.clip(s0 - off, 0, KEY_GROUP - Q_BLOCK), Q_BLOCK)
        tile = ref[pl.ds(r, Q_BLOCK), :]
        ref[pl.ds(r, Q_BLOCK), :] = jnp.where(off + r + r_q <= t_q, tile, NEG)

    def consume(ref, g, carry):
        off = pl.multiple_of(g * KEY_GROUP, KEY_GROUP)
        return flash_update(vst_ref[0, 0, :, pl.ds(off, KEY_GROUP)], ref[...], carry)

    last = n_groups - 1
    buf0, buf1 = (sa_ref, sb_ref), (sc_ref, sd_ref)
    put_scores(sa_ref, 0)
    put_scores(sb_ref, jnp.minimum(1, last))

    def stage(cur, nxt, pair, carries):
        put_scores(nxt[0], jnp.minimum(2 * pair + 2, last))
        put_scores(nxt[1], jnp.minimum(2 * pair + 3, last))
        return consume(cur[0], 2 * pair, carries[0]), consume(cur[1], 2 * pair + 1, carries[1])

    def two_pairs(i, carries):
        return stage(buf1, buf0, 2 * i + 1, stage(buf0, buf1, 2 * i, carries))

    carries = lax.fori_loop(0, n_pairs // 2, two_pairs, (init, init))
    odd_pairs = n_pairs % 2
    c_even, c_odd = lax.fori_loop(n_pairs - odd_pairs, n_pairs, lambda pr, c: stage(buf0, buf1, pr, c), carries)
    left = n_groups - 2 * n_pairs
    c_even = lax.fori_loop(0, left * (1 - odd_pairs), lambda _, c: consume(sa_ref, last, c), c_even)
    c_even = lax.fori_loop(0, left * odd_pairs, lambda _, c: consume(sc_ref, last, c), c_even)
    m_e, acc_e = c_even
    m_o, acc_o = c_odd
    m_s = jnp.maximum(m_e, m_o)
    acc_s = acc_e * jnp.exp2(m_e - m_s) + acc_o * jnp.exp2(m_o - m_s)

    wkeys = WINDOW + Q_BLOCK
    off_w = pl.multiple_of(jnp.maximum(s0 - WINDOW, 0), Q_BLOCK)
    sw = _dot_nt(kw_ref[0, 0, pl.ds(off_w, wkeys), :], q)
    kpos = off_w + lax.broadcasted_iota(jnp.int32, (wkeys, cols), 0)
    t_w = s0 + lax.broadcasted_iota(jnp.int32, (wkeys, cols), 1) % Q_BLOCK
    sw = jnp.where(kpos <= t_w, jnp.where(kpos > t_w - WINDOW, sw, NEG), NEG)
    pw = jnp.exp2((sw - jnp.max(sw, axis=0, keepdims=True)).astype(BF16))
    acc_w = _dot(vwt_ref[0, 0, :, pl.ds(off_w, wkeys)], pw)

    def gate_row(branch):
        return jnp.concatenate([gt_ref[0, 0, 3 * g + branch:3 * g + branch + 1, :] for g in range(NSA_GQA)], axis=1)

    o_t = (gate_row(0) * o_cmp + (gate_row(1) / acc_s[HEAD:HEAD + 1]) * acc_s[0:HEAD]
           + (gate_row(2) / acc_w[HEAD:HEAD + 1]) * acc_w[0:HEAD])
    stacked = jnp.concatenate([o_t[:, g * Q_BLOCK:(g + 1) * Q_BLOCK] for g in range(NSA_GQA)], axis=0)
    o_ref[0] = stacked.T


def _nsa_attention(q_r, kcmp, vcmp_t, ks_r, vs_t, kw_r, vw_t, g_t, mcs_t, blk_onehot):
    B, _, T, _ = q_r.shape
    ncp = kcmp.shape[2]
    nqb = T // Q_BLOCK
    assert T % KEY_GROUP == 0 and T >= WINDOW + Q_BLOCK and T // SEL_BLOCK <= NSEL_PAD
    kv_rows = lambda n: pl.BlockSpec((1, 1, n, LANE), lambda b, h, j: (b, h, 0, 0))
    kv_cols = lambda n, rows: pl.BlockSpec((1, 1, rows, n), lambda b, h, j: (b, h, 0, 0))
    return pl.pallas_call(
        _nsa_attn_body,
        grid=(B, NSA_KV_HEADS, nqb),
        in_specs=[pl.BlockSpec((1, NSA_GQA, Q_BLOCK, LANE), lambda b, h, j: (b, h, j, 0)),
                  kv_rows(ncp), kv_cols(ncp, HEAD), kv_rows(T), kv_cols(T, V_ROWS), kv_rows(T), kv_cols(T, V_ROWS),
                  pl.BlockSpec((1, 1, 16, Q_BLOCK), lambda b, h, j: (b, h, 0, j)),
                  pl.BlockSpec((NSEL_PAD, ncp), lambda b, h, j: (0, 0)),
                  pl.BlockSpec((T, NSEL_PAD), lambda b, h, j: (0, 0))],
        out_specs=pl.BlockSpec((1, Q_BLOCK, NSA_GQA * HEAD), lambda b, h, j: (b, j, h)),
        out_shape=jax.ShapeDtypeStruct((B, T, NSA_WIDTH), F32),
        scratch_shapes=[pltpu.VMEM((NSEL_PAD, Q_BLOCK), F32)]
        + [pltpu.VMEM((KEY_GROUP, NSA_GQA * Q_BLOCK), F32)] * 4,
        compiler_params=_cparams(("parallel", "parallel", "arbitrary")),
        name="nsa_attention",
    )(q_r, kcmp, vcmp_t, ks_r, vs_t, kw_r, vw_t, g_t, mcs_t, blk_onehot)


def _out_proj_body(x_ref, ya_ref, yb_ref, yc_ref, wa_ref, wb_ref, wc_ref, o_ref):
    acc = _dot(ya_ref[...].astype(BF16), wa_ref[...])
    acc = acc + _dot(yb_ref[...].astype(BF16), wb_ref[...])
    acc = acc + _dot(yc_ref[...].astype(BF16), wc_ref[...])
    o_ref[...] = x_ref[...] + acc


def _out_proj(x2, ya, yb, yc, w_out_bf16, tm=512):
    m, d = x2.shape
    wa, wb, wc = w_out_bf16[:RW_WIDTH], w_out_bf16[RW_WIDTH:RW_WIDTH + POOL_WIDTH], w_out_bf16[RW_WIDTH + POOL_WIDTH:]
    tile = lambda w: pl.BlockSpec((tm, w), lambda i: (i, 0))
    wfull = lambda w: pl.BlockSpec((w, d), lambda i: (0, 0))
    return pl.pallas_call(
        _out_proj_body,
        grid=(m // tm,),
        in_specs=[tile(d), tile(RW_WIDTH), tile(POOL_WIDTH), tile(NSA_WIDTH),
                  wfull(RW_WIDTH), wfull(POOL_WIDTH), wfull(NSA_WIDTH)],
        out_specs=tile(d),
        out_shape=jax.ShapeDtypeStruct((m, d), F32),
        compiler_params=_cparams(("parallel",)),
        name="out_proj_residual",
    )(x2, ya, yb, yc, wa, wb, wc)


def _gelu_tanh(x):
    return 0.5 * x * (1.0 + jnp.tanh(math.sqrt(2.0 / math.pi) * (x + 0.044715 * (x * x * x))))


def _ffn_body(x_ref, g_ref, wup_ref, cw_ref, cb_ref, wdn_ref, o_ref, carry_ref, *, tiles_per_seq, f_chunk):
    i = pl.program_id(0)
    x = x_ref[...]
    tm = x.shape[0]
    d_ff = wdn_ref.shape[0]
    h = _rms(x, g_ref[...]).astype(BF16)
    first = (i % tiles_per_seq) == 0
    row = lax.broadcasted_iota(jnp.int32, (tm, f_chunk), 0)

    def conv(col0):
        u = _dot(h, wup_ref[:, col0:col0 + f_chunk])
        p2 = jnp.where(first, 0.0, carry_ref[6:7, col0:col0 + f_chunk])
        p1 = jnp.where(first, 0.0, carry_ref[7:8, col0:col0 + f_chunk])
        u1 = jnp.where(row == 0, p1, pltpu.roll(u, 1, axis=0))
        u2 = jnp.where(row == 0, p2, jnp.where(row == 1, p1, pltpu.roll(u, 2, axis=0)))
        carry_ref[:, col0:col0 + f_chunk] = u[tm - 8:tm, :]
        cw = cw_ref[:, col0:col0 + f_chunk]
        return cw[0:1] * u2 + cw[1:2] * u1 + cw[2:3] * u + cb_ref[:, col0:col0 + f_chunk]

    acc = x
    for c0 in range(0, d_ff, f_chunk):
        act = _gelu_tanh(conv(c0)) * conv(d_ff + c0)
        acc = acc + _dot(act.astype(BF16), wdn_ref[c0:c0 + f_chunk, :])
    o_ref[...] = acc


def _conv_ffn(x2, g, w_up_bf16, conv_w, conv_b, w_down_bf16, seq_len, tm=512, f_chunk=1408):
    m, d = x2.shape
    f2 = w_up_bf16.shape[1]
    d_ff = f2 // 2
    assert d_ff % f_chunk == 0 and f_chunk % LANE == 0 and seq_len % tm == 0
    full = lambda shp: pl.BlockSpec(shp, lambda i: (0,) * len(shp))
    return pl.pallas_call(
        functools.partial(_ffn_body, tiles_per_seq=seq_len // tm, f_chunk=f_chunk),
        grid=(m // tm,),
        in_specs=[pl.BlockSpec((tm, d), lambda i: (i, 0)), full((1, d)), full((d, f2)),
                  full((CONV_W, f2)), full((1, f2)), full((d_ff, d))],
        out_specs=pl.BlockSpec((tm, d), lambda i: (i, 0)),
        out_shape=jax.ShapeDtypeStruct((m, d), F32),
        scratch_shapes=[pltpu.VMEM((8, f2), F32)],
        compiler_params=_cparams(("arbitrary",)),
        name="conv_ffn",
    )(x2, g.reshape(1, d), w_up_bf16, conv_w.reshape(CONV_W, f2), conv_b.reshape(1, f2), w_down_bf16)


def _ple_body(x_ref, p_ref, g_ref, wg_ref, wp_ref, gf_ref, o_ref, *, final):
    x = x_ref[...]
    gate = _sigmoid(_dot(_rms(x, g_ref[...]).astype(BF16), wg_ref[...]))
    y = x + _dot(p_ref[...].astype(BF16), wp_ref[...]) * gate
    o_ref[...] = _rms(y, gf_ref[...]) if final else y


def _ple(x2, p2, g, w_gate_bf16, w_proj_bf16, g_final, final, tm=512):
    m, d = x2.shape
    pd = p2.shape[1]
    full = lambda shp: pl.BlockSpec(shp, lambda i: (0,) * len(shp))
    return pl.pallas_call(
        functools.partial(_ple_body, final=final),
        grid=(m // tm,),
        in_specs=[pl.BlockSpec((tm, d), lambda i: (i, 0)), pl.BlockSpec((tm, pd), lambda i: (i, 0)),
                  full((1, d)), full((d, d)), full((pd, d)), full((1, d))],
        out_specs=pl.BlockSpec((tm, d), lambda i: (i, 0)),
        out_shape=jax.ShapeDtypeStruct((m, d), F32),
        compiler_params=_cparams(("parallel",)),
        name="ple_final_norm" if final else "ple",
    )(x2, p2, g.reshape(1, d), w_gate_bf16, w_proj_bf16, g_final.reshape(1, d))


def _overlap_matrix(seq_len):
    ncp = seq_len // CMP_STRIDE
    n_cmp = (seq_len - CMP_BLOCK) // CMP_STRIDE + 1
    cs = CMP_STRIDE * np.arange(n_cmp)
    ss = SEL_BLOCK * np.arange(seq_len // SEL_BLOCK)
    ov = (np.minimum(cs[:, None] + CMP_BLOCK - 1, ss[None] + SEL_BLOCK - 1) - np.maximum(cs[:, None], ss[None]) + 1)
    m = np.zeros((NSEL_PAD, ncp), np.float32)
    m[:ss.size, :n_cmp] = (np.clip(ov, 0, CMP_BLOCK).astype(np.float32) / CMP_BLOCK).T
    return jnp.asarray(m, dtype=BF16)


def _compress_weight(w):
    halves = w.reshape(2, CMP_STRIDE, HEAD, HEAD)
    eye = jnp.eye(NSA_KV_HEADS, dtype=w.dtype)
    w2 = jnp.einsum('sldf,hg->lhdsgf', halves, eye)
    return w2.reshape(CMP_STRIDE * NSA_KV, 2 * NSA_KV).astype(BF16)


def _compress_pe(pe):
    halves = pe.reshape(2, CMP_STRIDE, 1, HEAD)
    return jnp.broadcast_to(halves, (2, CMP_STRIDE, NSA_KV_HEADS, HEAD)).reshape(2, CMP_STRIDE * NSA_KV)


def _in_proj_layout(w_in):
    d = w_in.shape[0]
    gate0 = RW_IN + POOL_WIDTH + NSA_WIDTH + 6 * NSA_KV
    per_head = NSA_GQA * 3
    gates = jnp.zeros((d, LANE), w_in.dtype)
    for h in range(NSA_KV_HEADS):
        gates = gates.at[:, 16 * h:16 * h + per_head].set(w_in[:, gate0 + per_head * h:gate0 + per_head * (h + 1)])
    return jnp.concatenate([w_in[:, :gate0], gates], axis=1).astype(BF16)


_IN_SPLITS = ((0, RW_IN), (RW_IN, RW_IN + POOL_WIDTH), (RW_IN + POOL_WIDTH, RW_IN + POOL_WIDTH + NSA_WIDTH)) + tuple(
    (RW_IN + POOL_WIDTH + NSA_WIDTH + i * NSA_KV, RW_IN + POOL_WIDTH + NSA_WIDTH + (i + 1) * NSA_KV) for i in range(7))


def kernel(x, p, positions, g_mix, w_in, rw_mu, rw_w0, rw_w_up, rw_a0, rw_a_up, rw_g_up, rw_k_k, rw_k_a, rw_r_k, rw_gn_g, rw_gn_b, pool_w, pool_scale, nsa_pe_k, nsa_pe_v, nsa_w_ck, nsa_w_cv, w_out, g_ffn, ffn_w_up, ffn_conv_w, ffn_conv_b, ffn_w_down, g_ple, ple_w_gate, ple_w_proj, g_final):
    B, T, D = x.shape
    depth = w_in.shape[0]
    M = B * T

    half = HEAD // 2
    inv = ROPE_THETA ** (-jnp.arange(half, dtype=F32) / half)
    ang = positions.astype(F32)[..., None] * inv
    cos, sin = jnp.cos(ang), jnp.sin(ang)
    cos_t = jnp.concatenate([cos, cos] * (LANE // HEAD), axis=-1)
    sin_t = jnp.concatenate([-sin, sin] * (LANE // HEAD), axis=-1)
    mcs_t = _overlap_matrix(T)
    blk_onehot = (jnp.arange(T)[:, None] // SEL_BLOCK == jnp.arange(NSEL_PAD)[None, :]).astype(BF16)

    x2 = x.reshape(M, D)
    for i in range(depth):
        zs = _norm_matmul(x2, g_mix[i], _in_proj_layout(w_in[i]), _IN_SPLITS)
        z_rw, z_pool, z_q, z_kc, z_vc, z_ks, z_vs, z_kw, z_vw, z_g = (
            z.reshape(B, T, z.shape[1]) for z in zs)

        pm, qm, rh, y0, gate, bonus = _rw_chunks(z_rw, rw_mu[i], rw_w0[i], rw_w_up[i], rw_a0[i], rw_a_up[i],
                                                 rw_g_up[i], rw_k_k[i], rw_k_a[i], rw_r_k[i].reshape(-1))
        y_a = _rw_scan(pm, qm, rh, y0, gate, bonus, rw_gn_g[i], rw_gn_b[i])

        w_pool_bd = jax.scipy.linalg.block_diag(*[pool_w[i, gi] for gi in range(pool_w.shape[1])]).astype(BF16)
        y_b = _pool(z_pool, w_pool_bd, pool_scale[i])

        q_r, kc_r, ks_r, vs_t, kw_r, vw_t, g_t = _nsa_prep(z_q, z_kc, z_ks, z_vs, z_kw, z_vw, z_g, cos_t, sin_t)
        grp = (B, T // CMP_STRIDE, CMP_STRIDE * NSA_KV)
        kcmp, vcmp_t = _nsa_compress(kc_r.reshape(grp), z_vc.reshape(grp),
                                     _compress_pe(nsa_pe_k[i]), _compress_pe(nsa_pe_v[i]),
                                     _compress_weight(nsa_w_ck[i]), _compress_weight(nsa_w_cv[i]))
        y_c = _nsa_attention(q_r, kcmp, vcmp_t, ks_r, vs_t, kw_r, vw_t, g_t, mcs_t, blk_onehot)

        x2 = _out_proj(x2, y_a.reshape(M, -1), y_b.reshape(M, -1), y_c.reshape(M, -1), w_out[i].astype(BF16))
        x2 = _conv_ffn(x2, g_ffn[i], ffn_w_up[i].astype(BF16), ffn_conv_w[i], ffn_conv_b[i],
                       ffn_w_down[i].astype(BF16), T)
        x2 = _ple(x2, p[i].reshape(M, -1), g_ple[i], ple_w_gate[i].astype(BF16), ple_w_proj[i].astype(BF16),
                  g_final, final=(i == depth - 1))
    return x2.reshape(B, T, D)
```

```python
import functools
import math

import numpy as np
import jax
import jax.numpy as jnp
from jax import lax
from jax.experimental import pallas as pl
from jax.experimental.pallas import tpu as pltpu

F32 = jnp.float32
BF16 = jnp.bfloat16

RMS_EPS = 1e-6
ROPE_THETA = 10000.0
HEAD = 64
RW_HEADS = 4
RW_WIDTH = RW_HEADS * HEAD
RW_DECAY_LORA = 64
RW_AAA_LORA = 64
RW_GATE_LORA = 128
RW_GN_EPS = 64e-5
RW_IN = 3 * RW_WIDTH + RW_DECAY_LORA + RW_AAA_LORA + RW_GATE_LORA
POOL_WIDTH = 256
POOL_WINDOWS = (2, 4, 8, 16)
POOL_HALO = 16
NSA_Q_HEADS = 8
NSA_KV_HEADS = 2
NSA_GQA = NSA_Q_HEADS // NSA_KV_HEADS
NSA_WIDTH = NSA_Q_HEADS * HEAD
NSA_KV = NSA_KV_HEADS * HEAD
CMP_BLOCK = 32
CMP_STRIDE = 16
SEL_BLOCK = 64
SEL_TOPN = 16
WINDOW = 512
Q_BLOCK = 128
KEY_GROUP = 512
V_ROWS = HEAD + 16
NSEL_PAD = 128
NEG = -1e30
LOG2E = math.log2(math.e)
FORCE = 1e9
CONV_W = 3
LANE = 128
VMEM_LIMIT = 56 * 1024 * 1024

RW_CHUNK = 64

def _cparams(sem):
    return pltpu.CompilerParams(dimension_semantics=sem, vmem_limit_bytes=VMEM_LIMIT)


def _dot(a, b, prec=None):
    return lax.dot_general(a, b, (((1,), (0,)), ((), ())), precision=prec, preferred_element_type=F32)


def _dot_nt(a, b, prec=None):
    return lax.dot_general(a, b, (((1,), (1,)), ((), ())), precision=prec, preferred_element_type=F32)


def _sigmoid(x):
    return 1.0 / (1.0 + jnp.exp(-x))


def _rms(x, g):
    ms = jnp.mean(x * x, axis=-1, keepdims=True)
    return (x * lax.rsqrt(ms + RMS_EPS)) * g


def _norm_mm_body(x_ref, g_ref, w_ref, *o_refs, splits):
    h = _rms(x_ref[...], g_ref[...]).astype(BF16)
    for o_ref, (lo, hi) in zip(o_refs, splits):
        o_ref[...] = _dot(h, w_ref[:, lo:hi])


def _norm_matmul(x2, g, w_bf16, splits, tm=512):
    m, d = x2.shape
    n = w_bf16.shape[1]
    outs = tuple(jax.ShapeDtypeStruct((m, hi - lo), F32) for lo, hi in splits)
    return pl.pallas_call(
        functools.partial(_norm_mm_body, splits=splits),
        grid=(m // tm,),
        in_specs=[pl.BlockSpec((tm, d), lambda i: (i, 0)),
                  pl.BlockSpec((1, d), lambda i: (0, 0)),
                  pl.BlockSpec((d, n), lambda i: (0, 0))],
        out_specs=tuple(pl.BlockSpec((tm, hi - lo), lambda i: (i, 0)) for lo, hi in splits),
        out_shape=outs,
        compiler_params=_cparams(("parallel",)),
        name="norm_in_proj",
    )(x2, g.reshape(1, d), w_bf16)


def _split(x):
    hi = x.astype(BF16)
    return hi, (x - hi.astype(F32)).astype(BF16)


def _dot3(a, b, dot=_dot):
    return dot(a[0], b[0]) + (dot(a[0], b[1]) + dot(a[1], b[0]))


def _dot_ones(x, ones_bf16):
    hi, lo = _split(x)
    return _dot(hi, ones_bf16) + _dot(lo, ones_bf16)


def _head_ones():
    hr = lax.broadcasted_iota(jnp.int32, (RW_WIDTH, RW_WIDTH), 0) // HEAD
    hc = lax.broadcasted_iota(jnp.int32, (RW_WIDTH, RW_WIDTH), 1) // HEAD
    return hr == hc


def _rw_chunk_body(z_ref, zp_ref, mu_ref, w0_ref, wup_ref, a0_ref, aup_ref, gup_ref, kk_ref, ka_ref,
                   rk_ref, p_ref, q_ref, rh_ref, y0_ref, g_ref, bonus_ref):
    i = pl.program_id(1)
    C = RW_CHUNK
    W = RW_WIDTH
    z = z_ref[0]
    rows = z.shape[0]
    n_chunks = rows // C
    prev = jnp.where(i == 0, 0.0, zp_ref[0, 7:8, :])
    row_z = lax.broadcasted_iota(jnp.int32, z.shape, 0)
    zs = jnp.where(row_z == 0, prev, pltpu.roll(z, 1, axis=0))
    zf = z + (zs - z) * mu_ref[...]
    r = zf[:, 0:W]
    k = zf[:, W:2 * W]
    v = zf[:, 2 * W:3 * W]
    zw = zf[:, 3 * W:3 * W + RW_DECAY_LORA]
    za = zf[:, 3 * W + RW_DECAY_LORA:3 * W + RW_DECAY_LORA + RW_AAA_LORA]
    zg = zf[:, 3 * W + RW_DECAY_LORA + RW_AAA_LORA:]

    lora = lambda x, w_ref: _dot3(_split(x), (w_ref[0], w_ref[1]))
    xw = -(w0_ref[...] + lora(jnp.tanh(zw), wup_ref))
    softplus = jnp.maximum(xw, 0.0) + jnp.log(1.0 + jnp.exp(-jnp.abs(xw)))
    ld = -jnp.exp(-softplus - 0.5)
    a = _sigmoid(a0_ref[...] + lora(za, aup_ref))
    g_ref[0] = lora(_sigmoid(zg), gup_ref)
    kk = k * kk_ref[...]
    k2 = k * (1.0 + (a - 1.0) * ka_ref[...])

    same_head = _head_ones()
    head_ones = jnp.where(same_head, 1.0, 0.0).astype(BF16)
    kkn = kk / jnp.maximum(jnp.sqrt(_dot_ones(kk * kk, head_ones)), 1e-12)
    bonus_ref[0] = _dot_ones(r * k2 * rk_ref[...], head_ones) * v

    t_in = lax.broadcasted_iota(jnp.int32, (rows, W), 0) % C
    lc = ld
    shift = 1
    while shift < C:
        lc = lc + jnp.where(t_in >= shift, pltpu.roll(lc, shift, axis=0), 0.0)
        shift *= 2

    tr = lax.broadcasted_iota(jnp.int32, (W, W), 0) % C
    tc = lax.broadcasted_iota(jnp.int32, (W, W), 1) % C
    lower_incl = tr >= tc
    lower_strict = tr > tc
    eye = jnp.where(same_head, jnp.where(tr == tc, 1.0, 0.0), 0.0)

    def embed(x):
        return jnp.where(same_head, jnp.concatenate([x] * RW_HEADS, axis=0), 0.0)

    bf = lambda x: x.astype(BF16)
    A, Bt, Kt, Rt, V, BhT, KhT, gdiag, Rt_raw = [], [], [], [], [], [], [], [], []
    for c in range(n_chunks):
        rs = slice(c * C, (c + 1) * C)
        lc_c, ld_c = lc[rs], ld[rs]
        lc_end = lc_c[C - 1:C, :]
        ginv = jnp.exp(-lc_c)
        dec_end = jnp.exp(lc_end - lc_c)
        kkn_c, a_c, k2_c = kkn[rs], a[rs], k2[rs]
        r_t = r[rs] * jnp.exp(lc_c)
        A.append(bf(embed(-kkn_c * jnp.exp(lc_c - ld_c))))
        Bt.append(bf(embed(kkn_c * a_c * ginv)))
        Kt.append(bf(embed(k2_c * ginv)))
        Rt.append(bf(embed(r_t)))
        Rt_raw.append(r_t)
        V.append(bf(embed(v[rs])))
        BhT.append(bf(embed(kkn_c * a_c * dec_end).T))
        KhT.append(bf(embed(k2_c * dec_end).T))
        gdiag.append(jnp.where(eye > 0.5, jnp.broadcast_to(jnp.exp(lc_end), (W, W)), 0.0))

    each = lambda f, *lists: [f(*xs) for xs in zip(*lists)]
    L = each(lambda x, y: jnp.where(lower_strict, _dot_nt(x, y), 0.0), A, Bt)
    Lak = each(lambda x, y: bf(jnp.where(lower_strict, _dot_nt(x, y), 0.0)), A, Kt)
    Mrb = each(lambda x, y: bf(jnp.where(lower_incl, _dot_nt(x, y), 0.0)), Rt, Bt)
    Mrk = each(lambda x, y: bf(jnp.where(lower_incl, _dot_nt(x, y), 0.0)), Rt, Kt)
    Tm = [eye + x for x in L]
    Pw = each(bf, L)
    for _ in range(int(math.log2(C)) - 1):
        Pw = each(lambda x: bf(_dot(x, x)), Pw)
        Tm = each(lambda t, x: t + _dot(bf(t), x), Tm, Pw)
    Tm = each(bf, Tm)
    Wm = each(lambda t, x: bf(_dot(t, x)), Tm, A)
    LakV = each(lambda x, y: bf(_dot(x, y)), Lak, V)
    U0 = each(lambda t, x: bf(_dot(t, x)), Tm, LakV)
    Pm = each(lambda d, x, y: d + _dot(x, y), gdiag, BhT, Wm)
    Qm = each(lambda x, y, s, t: _dot(x, y) + _dot(s, t), BhT, U0, KhT, V)
    Rh = each(_dot, Mrb, Wm)
    Y0 = each(lambda x, y, s, t: _dot(x, y) + _dot(s, t), Mrb, U0, Mrk, V)

    def flatten(x_bd):
        out = x_bd[0:C]
        for h in range(1, RW_HEADS):
            out = out + x_bd[h * C:(h + 1) * C]
        return out

    for c in range(n_chunks):
        rs = slice(c * C, (c + 1) * C)
        p_hi, p_lo = _split(Pm[c])
        p_ref[0, c, 0] = p_hi
        p_ref[0, c, 1] = p_lo
        q_ref[0, c] = Qm[c]
        rh_ref[0, rs, :] = Rt_raw[c] + flatten(Rh[c])
        y0_ref[0, rs, :] = flatten(Y0[c])


def _rw_chunks(z_rw, mu, w0, w_up, a0, a_up, g_up, k_k, k_a, r_k, chunks_per_step=2):
    B, T, _ = z_rw.shape
    C = RW_CHUNK
    assert C == HEAD, "the chunk build shares one index grid between time and head-dim masks"
    rows = C * chunks_per_step
    nc = T // C
    row = lambda a: a.reshape(1, -1)
    pair = lambda w: jnp.stack(_split(w))
    full = lambda shp: pl.BlockSpec(shp, lambda b, i: (0,) * len(shp))
    seq = lambda w: pl.BlockSpec((1, rows, w), lambda b, i: (b, i, 0))
    sds = jax.ShapeDtypeStruct
    seq_shape = sds((B, T, RW_WIDTH), F32)
    return pl.pallas_call(
        _rw_chunk_body,
        grid=(B, T // rows),
        in_specs=[seq(RW_IN),
                  pl.BlockSpec((1, 8, RW_IN), lambda b, i: (b, jnp.maximum(i * (rows // 8) - 1, 0), 0)),
                  full((1, RW_IN)), full((1, RW_WIDTH)), full((2, RW_DECAY_LORA, RW_WIDTH)),
                  full((1, RW_WIDTH)), full((2, RW_AAA_LORA, RW_WIDTH)), full((2, RW_GATE_LORA, RW_WIDTH)),
                  full((1, RW_WIDTH)), full((1, RW_WIDTH)), full((1, RW_WIDTH))],
        out_specs=(pl.BlockSpec((1, chunks_per_step, 2, RW_WIDTH, RW_WIDTH), lambda b, i: (b, i, 0, 0, 0)),
                   pl.BlockSpec((1, chunks_per_step, RW_WIDTH, RW_WIDTH), lambda b, i: (b, i, 0, 0)),
                   seq(RW_WIDTH), seq(RW_WIDTH), seq(RW_WIDTH), seq(RW_WIDTH)),
        out_shape=(sds((B, nc, 2, RW_WIDTH, RW_WIDTH), BF16), sds((B, nc, RW_WIDTH, RW_WIDTH), F32),
                   seq_shape, seq_shape, seq_shape, seq_shape),
        compiler_params=_cparams(("parallel", "parallel")),
        name="rwkv_chunk_build",
    )(z_rw, z_rw, row(mu), row(w0), pair(w_up), row(a0), pair(a_up), pair(g_up), row(k_k), row(k_a), row(r_k))


def _rw_scan_body(p_ref, q_ref, rh_ref, y0_ref, g_ref, bonus_ref, gng_ref, gnb_ref, o_ref, h_ref):
    @pl.when(pl.program_id(0) == 0)
    def _():
        h_ref[...] = jnp.zeros_like(h_ref)

    nb = h_ref.shape[0]
    head_ones = jnp.where(_head_ones(), 1.0, 0.0).astype(BF16)
    hs = [_split(h_ref[b]) for b in range(nb)]
    ys = [_dot3(_split(rh_ref[b]), hs[b]) + y0_ref[b] for b in range(nb)]
    for b in range(nb):
        h_ref[b] = _dot3((p_ref[b, 0, 0], p_ref[b, 0, 1]), hs[b]) + q_ref[b, 0]
    inv_n = 1.0 / HEAD
    for b in range(nb):
        y = ys[b]
        d = y - _dot_ones(y, head_ones) * inv_n
        var = _dot_ones(d * d, head_ones) * inv_n
        yn = d * lax.rsqrt(var + RW_GN_EPS) * gng_ref[...] + gnb_ref[...]
        o_ref[b] = (yn + bonus_ref[b]) * g_ref[b]


def _rw_scan(p, q, rh, y0, g, bonus, gn_g, gn_b):
    B, T, _ = rh.shape
    C = RW_CHUNK
    nc = T // C
    seq = pl.BlockSpec((B, C, RW_WIDTH), lambda c: (0, c, 0))
    vec = pl.BlockSpec((1, RW_WIDTH), lambda c: (0, 0))
    return pl.pallas_call(
        _rw_scan_body,
        grid=(nc,),
        in_specs=[pl.BlockSpec((B, 1, 2, RW_WIDTH, RW_WIDTH), lambda c: (0, c, 0, 0, 0)),
                  pl.BlockSpec((B, 1, RW_WIDTH, RW_WIDTH), lambda c: (0, c, 0, 0)),
                  seq, seq, seq, seq, vec, vec],
        out_specs=seq,
        out_shape=jax.ShapeDtypeStruct((B, T, RW_WIDTH), F32),
        scratch_shapes=[pltpu.VMEM((B, RW_WIDTH, RW_WIDTH), F32)],
        compiler_params=_cparams(("arbitrary",)),
        name="rwkv_chunk_scan",
    )(p, q, rh, y0, g, bonus, gn_g.reshape(1, -1), gn_b.reshape(1, -1))


def _pool_body(z_ref, zp_ref, w_ref, scale_ref, o_ref):
    i = pl.program_id(1)
    z = z_ref[0]
    tt = z.shape[0]
    halo = jnp.where(i == 0, 0.0, zp_ref[0])
    e = jnp.concatenate([halo, z], axis=0)
    sums = []
    s = e
    for shift in (1, 2, 4, 8):
        s = s + pltpu.roll(s, shift, axis=0)
        sums.append(s[POOL_HALO:, :])
    t_idx = i * tt + lax.broadcasted_iota(jnp.int32, (tt, POOL_WIDTH), 0)
    lane_group = lax.broadcasted_iota(jnp.int32, (tt, POOL_WIDTH), 1) // HEAD
    pooled = jnp.zeros_like(z)
    for gi, win in enumerate(POOL_WINDOWS):
        cnt = jnp.minimum(t_idx + 1, win).astype(F32)
        pooled = jnp.where(lane_group == gi, sums[gi] / cnt - z, pooled)
    o_ref[0] = _dot(pooled.astype(BF16), w_ref[...]) * scale_ref[...]


def _pool(z_pool, w_blockdiag_bf16, scale, tt=512):
    B, T, _ = z_pool.shape
    return pl.pallas_call(
        _pool_body,
        grid=(B, T // tt),
        in_specs=[pl.BlockSpec((1, tt, POOL_WIDTH), lambda b, i: (b, i, 0)),
                  pl.BlockSpec((1, POOL_HALO, POOL_WIDTH),
                               lambda b, i: (b, jnp.maximum(i * (tt // POOL_HALO) - 1, 0), 0)),
                  pl.BlockSpec((POOL_WIDTH, POOL_WIDTH), lambda b, i: (0, 0)),
                  pl.BlockSpec((1, POOL_WIDTH), lambda b, i: (0, 0))],
        out_specs=pl.BlockSpec((1, tt, POOL_WIDTH), lambda b, i: (b, i, 0)),
        out_shape=jax.ShapeDtypeStruct((B, T, POOL_WIDTH), F32),
        compiler_params=_cparams(("parallel", "parallel")),
        name="pool_mixer",
    )(z_pool, z_pool, w_blockdiag_bf16, scale.reshape(1, -1))


def _rope(x, cos_w, sin_w):
    w = x.shape[1]
    lane = lax.broadcasted_iota(jnp.int32, x.shape, 1) % HEAD
    partner = jnp.where(lane < HEAD // 2, pltpu.roll(x, w - HEAD // 2, axis=1), pltpu.roll(x, HEAD // 2, axis=1))
    return x * cos_w + partner * sin_w


def _head_lanes(a, h, dtype):
    part = a[:, h * HEAD:(h + 1) * HEAD]
    return jnp.concatenate([part, jnp.zeros_like(part)], axis=1).astype(dtype)


def _nsa_prep_body(zq_ref, zkc_ref, zks_ref, zvs_ref, zkw_ref, zvw_ref, zg_ref, cos_ref, sin_ref,
                   q_ref, kc_ref, ks_ref, vst_ref, kw_ref, vwt_ref, gt_ref):
    cos = cos_ref[0]
    sin = sin_ref[0]
    cos_q = jnp.concatenate([cos] * (NSA_WIDTH // LANE), axis=1)
    sin_q = jnp.concatenate([sin] * (NSA_WIDTH // LANE), axis=1)
    qr = _rope(zq_ref[0], cos_q, sin_q) * (HEAD ** -0.5 * LOG2E)
    for h in range(NSA_Q_HEADS):
        q_ref[0, h] = _head_lanes(qr, h, BF16)
    kc_ref[0] = _rope(zkc_ref[0], cos, sin)
    ks = _rope(zks_ref[0], cos, sin)
    kw = _rope(zkw_ref[0], cos, sin)
    vst = zvs_ref[0].T
    vwt = zvw_ref[0].T
    for h in range(NSA_KV_HEADS):
        sl = slice(h * HEAD, (h + 1) * HEAD)
        ks_ref[0, h] = _head_lanes(ks, h, BF16)
        kw_ref[0, h] = _head_lanes(kw, h, BF16)
        tail = lax.broadcasted_iota(jnp.int32, (V_ROWS - HEAD, vst.shape[1]), 0)
        ones_row = jnp.where(tail == 0, 1.0, 0.0).astype(BF16)
        vst_ref[0, h, 0:HEAD, :] = vst[sl, :].astype(BF16)
        vst_ref[0, h, HEAD:V_ROWS, :] = ones_row
        vwt_ref[0, h, 0:HEAD, :] = vwt[sl, :].astype(BF16)
        vwt_ref[0, h, HEAD:V_ROWS, :] = ones_row
    gt = _sigmoid(zg_ref[0]).T
    for h in range(NSA_KV_HEADS):
        gt_ref[0, h] = gt[16 * h:16 * (h + 1), :]


def _nsa_prep(z_q, z_kc, z_ks, z_vs, z_kw, z_vw, z_g, cos_t, sin_t, tt=512):
    B, T, _ = z_q.shape
    seq = lambda w: pl.BlockSpec((1, tt, w), lambda b, i: (b, i, 0))
    hm = lambda nh: pl.BlockSpec((1, nh, tt, LANE), lambda b, i: (b, 0, i, 0))
    tr = lambda rows: pl.BlockSpec((1, NSA_KV_HEADS, rows, tt), lambda b, i: (b, 0, 0, i))
    sds = jax.ShapeDtypeStruct
    return pl.pallas_call(
        _nsa_prep_body,
        grid=(B, T // tt),
        in_specs=[seq(NSA_WIDTH)] + [seq(NSA_KV)] * 6 + [seq(LANE), seq(LANE)],
        out_specs=(hm(NSA_Q_HEADS), seq(NSA_KV), hm(NSA_KV_HEADS), tr(V_ROWS), hm(NSA_KV_HEADS), tr(V_ROWS), tr(16)),
        out_shape=(sds((B, NSA_Q_HEADS, T, LANE), BF16), sds((B, T, NSA_KV), F32),
                   sds((B, NSA_KV_HEADS, T, LANE), BF16), sds((B, NSA_KV_HEADS, V_ROWS, T), BF16),
                   sds((B, NSA_KV_HEADS, T, LANE), BF16), sds((B, NSA_KV_HEADS, V_ROWS, T), BF16),
                   sds((B, NSA_KV_HEADS, 16, T), F32)),
        compiler_params=_cparams(("parallel", "parallel")),
        name="nsa_prep",
    )(z_q, z_kc, z_ks, z_vs, z_kw, z_vw, z_g, cos_t, sin_t)


def _nsa_cmp_body(gk_ref, gv_ref, pek_ref, pev_ref, wk_ref, wv_ref, kc_ref, vct_ref):
    half = NSA_KV

    def compress(g, pe_ref, w_ref):
        n = g.shape[0]
        lo = _dot((g + pe_ref[0:1, :]).astype(BF16), w_ref[:, 0:half])
        hi = _dot((g + pe_ref[1:2, :]).astype(BF16), w_ref[:, half:2 * half])
        hi_next = pltpu.roll(hi, n - 1, axis=0)
        row = lax.broadcasted_iota(jnp.int32, lo.shape, 0)
        return lo + jnp.where(row == n - 1, 0.0, hi_next)

    kc = compress(gk_ref[0], pek_ref, wk_ref)
    vct = compress(gv_ref[0], pev_ref, wv_ref).T
    for h in range(NSA_KV_HEADS):
        sl = slice(h * HEAD, (h + 1) * HEAD)
        kc_ref[0, h] = _head_lanes(kc, h, BF16)
        vct_ref[0, h] = vct[sl, :].astype(BF16)


def _nsa_compress(kc_groups, vc_groups, pe_k2, pe_v2, wk2, wv2):
    B, ncp, gw = kc_groups.shape
    grp = pl.BlockSpec((1, ncp, gw), lambda b: (b, 0, 0))
    pe = pl.BlockSpec((2, gw), lambda b: (0, 0))
    wspec = pl.BlockSpec((gw, 2 * NSA_KV), lambda b: (0, 0))
    sds = jax.ShapeDtypeStruct
    return pl.pallas_call(
        _nsa_cmp_body,
        grid=(B,),
        in_specs=[grp, grp, pe, pe, wspec, wspec],
        out_specs=(pl.BlockSpec((1, NSA_KV_HEADS, ncp, LANE), lambda b: (b, 0, 0, 0)),
                   pl.BlockSpec((1, NSA_KV_HEADS, HEAD, ncp), lambda b: (b, 0, 0, 0))),
        out_shape=(sds((B, NSA_KV_HEADS, ncp, LANE), BF16), sds((B, NSA_KV_HEADS, HEAD, ncp), BF16)),
        compiler_params=_cparams(("parallel",)),
        name="nsa_compress",
    )(kc_groups, vc_groups, pe_k2, pe_v2, wk2, wv2)


def _nsa_attn_body(q_ref, kc_ref, vct_ref, ks_ref, vst_ref, kw_ref, vwt_ref, gt_ref, mcs_ref, blk_ref,
                   o_ref, sa_ref, sb_ref, sc_ref, sd_ref):
    jb = pl.program_id(2)
    s0 = jb * Q_BLOCK
    cols = NSA_GQA * Q_BLOCK
    q = q_ref[0].reshape(cols, LANE)
    ncp = kc_ref.shape[2]

    sc = _dot_nt(kc_ref[0, 0], q)
    n_idx = lax.broadcasted_iota(jnp.int32, (ncp, cols), 0)
    t_c = s0 + lax.broadcasted_iota(jnp.int32, (ncp, cols), 1) % Q_BLOCK
    sc = jnp.where(CMP_STRIDE * n_idx + (CMP_BLOCK - 1) <= t_c, sc, NEG)
    e = jnp.exp2(sc - jnp.max(sc, axis=0, keepdims=True))
    any_valid = jnp.where(t_c[0:1, :] >= CMP_BLOCK - 1, 1.0, 0.0)
    pc = e * (any_valid / jnp.sum(e, axis=0, keepdims=True))
    o_cmp = _dot(vct_ref[0, 0], pc.astype(BF16))
    psum = pc[:, 0:Q_BLOCK]
    for g in range(1, NSA_GQA):
        psum = psum + pc[:, g * Q_BLOCK:(g + 1) * Q_BLOCK]
    p_hi = psum.astype(BF16)
    rest = psum - p_hi.astype(F32)
    p_mid = rest.astype(BF16)
    p_lo = (rest - p_mid.astype(F32)).astype(BF16)
    mcs = mcs_ref[...]
    imp = _dot(mcs, p_hi) + (_dot(mcs, p_mid) + _dot(mcs, p_lo))

    wkeys = WINDOW + Q_BLOCK
    off_w = pl.multiple_of(jnp.maximum(s0 - WINDOW, 0), Q_BLOCK)
    sw = _dot_nt(kw_ref[0, 0, pl.ds(off_w, wkeys), :], q)
    kpos = off_w + lax.broadcasted_iota(jnp.int32, (wkeys, cols), 0)
    t_w = s0 + lax.broadcasted_iota(jnp.int32, (wkeys, cols), 1) % Q_BLOCK
    sw = jnp.where(kpos <= t_w, jnp.where(kpos > t_w - WINDOW, sw, NEG), NEG)
    pw = jnp.exp2((sw - jnp.max(sw, axis=0, keepdims=True)).astype(BF16))
    acc_w = _dot(vwt_ref[0, 0, :, pl.ds(off_w, wkeys)], pw)

    j_idx = lax.broadcasted_iota(jnp.int32, (NSEL_PAD, Q_BLOCK), 0)
    jc = (s0 + lax.broadcasted_iota(jnp.int32, (NSEL_PAD, Q_BLOCK), 1)) // SEL_BLOCK
    causal = j_idx <= jc
    forced_imp = jnp.where(j_idx == 0, FORCE, jnp.where(j_idx == jc, FORCE, jnp.where(j_idx == jc - 1, FORCE, imp)))
    work = jnp.where(causal, forced_imp, NEG)
    j_f = j_idx.astype(F32)
    for _ in range(SEL_TOPN):
        top = jnp.max(work, axis=0, keepdims=True)
        first = jnp.min(jnp.where(work == top, j_f, float(NSEL_PAD)), axis=0, keepdims=True)
        work = jnp.where(j_f == first, -jnp.inf, work)
    bias = jnp.where(causal, jnp.where(work == -jnp.inf, 0.0, NEG), NEG)

    bias_q = bias.T.astype(BF16)
    q_aug = jnp.concatenate([q, jnp.concatenate([bias_q] * NSA_GQA, axis=0)], axis=1)

    def sel_scores(off):
        k_aug = jnp.concatenate([ks_ref[0, 0, pl.ds(off, KEY_GROUP), :], blk_ref[pl.ds(off, KEY_GROUP), :]], axis=1)
        return _dot_nt(k_aug, q_aug)

    def flash_update(v_t, st, carry):
        m, acc = carry
        m_new = jnp.maximum(m, jnp.max(st, axis=0, keepdims=True))
        p = jnp.exp2((st - m_new).astype(BF16))
        return m_new, acc * jnp.exp2(m - m_new) + _dot(v_t, p)

    init =(jnp.full((1, cols), NEG, F32), jnp.zeros((V_ROWS, cols), F32))
    n_groups = s0 // KEY_GROUP + 1
    n_pairs = n_groups // 2
    t_q = s0 + lax.broadcasted_iota(jnp.int32, (Q_BLOCK, cols), 1) % Q_BLOCK
    r_q = lax.broadcasted_iota(jnp.int32, (Q_BLOCK, cols), 0)

    def put_scores(ref, g):
        off = pl.multiple_of(g * KEY_GROUP, KEY_GROUP)
        ref[...] = sel_scores(off)
        r = pl.multiple_of(jnp.clip(s0 - off, 0, KEY_GROUP - Q_BLOCK), Q_BLOCK)
        tile = ref[pl.ds(r, Q_BLOCK), :]
        ref[pl.ds(r, Q_BLOCK), :] = jnp.where(off + r + r_q <= t_q, tile, NEG)

    def consume(ref, g, carry):
        off = pl.multiple_of(g * KEY_GROUP, KEY_GROUP)
        return flash_update(vst_ref[0, 0, :, pl.ds(off, KEY_GROUP)], ref[...], carry)

    last = n_groups - 1
    buf0, buf1 = (sa_ref, sb_ref), (sc_ref, sd_ref)
    put_scores(sa_ref, 0)
    put_scores(sb_ref, jnp.minimum(1, last))

    def stage(cur, nxt, pair, carries):
        put_scores(nxt[0], jnp.minimum(2 * pair + 2, last))
        put_scores(nxt[1], jnp.minimum(2 * pair + 3, last))
        return consume(cur[0], 2 * pair, carries[0]), consume(cur[1], 2 * pair + 1, carries[1])

    def two_pairs(i, carries):
        return stage(buf1, buf0, 2 * i + 1, stage(buf0, buf1, 2 * i, carries))

    carries = lax.fori_loop(0, n_pairs // 2, two_pairs, (init, init))
    odd_pairs = n_pairs % 2
    c_even, c_odd = lax.fori_loop(n_pairs - odd_pairs, n_pairs, lambda pr, c: stage(buf0, buf1, pr, c), carries)
    left = n_groups - 2 * n_pairs
    c_even = lax.fori_loop(0, left * (1 - odd_pairs), lambda _, c: consume(sa_ref, last, c), c_even)
    c_even = lax.fori_loop(0, left * odd_pairs, lambda _, c: consume(sc_ref, last, c), c_even)
    m_e, acc_e = c_even
    m_o, acc_o = c_odd
    m_s = jnp.maximum(m_e, m_o)
    acc_s = acc_e * jnp.exp2(m_e - m_s) + acc_o * jnp.exp2(m_o - m_s)

    def gate_row(branch):
        return jnp.concatenate([gt_ref[0, 0, 3 * g + branch:3 * g + branch + 1, :] for g in range(NSA_GQA)], axis=1)

    o_t = (gate_row(0) * o_cmp + (gate_row(1) / acc_s[HEAD:HEAD + 1]) * acc_s[0:HEAD]
           + (gate_row(2) / acc_w[HEAD:HEAD + 1]) * acc_w[0:HEAD])
    stacked = jnp.concatenate([o_t[:, g * Q_BLOCK:(g + 1) * Q_BLOCK] for g in range(NSA_GQA)], axis=0)
    o_ref[0] = stacked.T


def _nsa_attention(q_r, kcmp, vcmp_t, ks_r, vs_t, kw_r, vw_t, g_t, mcs_t, blk_onehot):
    B, _, T, _ = q_r.shape
    ncp = kcmp.shape[2]
    nqb = T // Q_BLOCK
    assert T % KEY_GROUP == 0 and T >= WINDOW + Q_BLOCK and T // SEL_BLOCK <= NSEL_PAD
    kv_rows = lambda n: pl.BlockSpec((1, 1, n, LANE), lambda b, h, j: (b, h, 0, 0))
    kv_cols = lambda n, rows: pl.BlockSpec((1, 1, rows, n), lambda b, h, j: (b, h, 0, 0))
    return pl.pallas_call(
        _nsa_attn_body,
        grid=(B, NSA_KV_HEADS, nqb),
        in_specs=[pl.BlockSpec((1, NSA_GQA, Q_BLOCK, LANE), lambda b, h, j: (b, h, j, 0)),
                  kv_rows(ncp), kv_cols(ncp, HEAD), kv_rows(T), kv_cols(T, V_ROWS), kv_rows(T), kv_cols(T, V_ROWS),
                  pl.BlockSpec((1, 1, 16, Q_BLOCK), lambda b, h, j: (b, h, 0, j)),
                  pl.BlockSpec((NSEL_PAD, ncp), lambda b, h, j: (0, 0)),
                  pl.BlockSpec((T, NSEL_PAD), lambda b, h, j: (0, 0))],
        out_specs=pl.BlockSpec((1, Q_BLOCK, NSA_GQA * HEAD), lambda b, h, j: (b, j, h)),
        out_shape=jax.ShapeDtypeStruct((B, T, NSA_WIDTH), F32),
        scratch_shapes=[pltpu.VMEM((KEY_GROUP, NSA_GQA * Q_BLOCK), F32)] * 4,
        compiler_params=_cparams(("parallel", "parallel", "arbitrary")),
        name="nsa_attention",
    )(q_r, kcmp, vcmp_t, ks_r, vs_t, kw_r, vw_t, g_t, mcs_t, blk_onehot)


def _out_proj_body(x_ref, ya_ref, yb_ref, yc_ref, wa_ref, wb_ref, wc_ref, o_ref):
    acc = _dot(ya_ref[...].astype(BF16), wa_ref[...])
    acc = acc + _dot(yb_ref[...].astype(BF16), wb_ref[...])
    acc = acc + _dot(yc_ref[...].astype(BF16), wc_ref[...])
    o_ref[...] = x_ref[...] + acc


def _out_proj(x2, ya, yb, yc, w_out_bf16, tm=512):
    m, d = x2.shape
    wa, wb, wc = w_out_bf16[:RW_WIDTH], w_out_bf16[RW_WIDTH:RW_WIDTH + POOL_WIDTH], w_out_bf16[RW_WIDTH + POOL_WIDTH:]
    tile = lambda w: pl.BlockSpec((tm, w), lambda i: (i, 0))
    wfull = lambda w: pl.BlockSpec((w, d), lambda i: (0, 0))
    return pl.pallas_call(
        _out_proj_body,
        grid=(m // tm,),
        in_specs=[tile(d), tile(RW_WIDTH), tile(POOL_WIDTH), tile(NSA_WIDTH),
                  wfull(RW_WIDTH), wfull(POOL_WIDTH), wfull(NSA_WIDTH)],
        out_specs=tile(d),
        out_shape=jax.ShapeDtypeStruct((m, d), F32),
        compiler_params=_cparams(("parallel",)),
        name="out_proj_residual",
    )(x2, ya, yb, yc, wa, wb, wc)


def _gelu_tanh(x):
    return 0.5 * x * (1.0 + jnp.tanh(math.sqrt(2.0 / math.pi) * (x + 0.044715 * (x * x * x))))


def _ffn_body(x_ref, g_ref, wup_ref, cw_ref, cb_ref, wdn_ref, o_ref, carry_ref, *, tiles_per_seq, f_chunk):
    i = pl.program_id(0)
    x = x_ref[...]
    tm = x.shape[0]
    d_ff = wdn_ref.shape[0]
    h = _rms(x, g_ref[...]).astype(BF16)
    first = (i % tiles_per_seq) == 0
    row = lax.broadcasted_iota(jnp.int32, (tm, f_chunk), 0)

    def conv(col0):
        u = _dot(h, wup_ref[:, col0:col0 + f_chunk])
        p2 = jnp.where(first, 0.0, carry_ref[6:7, col0:col0 + f_chunk])
        p1 = jnp.where(first, 0.0, carry_ref[7:8, col0:col0 + f_chunk])
        u1 = jnp.where(row == 0, p1, pltpu.roll(u, 1, axis=0))
        u2 = jnp.where(row == 0, p2, jnp.where(row == 1, p1, pltpu.roll(u, 2, axis=0)))
        carry_ref[:, col0:col0 + f_chunk] = u[tm - 8:tm, :]
        cw = cw_ref[:, col0:col0 + f_chunk]
        return cw[0:1] * u2 + cw[1:2] * u1 + cw[2:3] * u + cb_ref[:, col0:col0 + f_chunk]

    acc = x
    for c0 in range(0, d_ff, f_chunk):
        act = _gelu_tanh(conv(c0)) * conv(d_ff + c0)
        acc = acc + _dot(act.astype(BF16), wdn_ref[c0:c0 + f_chunk, :])
    o_ref[...] = acc


def _conv_ffn(x2, g, w_up_bf16, conv_w, conv_b, w_down_bf16, seq_len, tm=512, f_chunk=1408):
    m, d = x2.shape
    f2 = w_up_bf16.shape[1]
    d_ff = f2 // 2
    assert d_ff % f_chunk == 0 and f_chunk % LANE == 0 and seq_len % tm == 0
    full = lambda shp: pl.BlockSpec(shp, lambda i: (0,) * len(shp))
    return pl.pallas_call(
        functools.partial(_ffn_body, tiles_per_seq=seq_len // tm, f_chunk=f_chunk),
        grid=(m // tm,),
        in_specs=[pl.BlockSpec((tm, d), lambda i: (i, 0)), full((1, d)), full((d, f2)),
                  full((CONV_W, f2)), full((1, f2)), full((d_ff, d))],
        out_specs=pl.BlockSpec((tm, d), lambda i: (i, 0)),
        out_shape=jax.ShapeDtypeStruct((m, d), F32),
        scratch_shapes=[pltpu.VMEM((8, f2), F32)],
        compiler_params=_cparams(("arbitrary",)),
        name="conv_ffn",
    )(x2, g.reshape(1, d), w_up_bf16, conv_w.reshape(CONV_W, f2), conv_b.reshape(1, f2), w_down_bf16)


def _ple_body(x_ref, p_ref, g_ref, wg_ref, wp_ref, gf_ref, o_ref, *, final):
    x = x_ref[...]
    gate = _sigmoid(_dot(_rms(x, g_ref[...]).astype(BF16), wg_ref[...]))
    y = x + _dot(p_ref[...].astype(BF16), wp_ref[...]) * gate
    o_ref[...] = _rms(y, gf_ref[...]) if final else y


def _ple(x2, p2, g, w_gate_bf16, w_proj_bf16, g_final, final, tm=512):
    m, d = x2.shape
    pd = p2.shape[1]
    full = lambda shp: pl.BlockSpec(shp, lambda i: (0,) * len(shp))
    return pl.pallas_call(
        functools.partial(_ple_body, final=final),
        grid=(m // tm,),
        in_specs=[pl.BlockSpec((tm, d), lambda i: (i, 0)), pl.BlockSpec((tm, pd), lambda i: (i, 0)),
                  full((1, d)), full((d, d)), full((pd, d)), full((1, d))],
        out_specs=pl.BlockSpec((tm, d), lambda i: (i, 0)),
        out_shape=jax.ShapeDtypeStruct((m, d), F32),
        compiler_params=_cparams(("parallel",)),
        name="ple_final_norm" if final else "ple",
    )(x2, p2, g.reshape(1, d), w_gate_bf16, w_proj_bf16, g_final.reshape(1, d))


def _overlap_matrix(seq_len):
    ncp = seq_len // CMP_STRIDE
    n_cmp = (seq_len - CMP_BLOCK) // CMP_STRIDE + 1
    cs = CMP_STRIDE * np.arange(n_cmp)
    ss = SEL_BLOCK * np.arange(seq_len // SEL_BLOCK)
    ov = (np.minimum(cs[:, None] + CMP_BLOCK - 1, ss[None] + SEL_BLOCK - 1) - np.maximum(cs[:, None], ss[None]) + 1)
    m = np.zeros((NSEL_PAD, ncp), np.float32)
    m[:ss.size, :n_cmp] = (np.clip(ov, 0, CMP_BLOCK).astype(np.float32) / CMP_BLOCK).T
    return jnp.asarray(m, dtype=BF16)


def _compress_weight(w):
    halves = w.reshape(2, CMP_STRIDE, HEAD, HEAD)
    eye = jnp.eye(NSA_KV_HEADS, dtype=w.dtype)
    w2 = jnp.einsum('sldf,hg->lhdsgf', halves, eye)
    return w2.reshape(CMP_STRIDE * NSA_KV, 2 * NSA_KV).astype(BF16)


def _compress_pe(pe):
    halves = pe.reshape(2, CMP_STRIDE, 1, HEAD)
    return jnp.broadcast_to(halves, (2, CMP_STRIDE, NSA_KV_HEADS, HEAD)).reshape(2, CMP_STRIDE * NSA_KV)


def _in_proj_layout(w_in):
    d = w_in.shape[0]
    gate0 = RW_IN + POOL_WIDTH + NSA_WIDTH + 6 * NSA_KV
    per_head = NSA_GQA * 3
    gates = jnp.zeros((d, LANE), w_in.dtype)
    for h in range(NSA_KV_HEADS):
        gates = gates.at[:, 16 * h:16 * h + per_head].set(w_in[:, gate0 + per_head * h:gate0 + per_head * (h + 1)])
    return jnp.concatenate([w_in[:, :gate0], gates], axis=1).astype(BF16)


_IN_SPLITS = ((0, RW_IN), (RW_IN, RW_IN + POOL_WIDTH), (RW_IN + POOL_WIDTH, RW_IN + POOL_WIDTH + NSA_WIDTH)) + tuple(
    (RW_IN + POOL_WIDTH + NSA_WIDTH + i * NSA_KV, RW_IN + POOL_WIDTH + NSA_WIDTH + (i + 1) * NSA_KV) for i in range(7))


def kernel(x, p, positions, g_mix, w_in, rw_mu, rw_w0, rw_w_up, rw_a0, rw_a_up, rw_g_up, rw_k_k, rw_k_a, rw_r_k, rw_gn_g, rw_gn_b, pool_w, pool_scale, nsa_pe_k, nsa_pe_v, nsa_w_ck, nsa_w_cv, w_out, g_ffn, ffn_w_up, ffn_conv_w, ffn_conv_b, ffn_w_down, g_ple, ple_w_gate, ple_w_proj, g_final):
    B, T, D = x.shape
    depth = w_in.shape[0]
    M = B * T

    half = HEAD // 2
    inv = ROPE_THETA ** (-jnp.arange(half, dtype=F32) / half)
    ang = positions.astype(F32)[..., None] * inv
    cos, sin = jnp.cos(ang), jnp.sin(ang)
    cos_t = jnp.concatenate([cos, cos] * (LANE // HEAD), axis=-1)
    sin_t = jnp.concatenate([-sin, sin] * (LANE // HEAD), axis=-1)
    mcs_t = _overlap_matrix(T)
    blk_onehot = (jnp.arange(T)[:, None] // SEL_BLOCK == jnp.arange(NSEL_PAD)[None, :]).astype(BF16)

    x2 = x.reshape(M, D)
    for i in range(depth):
        zs = _norm_matmul(x2, g_mix[i], _in_proj_layout(w_in[i]), _IN_SPLITS)
        z_rw, z_pool, z_q, z_kc, z_vc, z_ks, z_vs, z_kw, z_vw, z_g = (
            z.reshape(B, T, z.shape[1]) for z in zs)

        pm, qm, rh, y0, gate, bonus = _rw_chunks(z_rw, rw_mu[i], rw_w0[i], rw_w_up[i], rw_a0[i], rw_a_up[i],
                                                 rw_g_up[i], rw_k_k[i], rw_k_a[i], rw_r_k[i].reshape(-1))
        y_a = _rw_scan(pm, qm, rh, y0, gate, bonus, rw_gn_g[i], rw_gn_b[i])

        w_pool_bd = jax.scipy.linalg.block_diag(*[pool_w[i, gi] for gi in range(pool_w.shape[1])]).astype(BF16)
        y_b = _pool(z_pool, w_pool_bd, pool_scale[i])

        q_r, kc_r, ks_r, vs_t, kw_r, vw_t, g_t = _nsa_prep(z_q, z_kc, z_ks, z_vs, z_kw, z_vw, z_g, cos_t, sin_t)
        grp = (B, T // CMP_STRIDE, CMP_STRIDE * NSA_KV)
        kcmp, vcmp_t = _nsa_compress(kc_r.reshape(grp), z_vc.reshape(grp),
                                     _compress_pe(nsa_pe_k[i]), _compress_pe(nsa_pe_v[i]),
                                     _compress_weight(nsa_w_ck[i]), _compress_weight(nsa_w_cv[i]))
        y_c = _nsa_attention(q_r, kcmp, vcmp_t, ks_r, vs_t, kw_r, vw_t, g_t, mcs_t, blk_onehot)

        x2 = _out_proj(x2, y_a.reshape(M, -1), y_b.reshape(M, -1), y_c.reshape(M, -1), w_out[i].astype(BF16))
        x2 = _conv_ffn(x2, g_ffn[i], ffn_w_up[i].astype(BF16), ffn_conv_w[i], ffn_conv_b[i],
                       ffn_w_down[i].astype(BF16), T)
        x2 = _ple(x2, p[i].reshape(M, -1), g_ple[i], ple_w_gate[i].astype(BF16), ple_w_proj[i].astype(BF16),
                  g_final, final=(i == depth - 1))
    return x2.reshape(B, T, D)
```

```python
import functools
import math

import numpy as np
import jax
import jax.numpy as jnp
from jax import lax
from jax.experimental import pallas as pl
from jax.experimental.pallas import tpu as pltpu

F32 = jnp.float32
BF16 = jnp.bfloat16

RMS_EPS = 1e-6
ROPE_THETA = 10000.0
HEAD = 64
RW_HEADS = 4
RW_WIDTH = RW_HEADS * HEAD
RW_DECAY_LORA = 64
RW_AAA_LORA = 64
RW_GATE_LORA = 128
RW_GN_EPS = 64e-5
RW_IN = 3 * RW_WIDTH + RW_DECAY_LORA + RW_AAA_LORA + RW_GATE_LORA
POOL_WIDTH = 256
POOL_WINDOWS = (2, 4, 8, 16)
POOL_HALO = 16
NSA_Q_HEADS = 8
NSA_KV_HEADS = 2
NSA_GQA = NSA_Q_HEADS // NSA_KV_HEADS
NSA_WIDTH = NSA_Q_HEADS * HEAD
NSA_KV = NSA_KV_HEADS * HEAD
CMP_BLOCK = 32
CMP_STRIDE = 16
SEL_BLOCK = 64
SEL_TOPN = 16
WINDOW = 512
Q_BLOCK = 128
KEY_GROUP = 512
V_ROWS = HEAD + 16
NSEL_PAD = 128
NEG = -1e30
LOG2E = math.log2(math.e)
FORCE = 1e9
CONV_W = 3
LANE = 128
VMEM_LIMIT = 56 * 1024 * 1024

RW_CHUNK = 64

def _cparams(sem):
    return pltpu.CompilerParams(dimension_semantics=sem, vmem_limit_bytes=VMEM_LIMIT)


def _dot(a, b, prec=None):
    return lax.dot_general(a, b, (((1,), (0,)), ((), ())), precision=prec, preferred_element_type=F32)


def _dot_nt(a, b, prec=None):
    return lax.dot_general(a, b, (((1,), (1,)), ((), ())), precision=prec, preferred_element_type=F32)


def _sigmoid(x):
    return 1.0 / (1.0 + jnp.exp(-x))


def _rms(x, g):
    ms = jnp.mean(x * x, axis=-1, keepdims=True)
    return (x * lax.rsqrt(ms + RMS_EPS)) * g


def _norm_mm_body(x_ref, g_ref, w_ref, *o_refs, splits):
    h = _rms(x_ref[...], g_ref[...]).astype(BF16)
    for o_ref, (lo, hi) in zip(o_refs, splits):
        o_ref[...] = _dot(h, w_ref[:, lo:hi])


def _norm_matmul(x2, g, w_bf16, splits, tm=512):
    m, d = x2.shape
    n = w_bf16.shape[1]
    outs = tuple(jax.ShapeDtypeStruct((m, hi - lo), F32) for lo, hi in splits)
    return pl.pallas_call(
        functools.partial(_norm_mm_body, splits=splits),
        grid=(m // tm,),
        in_specs=[pl.BlockSpec((tm, d), lambda i: (i, 0)),
                  pl.BlockSpec((1, d), lambda i: (0, 0)),
                  pl.BlockSpec((d, n), lambda i: (0, 0))],
        out_specs=tuple(pl.BlockSpec((tm, hi - lo), lambda i: (i, 0)) for lo, hi in splits),
        out_shape=outs,
        compiler_params=_cparams(("parallel",)),
        name="norm_in_proj",
    )(x2, g.reshape(1, d), w_bf16)


def _split(x):
    hi = x.astype(BF16)
    return hi, (x - hi.astype(F32)).astype(BF16)


def _dot3(a, b, dot=_dot):
    return dot(a[0], b[0]) + (dot(a[0], b[1]) + dot(a[1], b[0]))


def _dot_ones(x, ones_bf16):
    hi, lo = _split(x)
    return _dot(hi, ones_bf16) + _dot(lo, ones_bf16)


def _head_ones():
    hr = lax.broadcasted_iota(jnp.int32, (RW_WIDTH, RW_WIDTH), 0) // HEAD
    hc = lax.broadcasted_iota(jnp.int32, (RW_WIDTH, RW_WIDTH), 1) // HEAD
    return hr == hc


def _rw_chunk_body(z_ref, zp_ref, mu_ref, w0_ref, wup_ref, a0_ref, aup_ref, gup_ref, kk_ref, ka_ref,
                   rk_ref, p_ref, q_ref, rh_ref, y0_ref, g_ref, bonus_ref):
    i = pl.program_id(1)
    C = RW_CHUNK
    W = RW_WIDTH
    z = z_ref[0]
    rows = z.shape[0]
    n_chunks = rows // C
    prev = jnp.where(i == 0, 0.0, zp_ref[0, 7:8, :])
    row_z = lax.broadcasted_iota(jnp.int32, z.shape, 0)
    zs = jnp.where(row_z == 0, prev, pltpu.roll(z, 1, axis=0))
    zf = z + (zs - z) * mu_ref[...]
    r = zf[:, 0:W]
    k = zf[:, W:2 * W]
    v = zf[:, 2 * W:3 * W]
    zw = zf[:, 3 * W:3 * W + RW_DECAY_LORA]
    za = zf[:, 3 * W + RW_DECAY_LORA:3 * W + RW_DECAY_LORA + RW_AAA_LORA]
    zg = zf[:, 3 * W + RW_DECAY_LORA + RW_AAA_LORA:]

    lora = lambda x, w_ref: _dot3(_split(x), (w_ref[0], w_ref[1]))
    xw = -(w0_ref[...] + lora(jnp.tanh(zw), wup_ref))
    softplus = jnp.maximum(xw, 0.0) + jnp.log(1.0 + jnp.exp(-jnp.abs(xw)))
    ld = -jnp.exp(-softplus - 0.5)
    a = _sigmoid(a0_ref[...] + lora(za, aup_ref))
    g_ref[0] = lora(_sigmoid(zg), gup_ref)
    kk = k * kk_ref[...]
    k2 = k * (1.0 + (a - 1.0) * ka_ref[...])

    same_head = _head_ones()
    head_ones = jnp.where(same_head, 1.0, 0.0).astype(BF16)
    kkn = kk / jnp.maximum(jnp.sqrt(_dot_ones(kk * kk, head_ones)), 1e-12)
    bonus_ref[0] = _dot_ones(r * k2 * rk_ref[...], head_ones) * v

    t_in = lax.broadcasted_iota(jnp.int32, (rows, W), 0) % C
    lc = ld
    shift = 1
    while shift < C:
        lc = lc + jnp.where(t_in >= shift, pltpu.roll(lc, shift, axis=0), 0.0)
        shift *= 2

    tr = lax.broadcasted_iota(jnp.int32, (W, W), 0) % C
    tc = lax.broadcasted_iota(jnp.int32, (W, W), 1) % C
    lower_incl = tr >= tc
    lower_strict = tr > tc
    eye = jnp.where(same_head, jnp.where(tr == tc, 1.0, 0.0), 0.0)

    def embed(x):
        return jnp.where(same_head, jnp.concatenate([x] * RW_HEADS, axis=0), 0.0)

    bf = lambda x: x.astype(BF16)
    A, Bt, Kt, Rt, V, BhT, KhT, gdiag, Rt_raw = [], [], [], [], [], [], [], [], []
    for c in range(n_chunks):
        rs = slice(c * C, (c + 1) * C)
        lc_c, ld_c = lc[rs], ld[rs]
        lc_end = lc_c[C - 1:C, :]
        ginv = jnp.exp(-lc_c)
        dec_end = jnp.exp(lc_end - lc_c)
        kkn_c, a_c, k2_c = kkn[rs], a[rs], k2[rs]
        r_t = r[rs] * jnp.exp(lc_c)
        A.append(bf(embed(-kkn_c * jnp.exp(lc_c - ld_c))))
        Bt.append(bf(embed(kkn_c * a_c * ginv)))
        Kt.append(bf(embed(k2_c * ginv)))
        Rt.append(bf(embed(r_t)))
        Rt_raw.append(r_t)
        V.append(bf(embed(v[rs])))
        BhT.append(bf(embed(kkn_c * a_c * dec_end).T))
        KhT.append(bf(embed(k2_c * dec_end).T))
        gdiag.append(jnp.where(eye > 0.5, jnp.broadcast_to(jnp.exp(lc_end), (W, W)), 0.0))

    each = lambda f, *lists: [f(*xs) for xs in zip(*lists)]
    L = each(lambda x, y: jnp.where(lower_strict, _dot_nt(x, y), 0.0), A, Bt)
    Lak = each(lambda x, y: bf(jnp.where(lower_strict, _dot_nt(x, y), 0.0)), A, Kt)
    Mrb = each(lambda x, y: bf(jnp.where(lower_incl, _dot_nt(x, y), 0.0)), Rt, Bt)
    Mrk = each(lambda x, y: bf(jnp.where(lower_incl, _dot_nt(x, y), 0.0)), Rt, Kt)
    Tm = [eye + x for x in L]
    Pw = each(bf, L)
    for _ in range(int(math.log2(C)) - 1):
        Pw = each(lambda x: bf(_dot(x, x)), Pw)
        Tm = each(lambda t, x: t + _dot(bf(t), x), Tm, Pw)
    Tm = each(bf, Tm)
    Wm = each(lambda t, x: bf(_dot(t, x)), Tm, A)
    LakV = each(lambda x, y: bf(_dot(x, y)), Lak, V)
    U0 = each(lambda t, x: bf(_dot(t, x)), Tm, LakV)
    Pm = each(lambda d, x, y: d + _dot(x, y), gdiag, BhT, Wm)
    Qm = each(lambda x, y, s, t: _dot(x, y) + _dot(s, t), BhT, U0, KhT, V)
    Rh = each(_dot, Mrb, Wm)
    Y0 = each(lambda x, y, s, t: _dot(x, y) + _dot(s, t), Mrb, U0, Mrk, V)

    def flatten(x_bd):
        out = x_bd[0:C]
        for h in range(1, RW_HEADS):
            out = out + x_bd[h * C:(h + 1) * C]
        return out

    for c in range(n_chunks):
        rs = slice(c * C, (c + 1) * C)
        p_hi, p_lo = _split(Pm[c])
        p_ref[0, c, 0] = p_hi
        p_ref[0, c, 1] = p_lo
        q_ref[0, c] = Qm[c]
        rh_ref[0, rs, :] = Rt_raw[c] + flatten(Rh[c])
        y0_ref[0, rs, :] = flatten(Y0[c])


def _rw_chunks(z_rw, mu, w0, w_up, a0, a_up, g_up, k_k, k_a, r_k, chunks_per_step=8):
    B, T, _ = z_rw.shape
    C = RW_CHUNK
    assert C == HEAD, "the chunk build shares one index grid between time and head-dim masks"
    rows = C * chunks_per_step
    nc = T // C
    row = lambda a: a.reshape(1, -1)
    pair = lambda w: jnp.stack(_split(w))
    full = lambda shp: pl.BlockSpec(shp, lambda b, i: (0,) * len(shp))
    seq = lambda w: pl.BlockSpec((1, rows, w), lambda b, i: (b, i, 0))
    sds = jax.ShapeDtypeStruct
    seq_shape = sds((B, T, RW_WIDTH), F32)
    return pl.pallas_call(
        _rw_chunk_body,
        grid=(B, T // rows),
        in_specs=[seq(RW_IN),
                  pl.BlockSpec((1, 8, RW_IN), lambda b, i: (b, jnp.maximum(i * (rows // 8) - 1, 0), 0)),
                  full((1, RW_IN)), full((1, RW_WIDTH)), full((2, RW_DECAY_LORA, RW_WIDTH)),
                  full((1, RW_WIDTH)), full((2, RW_AAA_LORA, RW_WIDTH)), full((2, RW_GATE_LORA, RW_WIDTH)),
                  full((1, RW_WIDTH)), full((1, RW_WIDTH)), full((1, RW_WIDTH))],
        out_specs=(pl.BlockSpec((1, chunks_per_step, 2, RW_WIDTH, RW_WIDTH), lambda b, i: (b, i, 0, 0, 0)),
                   pl.BlockSpec((1, chunks_per_step, RW_WIDTH, RW_WIDTH), lambda b, i: (b, i, 0, 0)),
                   seq(RW_WIDTH), seq(RW_WIDTH), seq(RW_WIDTH), seq(RW_WIDTH)),
        out_shape=(sds((B, nc, 2, RW_WIDTH, RW_WIDTH), BF16), sds((B, nc, RW_WIDTH, RW_WIDTH), F32),
                   seq_shape, seq_shape, seq_shape, seq_shape),
        compiler_params=_cparams(("parallel", "parallel")),
        name="rwkv_chunk_build",
    )(z_rw, z_rw, row(mu), row(w0), pair(w_up), row(a0), pair(a_up), pair(g_up), row(k_k), row(k_a), row(r_k))


def _rw_scan_body(p_ref, q_ref, rh_ref, y0_ref, g_ref, bonus_ref, gng_ref, gnb_ref, o_ref, h_ref):
    @pl.when(pl.program_id(0) == 0)
    def _():
        h_ref[...] = jnp.zeros_like(h_ref)

    nb = h_ref.shape[0]
    head_ones = jnp.where(_head_ones(), 1.0, 0.0).astype(BF16)
    hs = [_split(h_ref[b]) for b in range(nb)]
    ys = [_dot3(_split(rh_ref[b]), hs[b]) + y0_ref[b] for b in range(nb)]
    for b in range(nb):
        h_ref[b] = _dot3((p_ref[b, 0, 0], p_ref[b, 0, 1]), hs[b]) + q_ref[b, 0]
    inv_n = 1.0 / HEAD
    for b in range(nb):
        y = ys[b]
        d = y - _dot_ones(y, head_ones) * inv_n
        var = _dot_ones(d * d, head_ones) * inv_n
        yn = d * lax.rsqrt(var + RW_GN_EPS) * gng_ref[...] + gnb_ref[...]
        o_ref[b] = (yn + bonus_ref[b]) * g_ref[b]


def _rw_scan(p, q, rh, y0, g, bonus, gn_g, gn_b):
    B, T, _ = rh.shape
    C = RW_CHUNK
    nc = T // C
    seq = pl.BlockSpec((B, C, RW_WIDTH), lambda c: (0, c, 0))
    vec = pl.BlockSpec((1, RW_WIDTH), lambda c: (0, 0))
    return pl.pallas_call(
        _rw_scan_body,
        grid=(nc,),
        in_specs=[pl.BlockSpec((B, 1, 2, RW_WIDTH, RW_WIDTH), lambda c: (0, c, 0, 0, 0)),
                  pl.BlockSpec((B, 1, RW_WIDTH, RW_WIDTH), lambda c: (0, c, 0, 0)),
                  seq, seq, seq, seq, vec, vec],
        out_specs=seq,
        out_shape=jax.ShapeDtypeStruct((B, T, RW_WIDTH), F32),
        scratch_shapes=[pltpu.VMEM((B, RW_WIDTH, RW_WIDTH), F32)],
        compiler_params=_cparams(("arbitrary",)),
        name="rwkv_chunk_scan",
    )(p, q, rh, y0, g, bonus, gn_g.reshape(1, -1), gn_b.reshape(1, -1))


def _pool_body(z_ref, zp_ref, w_ref, scale_ref, o_ref):
    i = pl.program_id(1)
    z = z_ref[0]
    tt = z.shape[0]
    halo = jnp.where(i == 0, 0.0, zp_ref[0])
    e = jnp.concatenate([halo, z], axis=0)
    sums = []
    s = e
    for shift in (1, 2, 4, 8):
        s = s + pltpu.roll(s, shift, axis=0)
        sums.append(s[POOL_HALO:, :])
    t_idx = i * tt + lax.broadcasted_iota(jnp.int32, (tt, POOL_WIDTH), 0)
    lane_group = lax.broadcasted_iota(jnp.int32, (tt, POOL_WIDTH), 1) // HEAD
    pooled = jnp.zeros_like(z)
    for gi, win in enumerate(POOL_WINDOWS):
        cnt = jnp.minimum(t_idx + 1, win).astype(F32)
        pooled = jnp.where(lane_group == gi, sums[gi] / cnt - z, pooled)
    o_ref[0] = _dot(pooled.astype(BF16), w_ref[...]) * scale_ref[...]


def _pool(z_pool, w_blockdiag_bf16, scale, tt=512):
    B, T, _ = z_pool.shape
    return pl.pallas_call(
        _pool_body,
        grid=(B, T // tt),
        in_specs=[pl.BlockSpec((1, tt, POOL_WIDTH), lambda b, i: (b, i, 0)),
                  pl.BlockSpec((1, POOL_HALO, POOL_WIDTH),
                               lambda b, i: (b, jnp.maximum(i * (tt // POOL_HALO) - 1, 0), 0)),
                  pl.BlockSpec((POOL_WIDTH, POOL_WIDTH), lambda b, i: (0, 0)),
                  pl.BlockSpec((1, POOL_WIDTH), lambda b, i: (0, 0))],
        out_specs=pl.BlockSpec((1, tt, POOL_WIDTH), lambda b, i: (b, i, 0)),
        out_shape=jax.ShapeDtypeStruct((B, T, POOL_WIDTH), F32),
        compiler_params=_cparams(("parallel", "parallel")),
        name="pool_mixer",
    )(z_pool, z_pool, w_blockdiag_bf16, scale.reshape(1, -1))


def _rope(x, cos_w, sin_w):
    w = x.shape[1]
    lane = lax.broadcasted_iota(jnp.int32, x.shape, 1) % HEAD
    partner = jnp.where(lane < HEAD // 2, pltpu.roll(x, w - HEAD // 2, axis=1), pltpu.roll(x, HEAD // 2, axis=1))
    return x * cos_w + partner * sin_w


def _head_lanes(a, h, dtype):
    part = a[:, h * HEAD:(h + 1) * HEAD]
    return jnp.concatenate([part, jnp.zeros_like(part)], axis=1).astype(dtype)


def _nsa_prep_body(zq_ref, zkc_ref, zks_ref, zvs_ref, zkw_ref, zvw_ref, zg_ref, cos_ref, sin_ref,
                   q_ref, kc_ref, ks_ref, vst_ref, kw_ref, vwt_ref, gt_ref):
    cos = cos_ref[0]
    sin = sin_ref[0]
    cos_q = jnp.concatenate([cos] * (NSA_WIDTH // LANE), axis=1)
    sin_q = jnp.concatenate([sin] * (NSA_WIDTH // LANE), axis=1)
    qr = _rope(zq_ref[0], cos_q, sin_q) * (HEAD ** -0.5 * LOG2E)
    pad_lane = lax.broadcasted_iota(jnp.int32, (qr.shape[0], LANE), 1) == HEAD
    for h in range(NSA_Q_HEADS):
        q_ref[0, h] = jnp.where(pad_lane, NEG, _head_lanes(qr, h, F32)).astype(BF16)
    kc_ref[0] = _rope(zkc_ref[0], cos, sin)
    ks = _rope(zks_ref[0], cos, sin)
    kw = _rope(zkw_ref[0], cos, sin)
    vst = zvs_ref[0].T
    vwt = zvw_ref[0].T
    for h in range(NSA_KV_HEADS):
        sl = slice(h * HEAD, (h + 1) * HEAD)
        ks_ref[0, h] = _head_lanes(ks, h, BF16)
        kw_ref[0, h] = _head_lanes(kw, h, BF16)
        tail = lax.broadcasted_iota(jnp.int32, (V_ROWS - HEAD, vst.shape[1]), 0)
        ones_row = jnp.where(tail == 0, 1.0, 0.0).astype(BF16)
        vst_ref[0, h, 0:HEAD, :] = vst[sl, :].astype(BF16)
        vst_ref[0, h, HEAD:V_ROWS, :] = ones_row
        vwt_ref[0, h, 0:HEAD, :] = vwt[sl, :].astype(BF16)
        vwt_ref[0, h, HEAD:V_ROWS, :] = ones_row
    gt = _sigmoid(zg_ref[0]).T
    for h in range(NSA_KV_HEADS):
        gt_ref[0, h] = gt[16 * h:16 * (h + 1), :]


def _nsa_prep(z_q, z_kc, z_ks, z_vs, z_kw, z_vw, z_g, cos_t, sin_t, tt=512):
    B, T, _ = z_q.shape
    seq = lambda w: pl.BlockSpec((1, tt, w), lambda b, i: (b, i, 0))
    hm = lambda nh: pl.BlockSpec((1, nh, tt, LANE), lambda b, i: (b, 0, i, 0))
    tr = lambda rows: pl.BlockSpec((1, NSA_KV_HEADS, rows, tt), lambda b, i: (b, 0, 0, i))
    sds = jax.ShapeDtypeStruct
    return pl.pallas_call(
        _nsa_prep_body,
        grid=(B, T // tt),
        in_specs=[seq(NSA_WIDTH)] + [seq(NSA_KV)] * 6 + [seq(LANE), seq(LANE)],
        out_specs=(hm(NSA_Q_HEADS), seq(NSA_KV), hm(NSA_KV_HEADS), tr(V_ROWS), hm(NSA_KV_HEADS), tr(V_ROWS), tr(16)),
        out_shape=(sds((B, NSA_Q_HEADS, T, LANE), BF16), sds((B, T, NSA_KV), F32),
                   sds((B, NSA_KV_HEADS, T, LANE), BF16), sds((B, NSA_KV_HEADS, V_ROWS, T), BF16),
                   sds((B, NSA_KV_HEADS, T, LANE), BF16), sds((B, NSA_KV_HEADS, V_ROWS, T), BF16),
                   sds((B, NSA_KV_HEADS, 16, T), F32)),
        compiler_params=_cparams(("parallel", "parallel")),
        name="nsa_prep",
    )(z_q, z_kc, z_ks, z_vs, z_kw, z_vw, z_g, cos_t, sin_t)


def _nsa_cmp_body(gk_ref, gv_ref, pek_ref, pev_ref, wk_ref, wv_ref, kc_ref, vct_ref):
    half = NSA_KV

    def compress(g, pe_ref, w_ref):
        n = g.shape[0]
        lo = _dot((g + pe_ref[0:1, :]).astype(BF16), w_ref[:, 0:half])
        hi = _dot((g + pe_ref[1:2, :]).astype(BF16), w_ref[:, half:2 * half])
        hi_next = pltpu.roll(hi, n - 1, axis=0)
        row = lax.broadcasted_iota(jnp.int32, lo.shape, 0)
        return lo + jnp.where(row == n - 1, 0.0, hi_next)

    kc = compress(gk_ref[0], pek_ref, wk_ref)
    vct = compress(gv_ref[0], pev_ref, wv_ref).T
    for h in range(NSA_KV_HEADS):
        sl = slice(h * HEAD, (h + 1) * HEAD)
        kc_ref[0, h] = _head_lanes(kc, h, BF16)
        vct_ref[0, h] = vct[sl, :].astype(BF16)


def _nsa_compress(kc_groups, vc_groups, pe_k2, pe_v2, wk2, wv2):
    B, ncp, gw = kc_groups.shape
    grp = pl.BlockSpec((1, ncp, gw), lambda b: (b, 0, 0))
    pe = pl.BlockSpec((2, gw), lambda b: (0, 0))
    wspec = pl.BlockSpec((gw, 2 * NSA_KV), lambda b: (0, 0))
    sds = jax.ShapeDtypeStruct
    return pl.pallas_call(
        _nsa_cmp_body,
        grid=(B,),
        in_specs=[grp, grp, pe, pe, wspec, wspec],
        out_specs=(pl.BlockSpec((1, NSA_KV_HEADS, ncp, LANE), lambda b: (b, 0, 0, 0)),
                   pl.BlockSpec((1, NSA_KV_HEADS, HEAD, ncp), lambda b: (b, 0, 0, 0))),
        out_shape=(sds((B, NSA_KV_HEADS, ncp, LANE), BF16), sds((B, NSA_KV_HEADS, HEAD, ncp), BF16)),
        compiler_params=_cparams(("parallel",)),
        name="nsa_compress",
    )(kc_groups, vc_groups, pe_k2, pe_v2, wk2, wv2)


def _nsa_attn_body(q_ref, kc_ref, vct_ref, ks_ref, vst_ref, kw_ref, vwt_ref, gt_ref, mcs_ref, blk_ref, cmpb_ref,
                   winb_ref, o_ref, sa_ref, sb_ref, sc_ref, sd_ref):
    jb = pl.program_id(2)
    s0 = jb * Q_BLOCK
    cols = NSA_GQA * Q_BLOCK
    q = q_ref[0].reshape(cols, LANE)
    ncp = kc_ref.shape[2]

    cmp_mask = cmpb_ref[pl.ds(pl.multiple_of(ncp - s0 // CMP_STRIDE, 8), ncp), :]
    sc = _dot_nt(kc_ref[0, 0], q) + jnp.concatenate([cmp_mask] * NSA_GQA, axis=1)
    e = jnp.exp2(sc - jnp.max(sc, axis=0, keepdims=True))
    t_c = s0 + lax.broadcasted_iota(jnp.int32, (1, cols), 1) % Q_BLOCK
    any_valid = jnp.where(t_c >= CMP_BLOCK - 1, 1.0, 0.0)
    pc = e * (any_valid / jnp.sum(e, axis=0, keepdims=True))
    o_cmp = _dot(vct_ref[0, 0], pc.astype(BF16))
    psum = pc[:, 0:Q_BLOCK]
    for g in range(1, NSA_GQA):
        psum = psum + pc[:, g * Q_BLOCK:(g + 1) * Q_BLOCK]
    p_hi = psum.astype(BF16)
    rest = psum - p_hi.astype(F32)
    p_mid = rest.astype(BF16)
    p_lo = (rest - p_mid.astype(F32)).astype(BF16)
    mcs = mcs_ref[...]
    imp = _dot(mcs, p_hi) + (_dot(mcs, p_mid) + _dot(mcs, p_lo))

    wkeys = WINDOW + Q_BLOCK
    off_w = pl.multiple_of(s0, Q_BLOCK)
    sw = _dot_nt(kw_ref[0, 0, pl.ds(off_w, wkeys), :], q)
    tri = lambda i: jnp.concatenate([winb_ref[i]] * NSA_GQA, axis=1)
    sw = jnp.concatenate([sw[0:Q_BLOCK] + tri(0), sw[Q_BLOCK:WINDOW], sw[WINDOW:wkeys] + tri(1)], axis=0)
    pw = jnp.exp2((sw - jnp.max(sw, axis=0, keepdims=True)).astype(BF16))
    acc_w = _dot(vwt_ref[0, 0, :, pl.ds(off_w, wkeys)], pw)

    j_idx = lax.broadcasted_iota(jnp.int32, (NSEL_PAD, Q_BLOCK), 0)
    jc = (s0 + lax.broadcasted_iota(jnp.int32, (NSEL_PAD, Q_BLOCK), 1)) // SEL_BLOCK
    causal = j_idx <= jc
    forced_imp = jnp.where(j_idx == 0, FORCE, jnp.where(j_idx == jc, FORCE, jnp.where(j_idx == jc - 1, FORCE, imp)))
    work = jnp.where(causal, forced_imp, NEG)
    j_f = j_idx.astype(F32)
    for _ in range(SEL_TOPN):
        top = jnp.max(work, axis=0, keepdims=True)
        first = jnp.min(jnp.where(work == top, j_f, float(NSEL_PAD)), axis=0, keepdims=True)
        work = jnp.where(j_f == first, -jnp.inf, work)
    bias = jnp.where(causal, jnp.where(work == -jnp.inf, 0.0, NEG), NEG)

    bias_q = bias.T.astype(BF16)
    q_aug = jnp.concatenate([q, jnp.concatenate([bias_q] * NSA_GQA, axis=0)], axis=1)

    def sel_scores(off):
        k_aug = jnp.concatenate([ks_ref[0, 0, pl.ds(off, KEY_GROUP), :], blk_ref[pl.ds(off, KEY_GROUP), :]], axis=1)
        return _dot_nt(k_aug, q_aug)

    def flash_update(v_t, st, carry):
        m, acc = carry
        m_new = jnp.maximum(m, jnp.max(st, axis=0, keepdims=True))
        p = jnp.exp2((st - m_new).astype(BF16))
        return m_new, acc * jnp.exp2(m - m_new) + _dot(v_t, p)

    init =(jnp.full((1, cols), NEG, F32), jnp.zeros((V_ROWS, cols), F32))
    n_groups = s0 // KEY_GROUP + 1
    n_pairs = n_groups // 2
    t_q = s0 + lax.broadcasted_iota(jnp.int32, (Q_BLOCK, cols), 1) % Q_BLOCK
    r_q = lax.broadcasted_iota(jnp.int32, (Q_BLOCK, cols), 0)

    def put_scores(ref, g):
        off = pl.multiple_of(g * KEY_GROUP, KEY_GROUP)
        ref[...] = sel_scores(off)
        r = pl.multiple_of(jnp.clip(s0 - off, 0, KEY_GROUP - Q_BLOCK), Q_BLOCK)
        tile = ref[pl.ds(r, Q_BLOCK), :]
        ref[pl.ds(r, Q_BLOCK), :] = jnp.where(off + r + r_q <= t_q, tile, NEG)

    def consume(ref, g, carry):
        off = pl.multiple_of(g * KEY_GROUP, KEY_GROUP)
        return flash_update(vst_ref[0, 0, :, pl.ds(off, KEY_GROUP)], ref[...], carry)

    last = n_groups - 1
    buf0, buf1 = (sa_ref, sb_ref), (sc_ref, sd_ref)
    put_scores(sa_ref, 0)
    put_scores(sb_ref, jnp.minimum(1, last))

    def stage(cur, nxt, pair, carries):
        put_scores(nxt[0], jnp.minimum(2 * pair + 2, last))
        c_even = consume(cur[0], 2 * pair, carries[0])
        put_scores(nxt[1], jnp.minimum(2 * pair + 3, last))
        return c_even, consume(cur[1], 2 * pair + 1, carries[1])

    def two_pairs(i, carries):
        return stage(buf1, buf0, 2 * i + 1, stage(buf0, buf1, 2 * i, carries))

    carries = lax.fori_loop(0, n_pairs // 2, two_pairs, (init, init))
    odd_pairs = n_pairs % 2
    c_even, c_odd = lax.fori_loop(n_pairs - odd_pairs, n_pairs, lambda pr, c: stage(buf0, buf1, pr, c), carries)
    left = n_groups - 2 * n_pairs
    c_even = lax.fori_loop(0, left * (1 - odd_pairs), lambda _, c: consume(sa_ref, last, c), c_even)
    c_even = lax.fori_loop(0, left * odd_pairs, lambda _, c: consume(sc_ref, last, c), c_even)
    m_e, acc_e = c_even
    m_o, acc_o = c_odd
    m_s = jnp.maximum(m_e, m_o)
    acc_s = acc_e * jnp.exp2(m_e - m_s) + acc_o * jnp.exp2(m_o - m_s)

    def gate_row(branch):
        return jnp.concatenate([gt_ref[0, 0, 3 * g + branch:3 * g + branch + 1, :] for g in range(NSA_GQA)], axis=1)

    o_t = (gate_row(0) * o_cmp + (gate_row(1) / acc_s[HEAD:HEAD + 1]) * acc_s[0:HEAD]
           + (gate_row(2) / acc_w[HEAD:HEAD + 1]) * acc_w[0:HEAD])
    stacked = jnp.concatenate([o_t[:, g * Q_BLOCK:(g + 1) * Q_BLOCK] for g in range(NSA_GQA)], axis=0)
    o_ref[0] = stacked.T


def _nsa_masks(seq_len):
    tq = np.arange(Q_BLOCK)[None, :]
    ncp = seq_len // CMP_STRIDE
    n_rel = np.arange(2 * ncp)[:, None] - ncp
    cmp_b = np.where(CMP_STRIDE * n_rel + CMP_BLOCK - 1 <= tq, 0.0, NEG).astype(np.float32)
    k_rel = np.arange(Q_BLOCK)[:, None]
    win_b = np.stack([np.where(k_rel > tq, 0.0, NEG), np.where(k_rel <= tq, 0.0, NEG)]).astype(np.float32)
    return jnp.asarray(cmp_b), jnp.asarray(win_b)


def _nsa_attention(q_r, kcmp, vcmp_t, ks_r, vs_t, kw_pad, vw_pad_t, g_t, mcs_t, blk_onehot, cmp_b, win_b):
    B, _, T, _ = q_r.shape
    ncp = kcmp.shape[2]
    nqb = T // Q_BLOCK
    tw = T + WINDOW
    assert T % KEY_GROUP == 0 and T // SEL_BLOCK <= NSEL_PAD and WINDOW > Q_BLOCK and WINDOW % Q_BLOCK == 0
    kv_rows = lambda n: pl.BlockSpec((1, 1, n, LANE), lambda b, h, j: (b, h, 0, 0))
    kv_cols = lambda n, rows: pl.BlockSpec((1, 1, rows, n), lambda b, h, j: (b, h, 0, 0))
    const = lambda shp: pl.BlockSpec(shp, lambda b, h, j: (0,) * len(shp))
    return pl.pallas_call(
        _nsa_attn_body,
        grid=(B, NSA_KV_HEADS, nqb),
        in_specs=[pl.BlockSpec((1, NSA_GQA, Q_BLOCK, LANE), lambda b, h, j: (b, h, j, 0)),
                  kv_rows(ncp), kv_cols(ncp, HEAD), kv_rows(T), kv_cols(T, V_ROWS), kv_rows(tw), kv_cols(tw, V_ROWS),
                  pl.BlockSpec((1, 1, 16, Q_BLOCK), lambda b, h, j: (b, h, 0, j)),
                  const((NSEL_PAD, ncp)), const((T, NSEL_PAD)), const((2 * ncp, Q_BLOCK)),
                  const((2, Q_BLOCK, Q_BLOCK))],
        out_specs=pl.BlockSpec((1, Q_BLOCK, NSA_GQA * HEAD), lambda b, h, j: (b, j, h)),
        out_shape=jax.ShapeDtypeStruct((B, T, NSA_WIDTH), F32),
        scratch_shapes=[pltpu.VMEM((KEY_GROUP, NSA_GQA * Q_BLOCK), F32)] * 4,
        compiler_params=_cparams(("parallel", "parallel", "arbitrary")),
        name="nsa_attention",
    )(q_r, kcmp, vcmp_t, ks_r, vs_t, kw_pad, vw_pad_t, g_t, mcs_t, blk_onehot, cmp_b, win_b)


def _out_proj_body(x_ref, ya_ref, yb_ref, yc_ref, wa_ref, wb_ref, wc_ref, o_ref):
    acc = _dot(ya_ref[...].astype(BF16), wa_ref[...])
    acc = acc + _dot(yb_ref[...].astype(BF16), wb_ref[...])
    acc = acc + _dot(yc_ref[...].astype(BF16), wc_ref[...])
    o_ref[...] = x_ref[...] + acc


def _out_proj(x2, ya, yb, yc, w_out_bf16, tm=512):
    m, d = x2.shape
    wa, wb, wc = w_out_bf16[:RW_WIDTH], w_out_bf16[RW_WIDTH:RW_WIDTH + POOL_WIDTH], w_out_bf16[RW_WIDTH + POOL_WIDTH:]
    tile = lambda w: pl.BlockSpec((tm, w), lambda i: (i, 0))
    wfull = lambda w: pl.BlockSpec((w, d), lambda i: (0, 0))
    return pl.pallas_call(
        _out_proj_body,
        grid=(m // tm,),
        in_specs=[tile(d), tile(RW_WIDTH), tile(POOL_WIDTH), tile(NSA_WIDTH),
                  wfull(RW_WIDTH), wfull(POOL_WIDTH), wfull(NSA_WIDTH)],
        out_specs=tile(d),
        out_shape=jax.ShapeDtypeStruct((m, d), F32),
        compiler_params=_cparams(("parallel",)),
        name="out_proj_residual",
    )(x2, ya, yb, yc, wa, wb, wc)


def _gelu_tanh(x):
    return 0.5 * x * (1.0 + jnp.tanh(math.sqrt(2.0 / math.pi) * (x + 0.044715 * (x * x * x))))


def _ffn_body(x_ref, g_ref, wup_ref, cw_ref, cb_ref, wdn_ref, o_ref, carry_ref, *, tiles_per_seq, f_chunk):
    i = pl.program_id(0)
    x = x_ref[...]
    tm = x.shape[0]
    d_ff = wdn_ref.shape[0]
    h = _rms(x, g_ref[...]).astype(BF16)
    first = (i % tiles_per_seq) == 0
    row = lax.broadcasted_iota(jnp.int32, (tm, f_chunk), 0)

    def conv(col0):
        u = _dot(h, wup_ref[:, col0:col0 + f_chunk])
        p2 = jnp.where(first, 0.0, carry_ref[6:7, col0:col0 + f_chunk])
        p1 = jnp.where(first, 0.0, carry_ref[7:8, col0:col0 + f_chunk])
        u1 = jnp.where(row == 0, p1, pltpu.roll(u, 1, axis=0))
        u2 = jnp.where(row == 0, p2, jnp.where(row == 1, p1, pltpu.roll(u, 2, axis=0)))
        carry_ref[:, col0:col0 + f_chunk] = u[tm - 8:tm, :]
        cw = cw_ref[:, col0:col0 + f_chunk]
        return cw[0:1] * u2 + cw[1:2] * u1 + cw[2:3] * u + cb_ref[:, col0:col0 + f_chunk]

    acc = x
    for c0 in range(0, d_ff, f_chunk):
        act = _gelu_tanh(conv(c0)) * conv(d_ff + c0)
        acc = acc + _dot(act.astype(BF16), wdn_ref[c0:c0 + f_chunk, :])
    o_ref[...] = acc


def _conv_ffn(x2, g, w_up_bf16, conv_w, conv_b, w_down_bf16, seq_len, tm=512, f_chunk=1408):
    m, d = x2.shape
    f2 = w_up_bf16.shape[1]
    d_ff = f2 // 2
    assert d_ff % f_chunk == 0 and f_chunk % LANE == 0 and seq_len % tm == 0
    full = lambda shp: pl.BlockSpec(shp, lambda i: (0,) * len(shp))
    return pl.pallas_call(
        functools.partial(_ffn_body, tiles_per_seq=seq_len // tm, f_chunk=f_chunk),
        grid=(m // tm,),
        in_specs=[pl.BlockSpec((tm, d), lambda i: (i, 0)), full((1, d)), full((d, f2)),
                  full((CONV_W, f2)), full((1, f2)), full((d_ff, d))],
        out_specs=pl.BlockSpec((tm, d), lambda i: (i, 0)),
        out_shape=jax.ShapeDtypeStruct((m, d), F32),
        scratch_shapes=[pltpu.VMEM((8, f2), F32)],
        compiler_params=_cparams(("arbitrary",)),
        name="conv_ffn",
    )(x2, g.reshape(1, d), w_up_bf16, conv_w.reshape(CONV_W, f2), conv_b.reshape(1, f2), w_down_bf16)


def _ple_body(x_ref, p_ref, g_ref, wg_ref, wp_ref, gf_ref, o_ref, *, final):
    x = x_ref[...]
    gate = _sigmoid(_dot(_rms(x, g_ref[...]).astype(BF16), wg_ref[...]))
    y = x + _dot(p_ref[...].astype(BF16), wp_ref[...]) * gate
    o_ref[...] = _rms(y, gf_ref[...]) if final else y


def _ple(x2, p2, g, w_gate_bf16, w_proj_bf16, g_final, final, tm=512):
    m, d = x2.shape
    pd = p2.shape[1]
    full = lambda shp: pl.BlockSpec(shp, lambda i: (0,) * len(shp))
    return pl.pallas_call(
        functools.partial(_ple_body, final=final),
        grid=(m // tm,),
        in_specs=[pl.BlockSpec((tm, d), lambda i: (i, 0)), pl.BlockSpec((tm, pd), lambda i: (i, 0)),
                  full((1, d)), full((d, d)), full((pd, d)), full((1, d))],
        out_specs=pl.BlockSpec((tm, d), lambda i: (i, 0)),
        out_shape=jax.ShapeDtypeStruct((m, d), F32),
        compiler_params=_cparams(("parallel",)),
        name="ple_final_norm" if final else "ple",
    )(x2, p2, g.reshape(1, d), w_gate_bf16, w_proj_bf16, g_final.reshape(1, d))


def _overlap_matrix(seq_len):
    ncp = seq_len // CMP_STRIDE
    n_cmp = (seq_len - CMP_BLOCK) // CMP_STRIDE + 1
    cs = CMP_STRIDE * np.arange(n_cmp)
    ss = SEL_BLOCK * np.arange(seq_len // SEL_BLOCK)
    ov = (np.minimum(cs[:, None] + CMP_BLOCK - 1, ss[None] + SEL_BLOCK - 1) - np.maximum(cs[:, None], ss[None]) + 1)
    m = np.zeros((NSEL_PAD, ncp), np.float32)
    m[:ss.size, :n_cmp] = (np.clip(ov, 0, CMP_BLOCK).astype(np.float32) / CMP_BLOCK).T
    return jnp.asarray(m, dtype=BF16)


def _compress_weight(w):
    halves = w.reshape(2, CMP_STRIDE, HEAD, HEAD)
    eye = jnp.eye(NSA_KV_HEADS, dtype=w.dtype)
    w2 = jnp.einsum('sldf,hg->lhdsgf', halves, eye)
    return w2.reshape(CMP_STRIDE * NSA_KV, 2 * NSA_KV).astype(BF16)


def _compress_pe(pe):
    halves = pe.reshape(2, CMP_STRIDE, 1, HEAD)
    return jnp.broadcast_to(halves, (2, CMP_STRIDE, NSA_KV_HEADS, HEAD)).reshape(2, CMP_STRIDE * NSA_KV)


def _in_proj_layout(w_in):
    d = w_in.shape[0]
    gate0 = RW_IN + POOL_WIDTH + NSA_WIDTH + 6 * NSA_KV
    per_head = NSA_GQA * 3
    gates = jnp.zeros((d, LANE), w_in.dtype)
    for h in range(NSA_KV_HEADS):
        gates = gates.at[:, 16 * h:16 * h + per_head].set(w_in[:, gate0 + per_head * h:gate0 + per_head * (h + 1)])
    return jnp.concatenate([w_in[:, :gate0], gates], axis=1).astype(BF16)


_IN_SPLITS = ((0, RW_IN), (RW_IN, RW_IN + POOL_WIDTH), (RW_IN + POOL_WIDTH, RW_IN + POOL_WIDTH + NSA_WIDTH)) + tuple(
    (RW_IN + POOL_WIDTH + NSA_WIDTH + i * NSA_KV, RW_IN + POOL_WIDTH + NSA_WIDTH + (i + 1) * NSA_KV) for i in range(7))


def kernel(x, p, positions, g_mix, w_in, rw_mu, rw_w0, rw_w_up, rw_a0, rw_a_up, rw_g_up, rw_k_k, rw_k_a, rw_r_k, rw_gn_g, rw_gn_b, pool_w, pool_scale, nsa_pe_k, nsa_pe_v, nsa_w_ck, nsa_w_cv, w_out, g_ffn, ffn_w_up, ffn_conv_w, ffn_conv_b, ffn_w_down, g_ple, ple_w_gate, ple_w_proj, g_final):
    B, T, D = x.shape
    depth = w_in.shape[0]
    M = B * T

    half = HEAD // 2
    inv = ROPE_THETA ** (-jnp.arange(half, dtype=F32) / half)
    ang = positions.astype(F32)[..., None] * inv
    cos, sin = jnp.cos(ang), jnp.sin(ang)
    cos_t = jnp.concatenate([cos, cos] * (LANE // HEAD), axis=-1)
    sin_t = jnp.concatenate([-sin, sin] * (LANE // HEAD), axis=-1)
    mcs_t = _overlap_matrix(T)
    blk_onehot = (jnp.arange(T)[:, None] // SEL_BLOCK == jnp.arange(NSEL_PAD)[None, :]).astype(BF16)
    cmp_b, win_b = _nsa_masks(T)
    key_pad = (jnp.arange(LANE) == HEAD).astype(BF16)

    x2 = x.reshape(M, D)
    for i in range(depth):
        zs = _norm_matmul(x2, g_mix[i], _in_proj_layout(w_in[i]), _IN_SPLITS)
        z_rw, z_pool, z_q, z_kc, z_vc, z_ks, z_vs, z_kw, z_vw, z_g = (
            z.reshape(B, T, z.shape[1]) for z in zs)

        pm, qm, rh, y0, gate, bonus = _rw_chunks(z_rw, rw_mu[i], rw_w0[i], rw_w_up[i], rw_a0[i], rw_a_up[i],
                                                 rw_g_up[i], rw_k_k[i], rw_k_a[i], rw_r_k[i].reshape(-1))
        y_a = _rw_scan(pm, qm, rh, y0, gate, bonus, rw_gn_g[i], rw_gn_b[i])

        w_pool_bd = jax.scipy.linalg.block_diag(*[pool_w[i, gi] for gi in range(pool_w.shape[1])]).astype(BF16)
        y_b = _pool(z_pool, w_pool_bd, pool_scale[i])

        q_r, kc_r, ks_r, vs_t, kw_r, vw_t, g_t = _nsa_prep(z_q, z_kc, z_ks, z_vs, z_kw, z_vw, z_g, cos_t, sin_t)
        grp = (B, T // CMP_STRIDE, CMP_STRIDE * NSA_KV)
        kcmp, vcmp_t = _nsa_compress(kc_r.reshape(grp), z_vc.reshape(grp),
                                     _compress_pe(nsa_pe_k[i]), _compress_pe(nsa_pe_v[i]),
                                     _compress_weight(nsa_w_ck[i]), _compress_weight(nsa_w_cv[i]))
        kw_pad = jnp.concatenate([jnp.broadcast_to(key_pad, (B, NSA_KV_HEADS, WINDOW, LANE)), kw_r], axis=2)
        vw_pad_t = jnp.pad(vw_t, ((0, 0), (0, 0), (0, 0), (WINDOW, 0)))
        y_c = _nsa_attention(q_r, kcmp, vcmp_t, ks_r, vs_t, kw_pad, vw_pad_t, g_t, mcs_t, blk_onehot, cmp_b, win_b)

        x2 = _out_proj(x2, y_a.reshape(M, -1), y_b.reshape(M, -1), y_c.reshape(M, -1), w_out[i].astype(BF16))
        x2 = _conv_ffn(x2, g_ffn[i], ffn_w_up[i].astype(BF16), ffn_conv_w[i], ffn_conv_b[i],
                       ffn_w_down[i].astype(BF16), T)
        x2 = _ple(x2, p[i].reshape(M, -1), g_ple[i], ple_w_gate[i].astype(BF16), ple_w_proj[i].astype(BF16),
                  g_final, final=(i == depth - 1))
    return x2.reshape(B, T, D)
```

```python
import functools
import math

import numpy as np
import jax
import jax.numpy as jnp
from jax import lax
from jax.experimental import pallas as pl
from jax.experimental.pallas import tpu as pltpu

F32 = jnp.float32
BF16 = jnp.bfloat16

RMS_EPS = 1e-6
ROPE_THETA = 10000.0
HEAD = 64
RW_HEADS = 4
RW_WIDTH = RW_HEADS * HEAD
RW_DECAY_LORA = 64
RW_AAA_LORA = 64
RW_GATE_LORA = 128
RW_GN_EPS = 64e-5
RW_IN = 3 * RW_WIDTH + RW_DECAY_LORA + RW_AAA_LORA + RW_GATE_LORA
POOL_WIDTH = 256
POOL_WINDOWS = (2, 4, 8, 16)
POOL_HALO = 16
NSA_Q_HEADS = 8
NSA_KV_HEADS = 2
NSA_GQA = NSA_Q_HEADS // NSA_KV_HEADS
NSA_WIDTH = NSA_Q_HEADS * HEAD
NSA_KV = NSA_KV_HEADS * HEAD
CMP_BLOCK = 32
CMP_STRIDE = 16
SEL_BLOCK = 64
SEL_TOPN = 16
WINDOW = 512
Q_BLOCK = 128
KEY_GROUP = 512
V_ROWS = HEAD + 16
NSEL_PAD = 128
NEG = -1e30
LOG2E = math.log2(math.e)
FORCE = 1e9
CONV_W = 3
LANE = 128
VMEM_LIMIT = 56 * 1024 * 1024

RW_CHUNK = 64


def _cparams(sem):
    return pltpu.CompilerParams(dimension_semantics=sem, vmem_limit_bytes=VMEM_LIMIT)


def _dot(a, b, prec=None):
    return lax.dot_general(a, b, (((1,), (0,)), ((), ())), precision=prec, preferred_element_type=F32)


def _dot_nt(a, b, prec=None):
    return lax.dot_general(a, b, (((1,), (1,)), ((), ())), precision=prec, preferred_element_type=F32)


def _sigmoid(x):
    return 1.0 / (1.0 + jnp.exp(-x))


def _rms(x, g):
    ms = jnp.mean(x * x, axis=-1, keepdims=True)
    return (x * lax.rsqrt(ms + RMS_EPS)) * g


def _norm_mm_body(x_ref, g_ref, w_ref, *o_refs, splits):
    h = _rms(x_ref[...], g_ref[...]).astype(BF16)
    for o_ref, (lo, hi) in zip(o_refs, splits):
        o_ref[...] = _dot(h, w_ref[:, lo:hi])


def _norm_matmul(x2, g, w_bf16, splits, tm=512):
    m, d = x2.shape
    n = w_bf16.shape[1]
    outs = tuple(jax.ShapeDtypeStruct((m, hi - lo), F32) for lo, hi in splits)
    return pl.pallas_call(
        functools.partial(_norm_mm_body, splits=splits),
        grid=(m // tm,),
        in_specs=[pl.BlockSpec((tm, d), lambda i: (i, 0)),
                  pl.BlockSpec((1, d), lambda i: (0, 0)),
                  pl.BlockSpec((d, n), lambda i: (0, 0))],
        out_specs=tuple(pl.BlockSpec((tm, hi - lo), lambda i: (i, 0)) for lo, hi in splits),
        out_shape=outs,
        compiler_params=_cparams(("parallel",)),
        name="norm_in_proj",
    )(x2, g.reshape(1, d), w_bf16)


def _split(x):
    hi = x.astype(BF16)
    return hi, (x - hi.astype(F32)).astype(BF16)


def _dot3(a, b, dot=_dot):
    return dot(a[0], b[0]) + (dot(a[0], b[1]) + dot(a[1], b[0]))


def _dot_ones(x, ones_bf16):
    hi, lo = _split(x)
    return _dot(hi, ones_bf16) + _dot(lo, ones_bf16)


def _head_ones():
    hr = lax.broadcasted_iota(jnp.int32, (RW_WIDTH, RW_WIDTH), 0) // HEAD
    hc = lax.broadcasted_iota(jnp.int32, (RW_WIDTH, RW_WIDTH), 1) // HEAD
    return hr == hc


def _rw_chunk_body(z_ref, zp_ref, mu_ref, w0_ref, wup_ref, a0_ref, aup_ref, gup_ref, kk_ref, ka_ref,
                   rk_ref, p_ref, q_ref, rh_ref, y0_ref, g_ref, bonus_ref):
    i = pl.program_id(1)
    C = RW_CHUNK
    W = RW_WIDTH
    z = z_ref[0]
    rows = z.shape[0]
    n_chunks = rows // C
    prev = jnp.where(i == 0, 0.0, zp_ref[0, 7:8, :])
    row_z = lax.broadcasted_iota(jnp.int32, z.shape, 0)
    zs = jnp.where(row_z == 0, prev, pltpu.roll(z, 1, axis=0))
    zf = z + (zs - z) * mu_ref[...]
    r = zf[:, 0:W]
    k = zf[:, W:2 * W]
    v = zf[:, 2 * W:3 * W]
    zw = zf[:, 3 * W:3 * W + RW_DECAY_LORA]
    za = zf[:, 3 * W + RW_DECAY_LORA:3 * W + RW_DECAY_LORA + RW_AAA_LORA]
    zg = zf[:, 3 * W + RW_DECAY_LORA + RW_AAA_LORA:]

    lora = lambda x, w_ref: _dot3(_split(x), (w_ref[0], w_ref[1]))
    xw = -(w0_ref[...] + lora(jnp.tanh(zw), wup_ref))
    softplus = jnp.maximum(xw, 0.0) + jnp.log(1.0 + jnp.exp(-jnp.abs(xw)))
    ld = -jnp.exp(-softplus - 0.5)
    a = _sigmoid(a0_ref[...] + lora(za, aup_ref))
    g_ref[0] = lora(_sigmoid(zg), gup_ref)
    kk = k * kk_ref[...]
    k2 = k * (1.0 + (a - 1.0) * ka_ref[...])

    same_head = _head_ones()
    head_ones = jnp.where(same_head, 1.0, 0.0).astype(BF16)
    kkn = kk / jnp.maximum(jnp.sqrt(_dot_ones(kk * kk, head_ones)), 1e-12)
    bonus_ref[0] = _dot_ones(r * k2 * rk_ref[...], head_ones) * v

    t_in = lax.broadcasted_iota(jnp.int32, (rows, W), 0) % C
    lc = ld
    shift = 1
    while shift < C:
        lc = lc + jnp.where(t_in >= shift, pltpu.roll(lc, shift, axis=0), 0.0)
        shift *= 2

    tr = lax.broadcasted_iota(jnp.int32, (W, W), 0) % C
    tc = lax.broadcasted_iota(jnp.int32, (W, W), 1) % C
    lower_incl = tr >= tc
    lower_strict = tr > tc
    eye = jnp.where(same_head, jnp.where(tr == tc, 1.0, 0.0), 0.0)

    def embed(x):
        return jnp.where(same_head, jnp.concatenate([x] * RW_HEADS, axis=0), 0.0)

    bf = lambda x: x.astype(BF16)
    A, Bt, Kt, Rt, V, BhT, KhT, gdiag, Rt_raw = [], [], [], [], [], [], [], [], []
    for c in range(n_chunks):
        rs = slice(c * C, (c + 1) * C)
        lc_c, ld_c = lc[rs], ld[rs]
        lc_end = lc_c[C - 1:C, :]
        ginv = jnp.exp(-lc_c)
        dec_end = jnp.exp(lc_end - lc_c)
        kkn_c, a_c, k2_c = kkn[rs], a[rs], k2[rs]
        r_t = r[rs] * jnp.exp(lc_c)
        A.append(bf(embed(-kkn_c * jnp.exp(lc_c - ld_c))))
        Bt.append(bf(embed(kkn_c * a_c * ginv)))
        Kt.append(bf(embed(k2_c * ginv)))
        Rt.append(bf(embed(r_t)))
        Rt_raw.append(r_t)
        V.append(bf(embed(v[rs])))
        BhT.append(bf(embed(kkn_c * a_c * dec_end).T))
        KhT.append(bf(embed(k2_c * dec_end).T))
        gdiag.append(jnp.where(eye > 0.5, jnp.broadcast_to(jnp.exp(lc_end), (W, W)), 0.0))

    each = lambda f, *lists: [f(*xs) for xs in zip(*lists)]
    L = each(lambda x, y: jnp.where(lower_strict, _dot_nt(x, y), 0.0), A, Bt)
    Lak = each(lambda x, y: bf(jnp.where(lower_strict, _dot_nt(x, y), 0.0)), A, Kt)
    Mrb = each(lambda x, y: bf(jnp.where(lower_incl, _dot_nt(x, y), 0.0)), Rt, Bt)
    Mrk = each(lambda x, y: bf(jnp.where(lower_incl, _dot_nt(x, y), 0.0)), Rt, Kt)
    Tm = [eye + x for x in L]
    Pw = each(bf, L)
    for _ in range(int(math.log2(C)) - 1):
        Pw = each(lambda x: bf(_dot(x, x)), Pw)
        Tm = each(lambda t, x: t + _dot(bf(t), x), Tm, Pw)
    Tm = each(bf, Tm)
    Wm = each(lambda t, x: bf(_dot(t, x)), Tm, A)
    LakV = each(lambda x, y: bf(_dot(x, y)), Lak, V)
    U0 = each(lambda t, x: bf(_dot(t, x)), Tm, LakV)
    Pm = each(lambda d, x, y: d + _dot(x, y), gdiag, BhT, Wm)
    Qm = each(lambda x, y, s, t: _dot(x, y) + _dot(s, t), BhT, U0, KhT, V)
    Rh = each(_dot, Mrb, Wm)
    Y0 = each(lambda x, y, s, t: _dot(x, y) + _dot(s, t), Mrb, U0, Mrk, V)

    def flatten(x_bd):
        out = x_bd[0:C]
        for h in range(1, RW_HEADS):
            out = out + x_bd[h * C:(h + 1) * C]
        return out

    for c in range(n_chunks):
        rs = slice(c * C, (c + 1) * C)
        p_hi, p_lo = _split(Pm[c])
        p_ref[0, c, 0] = p_hi
        p_ref[0, c, 1] = p_lo
        q_ref[0, c] = Qm[c]
        rh_ref[0, rs, :] = Rt_raw[c] + flatten(Rh[c])
        y0_ref[0, rs, :] = flatten(Y0[c])


def _rw_chunks(z_rw, mu, w0, w_up, a0, a_up, g_up, k_k, k_a, r_k, chunks_per_step=8):
    B, T, _ = z_rw.shape
    C = RW_CHUNK
    assert C == HEAD, "the chunk build shares one index grid between time and head-dim masks"
    rows = C * chunks_per_step
    nc = T // C
    row = lambda a: a.reshape(1, -1)
    pair = lambda w: jnp.stack(_split(w))
    full = lambda shp: pl.BlockSpec(shp, lambda b, i: (0,) * len(shp))
    seq = lambda w: pl.BlockSpec((1, rows, w), lambda b, i: (b, i, 0))
    sds = jax.ShapeDtypeStruct
    seq_shape = sds((B, T, RW_WIDTH), F32)
    return pl.pallas_call(
        _rw_chunk_body,
        grid=(B, T // rows),
        in_specs=[seq(RW_IN),
                  pl.BlockSpec((1, 8, RW_IN), lambda b, i: (b, jnp.maximum(i * (rows // 8) - 1, 0), 0)),
                  full((1, RW_IN)), full((1, RW_WIDTH)), full((2, RW_DECAY_LORA, RW_WIDTH)),
                  full((1, RW_WIDTH)), full((2, RW_AAA_LORA, RW_WIDTH)), full((2, RW_GATE_LORA, RW_WIDTH)),
                  full((1, RW_WIDTH)), full((1, RW_WIDTH)), full((1, RW_WIDTH))],
        out_specs=(pl.BlockSpec((1, chunks_per_step, 2, RW_WIDTH, RW_WIDTH), lambda b, i: (b, i, 0, 0, 0)),
                   pl.BlockSpec((1, chunks_per_step, RW_WIDTH, RW_WIDTH), lambda b, i: (b, i, 0, 0)),
                   seq(RW_WIDTH), seq(RW_WIDTH), seq(RW_WIDTH), seq(RW_WIDTH)),
        out_shape=(sds((B, nc, 2, RW_WIDTH, RW_WIDTH), BF16), sds((B, nc, RW_WIDTH, RW_WIDTH), F32),
                   seq_shape, seq_shape, seq_shape, seq_shape),
        compiler_params=_cparams(("parallel", "parallel")),
        name="rwkv_chunk_build",
    )(z_rw, z_rw, row(mu), row(w0), pair(w_up), row(a0), pair(a_up), pair(g_up), row(k_k), row(k_a), row(r_k))


def _rw_scan_body(p_ref, q_ref, rh_ref, y0_ref, g_ref, bonus_ref, gng_ref, gnb_ref, o_ref, h_ref):
    @pl.when(pl.program_id(0) == 0)
    def _():
        h_ref[...] = jnp.zeros_like(h_ref)

    nb = h_ref.shape[0]
    head_ones = jnp.where(_head_ones(), 1.0, 0.0).astype(BF16)
    hs = [_split(h_ref[b]) for b in range(nb)]
    ys = [_dot3(_split(rh_ref[b]), hs[b]) + y0_ref[b] for b in range(nb)]
    for b in range(nb):
        h_ref[b] = _dot3((p_ref[b, 0, 0], p_ref[b, 0, 1]), hs[b]) + q_ref[b, 0]
    inv_n = 1.0 / HEAD
    for b in range(nb):
        y = ys[b]
        d = y - _dot_ones(y, head_ones) * inv_n
        var = _dot_ones(d * d, head_ones) * inv_n
        yn = d * lax.rsqrt(var + RW_GN_EPS) * gng_ref[...] + gnb_ref[...]
        o_ref[b] = (yn + bonus_ref[b]) * g_ref[b]


def _rw_scan(p, q, rh, y0, g, bonus, gn_g, gn_b):
    B, T, _ = rh.shape
    C = RW_CHUNK
    nc = T // C
    seq = pl.BlockSpec((B, C, RW_WIDTH), lambda c: (0, c, 0))
    vec = pl.BlockSpec((1, RW_WIDTH), lambda c: (0, 0))
    return pl.pallas_call(
        _rw_scan_body,
        grid=(nc,),
        in_specs=[pl.BlockSpec((B, 1, 2, RW_WIDTH, RW_WIDTH), lambda c: (0, c, 0, 0, 0)),
                  pl.BlockSpec((B, 1, RW_WIDTH, RW_WIDTH), lambda c: (0, c, 0, 0)),
                  seq, seq, seq, seq, vec, vec],
        out_specs=seq,
        out_shape=jax.ShapeDtypeStruct((B, T, RW_WIDTH), F32),
        scratch_shapes=[pltpu.VMEM((B, RW_WIDTH, RW_WIDTH), F32)],
        compiler_params=_cparams(("arbitrary",)),
        name="rwkv_chunk_scan",
    )(p, q, rh, y0, g, bonus, gn_g.reshape(1, -1), gn_b.reshape(1, -1))


def _pool_body(z_ref, zp_ref, w_ref, scale_ref, o_ref):
    i = pl.program_id(1)
    z = z_ref[0]
    tt = z.shape[0]
    halo = jnp.where(i == 0, 0.0, zp_ref[0])
    e = jnp.concatenate([halo, z], axis=0)
    sums = []
    s = e
    for shift in (1, 2, 4, 8):
        s = s + pltpu.roll(s, shift, axis=0)
        sums.append(s[POOL_HALO:, :])
    t_idx = i * tt + lax.broadcasted_iota(jnp.int32, (tt, POOL_WIDTH), 0)
    lane_group = lax.broadcasted_iota(jnp.int32, (tt, POOL_WIDTH), 1) // HEAD
    pooled = jnp.zeros_like(z)
    for gi, win in enumerate(POOL_WINDOWS):
        cnt = jnp.minimum(t_idx + 1, win).astype(F32)
        pooled = jnp.where(lane_group == gi, sums[gi] / cnt - z, pooled)
    o_ref[0] = _dot(pooled.astype(BF16), w_ref[...]) * scale_ref[...]


def _pool(z_pool, w_blockdiag_bf16, scale, tt=512):
    B, T, _ = z_pool.shape
    return pl.pallas_call(
        _pool_body,
        grid=(B, T // tt),
        in_specs=[pl.BlockSpec((1, tt, POOL_WIDTH), lambda b, i: (b, i, 0)),
                  pl.BlockSpec((1, POOL_HALO, POOL_WIDTH),
                               lambda b, i: (b, jnp.maximum(i * (tt // POOL_HALO) - 1, 0), 0)),
                  pl.BlockSpec((POOL_WIDTH, POOL_WIDTH), lambda b, i: (0, 0)),
                  pl.BlockSpec((1, POOL_WIDTH), lambda b, i: (0, 0))],
        out_specs=pl.BlockSpec((1, tt, POOL_WIDTH), lambda b, i: (b, i, 0)),
        out_shape=jax.ShapeDtypeStruct((B, T, POOL_WIDTH), F32),
        compiler_params=_cparams(("parallel", "parallel")),
        name="pool_mixer",
    )(z_pool, z_pool, w_blockdiag_bf16, scale.reshape(1, -1))


def _rope(x, cos_w, sin_w):
    w = x.shape[1]
    lane = lax.broadcasted_iota(jnp.int32, x.shape, 1) % HEAD
    partner = jnp.where(lane < HEAD // 2, pltpu.roll(x, w - HEAD // 2, axis=1), pltpu.roll(x, HEAD // 2, axis=1))
    return x * cos_w + partner * sin_w


def _head_lanes(a, h, dtype):
    part = a[:, h * HEAD:(h + 1) * HEAD]
    return jnp.concatenate([part, jnp.zeros_like(part)], axis=1).astype(dtype)


def _nsa_prep_body(zq_ref, zkc_ref, zks_ref, zvs_ref, zkw_ref, zvw_ref, zg_ref, cos_ref, sin_ref,
                   q_ref, kc_ref, ks_ref, vst_ref, kw_ref, vwt_ref, gt_ref):
    cos = cos_ref[0]
    sin = sin_ref[0]
    cos_q = jnp.concatenate([cos] * (NSA_WIDTH // LANE), axis=1)
    sin_q = jnp.concatenate([sin] * (NSA_WIDTH // LANE), axis=1)
    qr = _rope(zq_ref[0], cos_q, sin_q) * (HEAD ** -0.5 * LOG2E)
    pad_lane = lax.broadcasted_iota(jnp.int32, (qr.shape[0], LANE), 1) == HEAD
    for h in range(NSA_Q_HEADS):
        q_ref[0, h] = jnp.where(pad_lane, NEG, _head_lanes(qr, h, F32)).astype(BF16)
    kc_ref[0] = _rope(zkc_ref[0], cos, sin)
    ks = _rope(zks_ref[0], cos, sin)
    kw = _rope(zkw_ref[0], cos, sin)
    vst = zvs_ref[0].T
    vwt = zvw_ref[0].T
    for h in range(NSA_KV_HEADS):
        sl = slice(h * HEAD, (h + 1) * HEAD)
        ks_ref[0, h] = _head_lanes(ks, h, BF16)
        kw_ref[0, h] = _head_lanes(kw, h, BF16)
        tail = lax.broadcasted_iota(jnp.int32, (V_ROWS - HEAD, vst.shape[1]), 0)
        ones_row = jnp.where(tail == 0, 1.0, 0.0).astype(BF16)
        vst_ref[0, h, 0:HEAD, :] = vst[sl, :].astype(BF16)
        vst_ref[0, h, HEAD:V_ROWS, :] = ones_row
        vwt_ref[0, h, 0:HEAD, :] = vwt[sl, :].astype(BF16)
        vwt_ref[0, h, HEAD:V_ROWS, :] = ones_row
    gt = _sigmoid(zg_ref[0]).T
    for h in range(NSA_KV_HEADS):
        gt_ref[0, h] = gt[16 * h:16 * (h + 1), :]


def _nsa_prep(z_q, z_kc, z_ks, z_vs, z_kw, z_vw, z_g, cos_t, sin_t, tt=512):
    B, T, _ = z_q.shape
    seq = lambda w: pl.BlockSpec((1, tt, w), lambda b, i: (b, i, 0))
    hm = lambda nh: pl.BlockSpec((1, nh, tt, LANE), lambda b, i: (b, 0, i, 0))
    tr = lambda rows: pl.BlockSpec((1, NSA_KV_HEADS, rows, tt), lambda b, i: (b, 0, 0, i))
    sds = jax.ShapeDtypeStruct
    return pl.pallas_call(
        _nsa_prep_body,
        grid=(B, T // tt),
        in_specs=[seq(NSA_WIDTH)] + [seq(NSA_KV)] * 6 + [seq(LANE), seq(LANE)],
        out_specs=(hm(NSA_Q_HEADS), seq(NSA_KV), hm(NSA_KV_HEADS), tr(V_ROWS), hm(NSA_KV_HEADS), tr(V_ROWS), tr(16)),
        out_shape=(sds((B, NSA_Q_HEADS, T, LANE), BF16), sds((B, T, NSA_KV), F32),
                   sds((B, NSA_KV_HEADS, T, LANE), BF16), sds((B, NSA_KV_HEADS, V_ROWS, T), BF16),
                   sds((B, NSA_KV_HEADS, T, LANE), BF16), sds((B, NSA_KV_HEADS, V_ROWS, T), BF16),
                   sds((B, NSA_KV_HEADS, 16, T), F32)),
        compiler_params=_cparams(("parallel", "parallel")),
        name="nsa_prep",
    )(z_q, z_kc, z_ks, z_vs, z_kw, z_vw, z_g, cos_t, sin_t)


def _nsa_cmp_body(gk_ref, gv_ref, pek_ref, pev_ref, wk_ref, wv_ref, kc_ref, vct_ref):
    half = NSA_KV

    def compress(g, pe_ref, w_ref):
        n = g.shape[0]
        lo = _dot((g + pe_ref[0:1, :]).astype(BF16), w_ref[:, 0:half])
        hi = _dot((g + pe_ref[1:2, :]).astype(BF16), w_ref[:, half:2 * half])
        hi_next = pltpu.roll(hi, n - 1, axis=0)
        row = lax.broadcasted_iota(jnp.int32, lo.shape, 0)
        return lo + jnp.where(row == n - 1, 0.0, hi_next)

    kc = compress(gk_ref[0], pek_ref, wk_ref)
    vct = compress(gv_ref[0], pev_ref, wv_ref).T
    for h in range(NSA_KV_HEADS):
        sl = slice(h * HEAD, (h + 1) * HEAD)
        kc_ref[0, h] = _head_lanes(kc, h, BF16)
        vct_ref[0, h] = vct[sl, :].astype(BF16)


def _nsa_compress(kc_groups, vc_groups, pe_k2, pe_v2, wk2, wv2):
    B, ncp, gw = kc_groups.shape
    grp = pl.BlockSpec((1, ncp, gw), lambda b: (b, 0, 0))
    pe = pl.BlockSpec((2, gw), lambda b: (0, 0))
    wspec = pl.BlockSpec((gw, 2 * NSA_KV), lambda b: (0, 0))
    sds = jax.ShapeDtypeStruct
    return pl.pallas_call(
        _nsa_cmp_body,
        grid=(B,),
        in_specs=[grp, grp, pe, pe, wspec, wspec],
        out_specs=(pl.BlockSpec((1, NSA_KV_HEADS, ncp, LANE), lambda b: (b, 0, 0, 0)),
                   pl.BlockSpec((1, NSA_KV_HEADS, HEAD, ncp), lambda b: (b, 0, 0, 0))),
        out_shape=(sds((B, NSA_KV_HEADS, ncp, LANE), BF16), sds((B, NSA_KV_HEADS, HEAD, ncp), BF16)),
        compiler_params=_cparams(("parallel",)),
        name="nsa_compress",
    )(kc_groups, vc_groups, pe_k2, pe_v2, wk2, wv2)


def _nsa_attn_body(q_ref, kc_ref, vct_ref, ks_ref, vst_ref, kw_ref, vwt_ref, gt_ref, mcs_ref, blk_ref, cmpb_ref,
                   winb_ref, o_ref, *score_refs):
    jb = pl.program_id(1)
    s0 = jb * Q_BLOCK
    cols = NSA_GQA * Q_BLOCK
    heads = range(NSA_KV_HEADS)
    q = [q_ref[0, NSA_GQA * h:NSA_GQA * (h + 1)].reshape(cols, LANE) for h in heads]
    ncp = kc_ref.shape[2]
    per_gqa = lambda x: jnp.concatenate([x] * NSA_GQA, axis=1)
    col_max = lambda x: jnp.max(x, axis=0, keepdims=True)

    cmp_mask = per_gqa(cmpb_ref[pl.ds(pl.multiple_of(ncp - s0 // CMP_STRIDE, 8), ncp), :])
    t_c = s0 + lax.broadcasted_iota(jnp.int32, (1, cols), 1) % Q_BLOCK
    any_valid = jnp.where(t_c >= CMP_BLOCK - 1, 1.0, 0.0)
    mcs = mcs_ref[...]
    sc = [_dot_nt(kc_ref[0, h], q[h]) + cmp_mask for h in heads]
    e = [jnp.exp2(x - col_max(x)) for x in sc]
    pc = [x * (any_valid / jnp.sum(x, axis=0, keepdims=True)) for x in e]
    o_cmp = [_dot(vct_ref[0, h], pc[h].astype(BF16)) for h in heads]

    def importance(p):
        psum = p[:, 0:Q_BLOCK]
        for g in range(1, NSA_GQA):
            psum = psum + p[:, g * Q_BLOCK:(g + 1) * Q_BLOCK]
        p_hi = psum.astype(BF16)
        rest = psum - p_hi.astype(F32)
        p_mid = rest.astype(BF16)
        p_lo = (rest - p_mid.astype(F32)).astype(BF16)
        return _dot(mcs, p_hi) + (_dot(mcs, p_mid) + _dot(mcs, p_lo))

    imp = [importance(p) for p in pc]

    wkeys = WINDOW + Q_BLOCK
    off_w = pl.multiple_of(s0, Q_BLOCK)
    tri_old, tri_new = per_gqa(winb_ref[0]), per_gqa(winb_ref[1])
    sw = [_dot_nt(kw_ref[0, h, pl.ds(off_w, wkeys), :], q[h]) for h in heads]
    sw = [jnp.concatenate([x[0:Q_BLOCK] + tri_old, x[Q_BLOCK:WINDOW], x[WINDOW:wkeys] + tri_new], axis=0) for x in sw]
    pw = [jnp.exp2((x - col_max(x)).astype(BF16)) for x in sw]
    acc_w = [_dot(vwt_ref[0, h, :, pl.ds(off_w, wkeys)], pw[h]) for h in heads]

    j_idx = lax.broadcasted_iota(jnp.int32, (NSEL_PAD, Q_BLOCK), 0)
    jc = (s0 + lax.broadcasted_iota(jnp.int32, (NSEL_PAD, Q_BLOCK), 1)) // SEL_BLOCK
    causal = j_idx <= jc
    forced = lambda x: jnp.where(j_idx == 0, FORCE, jnp.where(j_idx == jc, FORCE, jnp.where(j_idx == jc - 1, FORCE, x)))
    work = [jnp.where(causal, forced(x), NEG) for x in imp]
    j_f = j_idx.astype(F32)
    for _ in range(SEL_TOPN):
        top = [col_max(w) for w in work]
        first = [jnp.min(jnp.where(w == t, j_f, float(NSEL_PAD)), axis=0, keepdims=True) for w, t in zip(work, top)]
        work = [jnp.where(j_f == f, -jnp.inf, w) for w, f in zip(work, first)]
    bias = [jnp.where(causal, jnp.where(w == -jnp.inf, 0.0, NEG), NEG) for w in work]

    q_aug = [jnp.concatenate([q[h], jnp.concatenate([bias[h].T.astype(BF16)] * NSA_GQA, axis=0)], axis=1)
             for h in heads]

    def sel_scores(h, off):
        k_aug = jnp.concatenate([ks_ref[0, h, pl.ds(off, KEY_GROUP), :], blk_ref[pl.ds(off, KEY_GROUP), :]], axis=1)
        return _dot_nt(k_aug, q_aug[h])

    def flash_update(v_t, st, carry):
        m, acc = carry
        m_new = jnp.maximum(m, col_max(st))
        p = jnp.exp2((st - m_new).astype(BF16))
        return m_new, acc * jnp.exp2(m - m_new) + _dot(v_t, p)

    init = (jnp.full((1, cols), NEG, F32), jnp.zeros((V_ROWS, cols), F32))
    n_groups = s0 // KEY_GROUP + 1
    n_pairs = n_groups // 2
    t_q = s0 + lax.broadcasted_iota(jnp.int32, (Q_BLOCK, cols), 1) % Q_BLOCK
    r_q = lax.broadcasted_iota(jnp.int32, (Q_BLOCK, cols), 0)

    def put_scores(ref, h, g):
        off = pl.multiple_of(g * KEY_GROUP, KEY_GROUP)
        ref[...] = sel_scores(h, off)
        r = pl.multiple_of(jnp.clip(s0 - off, 0, KEY_GROUP - Q_BLOCK), Q_BLOCK)
        tile = ref[pl.ds(r, Q_BLOCK), :]
        ref[pl.ds(r, Q_BLOCK), :] = jnp.where(off + r + r_q <= t_q, tile, NEG)

    def consume(ref, h, g, carry):
        off = pl.multiple_of(g * KEY_GROUP, KEY_GROUP)
        return flash_update(vst_ref[0, h, :, pl.ds(off, KEY_GROUP)], ref[...], carry)

    last = n_groups - 1
    buf0 = [(score_refs[4 * h], score_refs[4 * h + 1]) for h in heads]
    buf1 = [(score_refs[4 * h + 2], score_refs[4 * h + 3]) for h in heads]
    for h in heads:
        put_scores(buf0[h][0], h, 0)
    for h in heads:
        put_scores(buf0[h][1], h, jnp.minimum(1, last))

    def stage(cur, nxt, pair, carries):
        for h in heads:
            put_scores(nxt[h][0], h, jnp.minimum(2 * pair + 2, last))
        even = [consume(cur[h][0], h, 2 * pair, carries[h][0]) for h in heads]
        for h in heads:
            put_scores(nxt[h][1], h, jnp.minimum(2 * pair + 3, last))
        odd = [consume(cur[h][1], h, 2 * pair + 1, carries[h][1]) for h in heads]
        return tuple((even[h], odd[h]) for h in heads)

    def two_pairs(i, carries):
        return stage(buf1, buf0, 2 * i + 1, stage(buf0, buf1, 2 * i, carries))

    carries = lax.fori_loop(0, n_pairs // 2, two_pairs, tuple((init, init) for _ in heads))
    odd_pairs = n_pairs % 2
    carries = lax.fori_loop(n_pairs - odd_pairs, n_pairs, lambda pr, c: stage(buf0, buf1, pr, c), carries)
    left = n_groups - 2 * n_pairs

    def leftover(buf):
        return lambda _, c: tuple((consume(buf[h][0], h, last, c[h][0]), c[h][1]) for h in heads)

    carries = lax.fori_loop(0, left * (1 - odd_pairs), leftover(buf0), carries)
    carries = lax.fori_loop(0, left * odd_pairs, leftover(buf1), carries)

    for h in heads:
        (m_e, acc_e), (m_o, acc_o) = carries[h]
        m_s = jnp.maximum(m_e, m_o)
        acc_s = acc_e * jnp.exp2(m_e - m_s) + acc_o * jnp.exp2(m_o - m_s)

        def gate_row(branch):
            return jnp.concatenate([gt_ref[0, h, 3 * g + branch:3 * g + branch + 1, :] for g in range(NSA_GQA)], axis=1)

        o_t = (gate_row(0) * o_cmp[h] + (gate_row(1) / acc_s[HEAD:HEAD + 1]) * acc_s[0:HEAD]
               + (gate_row(2) / acc_w[h][HEAD:HEAD + 1]) * acc_w[h][0:HEAD])
        stacked = jnp.concatenate([o_t[:, g * Q_BLOCK:(g + 1) * Q_BLOCK] for g in range(NSA_GQA)], axis=0)
        o_ref[0, :, NSA_GQA * HEAD * h:NSA_GQA * HEAD * (h + 1)] = stacked.T


def _nsa_masks(seq_len):
    tq = np.arange(Q_BLOCK)[None, :]
    ncp = seq_len // CMP_STRIDE
    n_rel = np.arange(2 * ncp)[:, None] - ncp
    cmp_b = np.where(CMP_STRIDE * n_rel + CMP_BLOCK - 1 <= tq, 0.0, NEG).astype(np.float32)
    k_rel = np.arange(Q_BLOCK)[:, None]
    win_b = np.stack([np.where(k_rel > tq, 0.0, NEG), np.where(k_rel <= tq, 0.0, NEG)]).astype(np.float32)
    return jnp.asarray(cmp_b), jnp.asarray(win_b)


def _nsa_attention(q_r, kcmp, vcmp_t, ks_r, vs_t, kw_pad, vw_pad_t, g_t, mcs_t, blk_onehot, cmp_b, win_b):
    B, _, T, _ = q_r.shape
    ncp = kcmp.shape[2]
    nqb = T // Q_BLOCK
    tw = T + WINDOW
    nkv = NSA_KV_HEADS
    assert T % KEY_GROUP == 0 and T // SEL_BLOCK <= NSEL_PAD and WINDOW > Q_BLOCK and WINDOW % Q_BLOCK == 0
    kv_rows = lambda n: pl.BlockSpec((1, nkv, n, LANE), lambda b, j: (b, 0, 0, 0))
    kv_cols = lambda n, rows: pl.BlockSpec((1, nkv, rows, n), lambda b, j: (b, 0, 0, 0))
    const = lambda shp: pl.BlockSpec(shp, lambda b, j: (0,) * len(shp))
    return pl.pallas_call(
        _nsa_attn_body,
        grid=(B, nqb),
        in_specs=[pl.BlockSpec((1, NSA_Q_HEADS, Q_BLOCK, LANE), lambda b, j: (b, 0, j, 0)),
                  kv_rows(ncp), kv_cols(ncp, HEAD), kv_rows(T), kv_cols(T, V_ROWS), kv_rows(tw), kv_cols(tw, V_ROWS),
                  pl.BlockSpec((1, nkv, 16, Q_BLOCK), lambda b, j: (b, 0, 0, j)),
                  const((NSEL_PAD, ncp)), const((T, NSEL_PAD)), const((2 * ncp, Q_BLOCK)),
                  const((2, Q_BLOCK, Q_BLOCK))],
        out_specs=pl.BlockSpec((1, Q_BLOCK, NSA_WIDTH), lambda b, j: (b, j, 0)),
        out_shape=jax.ShapeDtypeStruct((B, T, NSA_WIDTH), F32),
        scratch_shapes=[pltpu.VMEM((KEY_GROUP, NSA_GQA * Q_BLOCK), F32)] * (4 * nkv),
        compiler_params=_cparams(("parallel", "arbitrary")),
        name="nsa_attention",
    )(q_r, kcmp, vcmp_t, ks_r, vs_t, kw_pad, vw_pad_t, g_t, mcs_t, blk_onehot, cmp_b, win_b)


def _out_proj_body(x_ref, ya_ref, yb_ref, yc_ref, wa_ref, wb_ref, wc_ref, o_ref):
    acc = _dot(ya_ref[...].astype(BF16), wa_ref[...])
    acc = acc + _dot(yb_ref[...].astype(BF16), wb_ref[...])
    acc = acc + _dot(yc_ref[...].astype(BF16), wc_ref[...])
    o_ref[...] = x_ref[...] + acc


def _out_proj(x2, ya, yb, yc, w_out_bf16, tm=512):
    m, d = x2.shape
    wa, wb, wc = w_out_bf16[:RW_WIDTH], w_out_bf16[RW_WIDTH:RW_WIDTH + POOL_WIDTH], w_out_bf16[RW_WIDTH + POOL_WIDTH:]
    tile = lambda w: pl.BlockSpec((tm, w), lambda i: (i, 0))
    wfull = lambda w: pl.BlockSpec((w, d), lambda i: (0, 0))
    return pl.pallas_call(
        _out_proj_body,
        grid=(m // tm,),
        in_specs=[tile(d), tile(RW_WIDTH), tile(POOL_WIDTH), tile(NSA_WIDTH),
                  wfull(RW_WIDTH), wfull(POOL_WIDTH), wfull(NSA_WIDTH)],
        out_specs=tile(d),
        out_shape=jax.ShapeDtypeStruct((m, d), F32),
        compiler_params=_cparams(("parallel",)),
        name="out_proj_residual",
    )(x2, ya, yb, yc, wa, wb, wc)


def _gelu_tanh(x):
    return 0.5 * x * (1.0 + jnp.tanh(math.sqrt(2.0 / math.pi) * (x + 0.044715 * (x * x * x))))


def _ffn_body(x_ref, g_ref, wup_ref, cw_ref, cb_ref, wdn_ref, o_ref, carry_ref, *, tiles_per_seq, f_chunk):
    i = pl.program_id(0)
    x = x_ref[...]
    tm = x.shape[0]
    d_ff = wdn_ref.shape[0]
    h = _rms(x, g_ref[...]).astype(BF16)
    first = (i % tiles_per_seq) == 0
    row = lax.broadcasted_iota(jnp.int32, (tm, f_chunk), 0)

    def conv(col0):
        u = _dot(h, wup_ref[:, col0:col0 + f_chunk])
        p2 = jnp.where(first, 0.0, carry_ref[6:7, col0:col0 + f_chunk])
        p1 = jnp.where(first, 0.0, carry_ref[7:8, col0:col0 + f_chunk])
        u1 = jnp.where(row == 0, p1, pltpu.roll(u, 1, axis=0))
        u2 = jnp.where(row == 0, p2, jnp.where(row == 1, p1, pltpu.roll(u, 2, axis=0)))
        carry_ref[:, col0:col0 + f_chunk] = u[tm - 8:tm, :]
        cw = cw_ref[:, col0:col0 + f_chunk]
        return cw[0:1] * u2 + cw[1:2] * u1 + cw[2:3] * u + cb_ref[:, col0:col0 + f_chunk]

    acc = x
    for c0 in range(0, d_ff, f_chunk):
        act = _gelu_tanh(conv(c0)) * conv(d_ff + c0)
        acc = acc + _dot(act.astype(BF16), wdn_ref[c0:c0 + f_chunk, :])
    o_ref[...] = acc


def _conv_ffn(x2, g, w_up_bf16, conv_w, conv_b, w_down_bf16, seq_len, tm=512, f_chunk=1408):
    m, d = x2.shape
    f2 = w_up_bf16.shape[1]
    d_ff = f2 // 2
    assert d_ff % f_chunk == 0 and f_chunk % LANE == 0 and seq_len % tm == 0
    full = lambda shp: pl.BlockSpec(shp, lambda i: (0,) * len(shp))
    return pl.pallas_call(
        functools.partial(_ffn_body, tiles_per_seq=seq_len // tm, f_chunk=f_chunk),
        grid=(m // tm,),
        in_specs=[pl.BlockSpec((tm, d), lambda i: (i, 0)), full((1, d)), full((d, f2)),
                  full((CONV_W, f2)), full((1, f2)), full((d_ff, d))],
        out_specs=pl.BlockSpec((tm, d), lambda i: (i, 0)),
        out_shape=jax.ShapeDtypeStruct((m, d), F32),
        scratch_shapes=[pltpu.VMEM((8, f2), F32)],
        compiler_params=_cparams(("arbitrary",)),
        name="conv_ffn",
    )(x2, g.reshape(1, d), w_up_bf16, conv_w.reshape(CONV_W, f2), conv_b.reshape(1, f2), w_down_bf16)


def _ple_body(x_ref, p_ref, g_ref, wg_ref, wp_ref, gf_ref, o_ref, *, final):
    x = x_ref[...]
    gate = _sigmoid(_dot(_rms(x, g_ref[...]).astype(BF16), wg_ref[...]))
    y = x + _dot(p_ref[...].astype(BF16), wp_ref[...]) * gate
    o_ref[...] = _rms(y, gf_ref[...]) if final else y


def _ple(x2, p2, g, w_gate_bf16, w_proj_bf16, g_final, final, tm=512):
    m, d = x2.shape
    pd = p2.shape[1]
    full = lambda shp: pl.BlockSpec(shp, lambda i: (0,) * len(shp))
    return pl.pallas_call(
        functools.partial(_ple_body, final=final),
        grid=(m // tm,),
        in_specs=[pl.BlockSpec((tm, d), lambda i: (i, 0)), pl.BlockSpec((tm, pd), lambda i: (i, 0)),
                  full((1, d)), full((d, d)), full((pd, d)), full((1, d))],
        out_specs=pl.BlockSpec((tm, d), lambda i: (i, 0)),
        out_shape=jax.ShapeDtypeStruct((m, d), F32),
        compiler_params=_cparams(("parallel",)),
        name="ple_final_norm" if final else "ple",
    )(x2, p2, g.reshape(1, d), w_gate_bf16, w_proj_bf16, g_final.reshape(1, d))


def _overlap_matrix(seq_len):
    ncp = seq_len // CMP_STRIDE
    n_cmp = (seq_len - CMP_BLOCK) // CMP_STRIDE + 1
    cs = CMP_STRIDE * np.arange(n_cmp)
    ss = SEL_BLOCK * np.arange(seq_len // SEL_BLOCK)
    ov = (np.minimum(cs[:, None] + CMP_BLOCK - 1, ss[None] + SEL_BLOCK - 1) - np.maximum(cs[:, None], ss[None]) + 1)
    m = np.zeros((NSEL_PAD, ncp), np.float32)
    m[:ss.size, :n_cmp] = (np.clip(ov, 0, CMP_BLOCK).astype(np.float32) / CMP_BLOCK).T
    return jnp.asarray(m, dtype=BF16)


def _compress_weight(w):
    halves = w.reshape(2, CMP_STRIDE, HEAD, HEAD)
    eye = jnp.eye(NSA_KV_HEADS, dtype=w.dtype)
    w2 = jnp.einsum('sldf,hg->lhdsgf', halves, eye)
    return w2.reshape(CMP_STRIDE * NSA_KV, 2 * NSA_KV).astype(BF16)


def _compress_pe(pe):
    halves = pe.reshape(2, CMP_STRIDE, 1, HEAD)
    return jnp.broadcast_to(halves, (2, CMP_STRIDE, NSA_KV_HEADS, HEAD)).reshape(2, CMP_STRIDE * NSA_KV)


def _in_proj_layout(w_in):
    d = w_in.shape[0]
    gate0 = RW_IN + POOL_WIDTH + NSA_WIDTH + 6 * NSA_KV
    per_head = NSA_GQA * 3
    gates = jnp.zeros((d, LANE), w_in.dtype)
    for h in range(NSA_KV_HEADS):
        gates = gates.at[:, 16 * h:16 * h + per_head].set(w_in[:, gate0 + per_head * h:gate0 + per_head * (h + 1)])
    return jnp.concatenate([w_in[:, :gate0], gates], axis=1).astype(BF16)


_IN_SPLITS = ((0, RW_IN), (RW_IN, RW_IN + POOL_WIDTH), (RW_IN + POOL_WIDTH, RW_IN + POOL_WIDTH + NSA_WIDTH)) + tuple(
    (RW_IN + POOL_WIDTH + NSA_WIDTH + i * NSA_KV, RW_IN + POOL_WIDTH + NSA_WIDTH + (i + 1) * NSA_KV) for i in range(7))


def kernel(x, p, positions, g_mix, w_in, rw_mu, rw_w0, rw_w_up, rw_a0, rw_a_up, rw_g_up, rw_k_k, rw_k_a, rw_r_k, rw_gn_g, rw_gn_b, pool_w, pool_scale, nsa_pe_k, nsa_pe_v, nsa_w_ck, nsa_w_cv, w_out, g_ffn, ffn_w_up, ffn_conv_w, ffn_conv_b, ffn_w_down, g_ple, ple_w_gate, ple_w_proj, g_final):
    B, T, D = x.shape
    depth = w_in.shape[0]
    M = B * T

    half = HEAD // 2
    inv = ROPE_THETA ** (-jnp.arange(half, dtype=F32) / half)
    ang = positions.astype(F32)[..., None] * inv
    cos, sin = jnp.cos(ang), jnp.sin(ang)
    cos_t = jnp.concatenate([cos, cos] * (LANE // HEAD), axis=-1)
    sin_t = jnp.concatenate([-sin, sin] * (LANE // HEAD), axis=-1)
    mcs_t = _overlap_matrix(T)
    blk_onehot = (jnp.arange(T)[:, None] // SEL_BLOCK == jnp.arange(NSEL_PAD)[None, :]).astype(BF16)
    cmp_b, win_b = _nsa_masks(T)
    key_pad = (jnp.arange(LANE) == HEAD).astype(BF16)

    x2 = x.reshape(M, D)
    for i in range(depth):
        zs = _norm_matmul(x2, g_mix[i], _in_proj_layout(w_in[i]), _IN_SPLITS)
        z_rw, z_pool, z_q, z_kc, z_vc, z_ks, z_vs, z_kw, z_vw, z_g = (
            z.reshape(B, T, z.shape[1]) for z in zs)

        pm, qm, rh, y0, gate, bonus = _rw_chunks(z_rw, rw_mu[i], rw_w0[i], rw_w_up[i], rw_a0[i], rw_a_up[i],
                                                 rw_g_up[i], rw_k_k[i], rw_k_a[i], rw_r_k[i].reshape(-1))
        y_a = _rw_scan(pm, qm, rh, y0, gate, bonus, rw_gn_g[i], rw_gn_b[i])

        w_pool_bd = jax.scipy.linalg.block_diag(*[pool_w[i, gi] for gi in range(pool_w.shape[1])]).astype(BF16)
        y_b = _pool(z_pool, w_pool_bd, pool_scale[i])

        q_r, kc_r, ks_r, vs_t, kw_r, vw_t, g_t = _nsa_prep(z_q, z_kc, z_ks, z_vs, z_kw, z_vw, z_g, cos_t, sin_t)
        grp = (B, T // CMP_STRIDE, CMP_STRIDE * NSA_KV)
        kcmp, vcmp_t = _nsa_compress(kc_r.reshape(grp), z_vc.reshape(grp),
                                     _compress_pe(nsa_pe_k[i]), _compress_pe(nsa_pe_v[i]),
                                     _compress_weight(nsa_w_ck[i]), _compress_weight(nsa_w_cv[i]))
        kw_pad = jnp.concatenate([jnp.broadcast_to(key_pad, (B, NSA_KV_HEADS, WINDOW, LANE)), kw_r], axis=2)
        vw_pad_t = jnp.pad(vw_t, ((0, 0), (0, 0), (0, 0), (WINDOW, 0)))
        y_c = _nsa_attention(q_r, kcmp, vcmp_t, ks_r, vs_t, kw_pad, vw_pad_t, g_t, mcs_t, blk_onehot, cmp_b, win_b)

        x2 = _out_proj(x2, y_a.reshape(M, -1), y_b.reshape(M, -1), y_c.reshape(M, -1), w_out[i].astype(BF16))
        x2 = _conv_ffn(x2, g_ffn[i], ffn_w_up[i].astype(BF16), ffn_conv_w[i], ffn_conv_b[i],
                       ffn_w_down[i].astype(BF16), T)
        x2 = _ple(x2, p[i].reshape(M, -1), g_ple[i], ple_w_gate[i].astype(BF16), ple_w_proj[i].astype(BF16),
                  g_final, final=(i == depth - 1))
    return x2.reshape(B, T, D)
```

```python
import functools
import math

import numpy as np
import jax
import jax.numpy as jnp
from jax import lax
from jax.experimental import pallas as pl
from jax.experimental.pallas import tpu as pltpu

F32 = jnp.float32
BF16 = jnp.bfloat16

RMS_EPS = 1e-6
ROPE_THETA = 10000.0
HEAD = 64
RW_HEADS = 4
RW_WIDTH = RW_HEADS * HEAD
RW_DECAY_LORA = 64
RW_AAA_LORA = 64
RW_GATE_LORA = 128
RW_GN_EPS = 64e-5
RW_IN = 3 * RW_WIDTH + RW_DECAY_LORA + RW_AAA_LORA + RW_GATE_LORA
POOL_WIDTH = 256
POOL_WINDOWS = (2, 4, 8, 16)
POOL_HALO = 16
NSA_Q_HEADS = 8
NSA_KV_HEADS = 2
NSA_GQA = NSA_Q_HEADS // NSA_KV_HEADS
NSA_WIDTH = NSA_Q_HEADS * HEAD
NSA_KV = NSA_KV_HEADS * HEAD
CMP_BLOCK = 32
CMP_STRIDE = 16
SEL_BLOCK = 64
SEL_TOPN = 16
WINDOW = 512
Q_BLOCK = 128
KEY_GROUP = 512
V_ROWS = HEAD + 16
NSEL_PAD = 128
NEG = -1e30
LOG2E = math.log2(math.e)
FORCE = 1e9
CONV_W = 3
LANE = 128
VMEM_LIMIT = 56 * 1024 * 1024

RW_CHUNK = 64


def _cparams(sem):
    return pltpu.CompilerParams(dimension_semantics=sem, vmem_limit_bytes=VMEM_LIMIT)


def _dot(a, b, prec=None):
    return lax.dot_general(a, b, (((1,), (0,)), ((), ())), precision=prec, preferred_element_type=F32)


def _dot_nt(a, b, prec=None):
    return lax.dot_general(a, b, (((1,), (1,)), ((), ())), precision=prec, preferred_element_type=F32)


def _sigmoid(x):
    return 1.0 / (1.0 + jnp.exp(-x))


def _rms(x, g):
    ms = jnp.mean(x * x, axis=-1, keepdims=True)
    return (x * lax.rsqrt(ms + RMS_EPS)) * g


def _norm_mm_body(x_ref, g_ref, w_ref, *o_refs, splits):
    h = _rms(x_ref[...], g_ref[...]).astype(BF16)
    for o_ref, (lo, hi) in zip(o_refs, splits):
        o_ref[...] = _dot(h, w_ref[:, lo:hi])


def _norm_matmul(x2, g, w_bf16, splits, tm=1024):
    m, d = x2.shape
    n = w_bf16.shape[1]
    outs = tuple(jax.ShapeDtypeStruct((m, hi - lo), F32) for lo, hi in splits)
    return pl.pallas_call(
        functools.partial(_norm_mm_body, splits=splits),
        grid=(m // tm,),
        in_specs=[pl.BlockSpec((tm, d), lambda i: (i, 0)),
                  pl.BlockSpec((1, d), lambda i: (0, 0)),
                  pl.BlockSpec((d, n), lambda i: (0, 0))],
        out_specs=tuple(pl.BlockSpec((tm, hi - lo), lambda i: (i, 0)) for lo, hi in splits),
        out_shape=outs,
        compiler_params=_cparams(("parallel",)),
        name="norm_in_proj",
    )(x2, g.reshape(1, d), w_bf16)


def _split(x):
    hi = x.astype(BF16)
    return hi, (x - hi.astype(F32)).astype(BF16)


def _dot3(a, b, dot=_dot):
    return dot(a[0], b[0]) + (dot(a[0], b[1]) + dot(a[1], b[0]))


def _dot_ones(x, ones_bf16):
    hi, lo = _split(x)
    return _dot(hi, ones_bf16) + _dot(lo, ones_bf16)


def _head_ones():
    hr = lax.broadcasted_iota(jnp.int32, (RW_WIDTH, RW_WIDTH), 0) // HEAD
    hc = lax.broadcasted_iota(jnp.int32, (RW_WIDTH, RW_WIDTH), 1) // HEAD
    return hr == hc


def _rw_chunk_body(z_ref, zp_ref, mu_ref, w0_ref, wup_ref, a0_ref, aup_ref, gup_ref, kk_ref, ka_ref,
                   rk_ref, p_ref, q_ref, rh_ref, y0_ref, g_ref, bonus_ref):
    i = pl.program_id(1)
    C = RW_CHUNK
    W = RW_WIDTH
    z = z_ref[0]
    rows = z.shape[0]
    n_chunks = rows // C
    prev = jnp.where(i == 0, 0.0, zp_ref[0, 7:8, :])
    row_z = lax.broadcasted_iota(jnp.int32, z.shape, 0)
    zs = jnp.where(row_z == 0, prev, pltpu.roll(z, 1, axis=0))
    zf = z + (zs - z) * mu_ref[...]
    r = zf[:, 0:W]
    k = zf[:, W:2 * W]
    v = zf[:, 2 * W:3 * W]
    zw = zf[:, 3 * W:3 * W + RW_DECAY_LORA]
    za = zf[:, 3 * W + RW_DECAY_LORA:3 * W + RW_DECAY_LORA + RW_AAA_LORA]
    zg = zf[:, 3 * W + RW_DECAY_LORA + RW_AAA_LORA:]

    lora = lambda x, w_ref: _dot3(_split(x), (w_ref[0], w_ref[1]))
    xw = -(w0_ref[...] + lora(jnp.tanh(zw), wup_ref))
    softplus = jnp.maximum(xw, 0.0) + jnp.log(1.0 + jnp.exp(-jnp.abs(xw)))
    ld = -jnp.exp(-softplus - 0.5)
    a = _sigmoid(a0_ref[...] + lora(za, aup_ref))
    g_ref[0] = lora(_sigmoid(zg), gup_ref)
    kk = k * kk_ref[...]
    k2 = k * (1.0 + (a - 1.0) * ka_ref[...])

    same_head = _head_ones()
    head_ones = jnp.where(same_head, 1.0, 0.0).astype(BF16)
    kkn = kk / jnp.maximum(jnp.sqrt(_dot_ones(kk * kk, head_ones)), 1e-12)
    bonus_ref[0] = _dot_ones(r * k2 * rk_ref[...], head_ones) * v

    t_in = lax.broadcasted_iota(jnp.int32, (rows, W), 0) % C
    lc = ld
    shift = 1
    while shift < C:
        lc = lc + jnp.where(t_in >= shift, pltpu.roll(lc, shift, axis=0), 0.0)
        shift *= 2

    tr = lax.broadcasted_iota(jnp.int32, (W, W), 0) % C
    tc = lax.broadcasted_iota(jnp.int32, (W, W), 1) % C
    lower_incl = tr >= tc
    lower_strict = tr > tc
    eye = jnp.where(same_head, jnp.where(tr == tc, 1.0, 0.0), 0.0)

    def embed(x):
        return jnp.where(same_head, jnp.concatenate([x] * RW_HEADS, axis=0), 0.0)

    bf = lambda x: x.astype(BF16)
    A, Bt, Kt, Rt, V, BhT, KhT, gdiag, Rt_raw = [], [], [], [], [], [], [], [], []
    for c in range(n_chunks):
        rs = slice(c * C, (c + 1) * C)
        lc_c, ld_c = lc[rs], ld[rs]
        lc_end = lc_c[C - 1:C, :]
        ginv = jnp.exp(-lc_c)
        dec_end = jnp.exp(lc_end - lc_c)
        kkn_c, a_c, k2_c = kkn[rs], a[rs], k2[rs]
        r_t = r[rs] * jnp.exp(lc_c)
        A.append(bf(embed(-kkn_c * jnp.exp(lc_c - ld_c))))
        Bt.append(bf(embed(kkn_c * a_c * ginv)))
        Kt.append(bf(embed(k2_c * ginv)))
        Rt.append(bf(embed(r_t)))
        Rt_raw.append(r_t)
        V.append(bf(embed(v[rs])))
        BhT.append(bf(embed(kkn_c * a_c * dec_end).T))
        KhT.append(bf(embed(k2_c * dec_end).T))
        gdiag.append(jnp.where(eye > 0.5, jnp.broadcast_to(jnp.exp(lc_end), (W, W)), 0.0))

    each = lambda f, *lists: [f(*xs) for xs in zip(*lists)]
    L = each(lambda x, y: jnp.where(lower_strict, _dot_nt(x, y), 0.0), A, Bt)
    Lak = each(lambda x, y: bf(jnp.where(lower_strict, _dot_nt(x, y), 0.0)), A, Kt)
    Mrb = each(lambda x, y: bf(jnp.where(lower_incl, _dot_nt(x, y), 0.0)), Rt, Bt)
    Mrk = each(lambda x, y: bf(jnp.where(lower_incl, _dot_nt(x, y), 0.0)), Rt, Kt)
    Tm = [eye + x for x in L]
    Pw = each(bf, L)
    for _ in range(int(math.log2(C)) - 1):
        Pw = each(lambda x: bf(_dot(x, x)), Pw)
        Tm = each(lambda t, x: t + _dot(bf(t), x), Tm, Pw)
    Tm = each(bf, Tm)
    Wm = each(lambda t, x: bf(_dot(t, x)), Tm, A)
    LakV = each(lambda x, y: bf(_dot(x, y)), Lak, V)
    U0 = each(lambda t, x: bf(_dot(t, x)), Tm, LakV)
    Pm = each(lambda d, x, y: d + _dot(x, y), gdiag, BhT, Wm)
    Qm = each(lambda x, y, s, t: _dot(x, y) + _dot(s, t), BhT, U0, KhT, V)
    Rh = each(_dot, Mrb, Wm)
    Y0 = each(lambda x, y, s, t: _dot(x, y) + _dot(s, t), Mrb, U0, Mrk, V)

    def flatten(x_bd):
        out = x_bd[0:C]
        for h in range(1, RW_HEADS):
            out = out + x_bd[h * C:(h + 1) * C]
        return out

    for c in range(n_chunks):
        rs = slice(c * C, (c + 1) * C)
        p_hi, p_lo = _split(Pm[c])
        p_ref[0, c, 0] = p_hi
        p_ref[0, c, 1] = p_lo
        q_ref[0, c] = Qm[c]
        rh_ref[0, rs, :] = Rt_raw[c] + flatten(Rh[c])
        y0_ref[0, rs, :] = flatten(Y0[c])


def _rw_chunks(z_rw, mu, w0, w_up, a0, a_up, g_up, k_k, k_a, r_k, chunks_per_step=8):
    B, T, _ = z_rw.shape
    C = RW_CHUNK
    assert C == HEAD, "the chunk build shares one index grid between time and head-dim masks"
    rows = C * chunks_per_step
    nc = T // C
    row = lambda a: a.reshape(1, -1)
    pair = lambda w: jnp.stack(_split(w))
    full = lambda shp: pl.BlockSpec(shp, lambda b, i: (0,) * len(shp))
    seq = lambda w: pl.BlockSpec((1, rows, w), lambda b, i: (b, i, 0))
    sds = jax.ShapeDtypeStruct
    seq_shape = sds((B, T, RW_WIDTH), F32)
    return pl.pallas_call(
        _rw_chunk_body,
        grid=(B, T // rows),
        in_specs=[seq(RW_IN),
                  pl.BlockSpec((1, 8, RW_IN), lambda b, i: (b, jnp.maximum(i * (rows // 8) - 1, 0), 0)),
                  full((1, RW_IN)), full((1, RW_WIDTH)), full((2, RW_DECAY_LORA, RW_WIDTH)),
                  full((1, RW_WIDTH)), full((2, RW_AAA_LORA, RW_WIDTH)), full((2, RW_GATE_LORA, RW_WIDTH)),
                  full((1, RW_WIDTH)), full((1, RW_WIDTH)), full((1, RW_WIDTH))],
        out_specs=(pl.BlockSpec((1, chunks_per_step, 2, RW_WIDTH, RW_WIDTH), lambda b, i: (b, i, 0, 0, 0)),
                   pl.BlockSpec((1, chunks_per_step, RW_WIDTH, RW_WIDTH), lambda b, i: (b, i, 0, 0)),
                   seq(RW_WIDTH), seq(RW_WIDTH), seq(RW_WIDTH), seq(RW_WIDTH)),
        out_shape=(sds((B, nc, 2, RW_WIDTH, RW_WIDTH), BF16), sds((B, nc, RW_WIDTH, RW_WIDTH), F32),
                   seq_shape, seq_shape, seq_shape, seq_shape),
        compiler_params=_cparams(("parallel", "parallel")),
        name="rwkv_chunk_build",
    )(z_rw, z_rw, row(mu), row(w0), pair(w_up), row(a0), pair(a_up), pair(g_up), row(k_k), row(k_a), row(r_k))


def _rw_scan_body(p_ref, q_ref, rh_ref, y0_ref, g_ref, bonus_ref, gng_ref, gnb_ref, o_ref, h_ref):
    @pl.when(pl.program_id(0) == 0)
    def _():
        h_ref[...] = jnp.zeros_like(h_ref)

    nb = h_ref.shape[0]
    head_ones = jnp.where(_head_ones(), 1.0, 0.0).astype(BF16)
    hs = [_split(h_ref[b]) for b in range(nb)]
    ys = [_dot3(_split(rh_ref[b]), hs[b]) + y0_ref[b] for b in range(nb)]
    for b in range(nb):
        h_ref[b] = _dot3((p_ref[b, 0, 0], p_ref[b, 0, 1]), hs[b]) + q_ref[b, 0]
    inv_n = 1.0 / HEAD
    for b in range(nb):
        y = ys[b]
        d = y - _dot_ones(y, head_ones) * inv_n
        var = _dot_ones(d * d, head_ones) * inv_n
        yn = d * lax.rsqrt(var + RW_GN_EPS) * gng_ref[...] + gnb_ref[...]
        o_ref[b] = (yn + bonus_ref[b]) * g_ref[b]


def _rw_scan(p, q, rh, y0, g, bonus, gn_g, gn_b):
    B, T, _ = rh.shape
    C = RW_CHUNK
    nc = T // C
    seq = pl.BlockSpec((B, C, RW_WIDTH), lambda c: (0, c, 0))
    vec = pl.BlockSpec((1, RW_WIDTH), lambda c: (0, 0))
    return pl.pallas_call(
        _rw_scan_body,
        grid=(nc,),
        in_specs=[pl.BlockSpec((B, 1, 2, RW_WIDTH, RW_WIDTH), lambda c: (0, c, 0, 0, 0)),
                  pl.BlockSpec((B, 1, RW_WIDTH, RW_WIDTH), lambda c: (0, c, 0, 0)),
                  seq, seq, seq, seq, vec, vec],
        out_specs=seq,
        out_shape=jax.ShapeDtypeStruct((B, T, RW_WIDTH), F32),
        scratch_shapes=[pltpu.VMEM((B, RW_WIDTH, RW_WIDTH), F32)],
        compiler_params=_cparams(("arbitrary",)),
        name="rwkv_chunk_scan",
    )(p, q, rh, y0, g, bonus, gn_g.reshape(1, -1), gn_b.reshape(1, -1))


def _pool_body(z_ref, zp_ref, w_ref, scale_ref, o_ref):
    i = pl.program_id(1)
    z = z_ref[0]
    tt = z.shape[0]
    halo = jnp.where(i == 0, 0.0, zp_ref[0])
    e = jnp.concatenate([halo, z], axis=0)
    sums = []
    s = e
    for shift in (1, 2, 4, 8):
        s = s + pltpu.roll(s, shift, axis=0)
        sums.append(s[POOL_HALO:, :])
    t_idx = i * tt + lax.broadcasted_iota(jnp.int32, (tt, POOL_WIDTH), 0)
    lane_group = lax.broadcasted_iota(jnp.int32, (tt, POOL_WIDTH), 1) // HEAD
    pooled = jnp.zeros_like(z)
    for gi, win in enumerate(POOL_WINDOWS):
        cnt = jnp.minimum(t_idx + 1, win).astype(F32)
        pooled = jnp.where(lane_group == gi, sums[gi] / cnt - z, pooled)
    o_ref[0] = _dot(pooled.astype(BF16), w_ref[...]) * scale_ref[...]


def _pool(z_pool, w_blockdiag_bf16, scale, tt=512):
    B, T, _ = z_pool.shape
    return pl.pallas_call(
        _pool_body,
        grid=(B, T // tt),
        in_specs=[pl.BlockSpec((1, tt, POOL_WIDTH), lambda b, i: (b, i, 0)),
                  pl.BlockSpec((1, POOL_HALO, POOL_WIDTH),
                               lambda b, i: (b, jnp.maximum(i * (tt // POOL_HALO) - 1, 0), 0)),
                  pl.BlockSpec((POOL_WIDTH, POOL_WIDTH), lambda b, i: (0, 0)),
                  pl.BlockSpec((1, POOL_WIDTH), lambda b, i: (0, 0))],
        out_specs=pl.BlockSpec((1, tt, POOL_WIDTH), lambda b, i: (b, i, 0)),
        out_shape=jax.ShapeDtypeStruct((B, T, POOL_WIDTH), F32),
        compiler_params=_cparams(("parallel", "parallel")),
        name="pool_mixer",
    )(z_pool, z_pool, w_blockdiag_bf16, scale.reshape(1, -1))


def _rope(x, cos_w, sin_w):
    w = x.shape[1]
    lane = lax.broadcasted_iota(jnp.int32, x.shape, 1) % HEAD
    partner = jnp.where(lane < HEAD // 2, pltpu.roll(x, w - HEAD // 2, axis=1), pltpu.roll(x, HEAD // 2, axis=1))
    return x * cos_w + partner * sin_w


def _head_lanes(a, h, dtype):
    part = a[:, h * HEAD:(h + 1) * HEAD]
    return jnp.concatenate([part, jnp.zeros_like(part)], axis=1).astype(dtype)


def _nsa_prep_body(zq_ref, zkc_ref, zks_ref, zvs_ref, zkw_ref, zvw_ref, zg_ref, cos_ref, sin_ref,
                   q_ref, kc_ref, ks_ref, vst_ref, kw_ref, vwt_ref, gt_ref):
    cos = cos_ref[0]
    sin = sin_ref[0]
    cos_q = jnp.concatenate([cos] * (NSA_WIDTH // LANE), axis=1)
    sin_q = jnp.concatenate([sin] * (NSA_WIDTH // LANE), axis=1)
    qr = _rope(zq_ref[0], cos_q, sin_q) * (HEAD ** -0.5 * LOG2E)
    pad_lane = lax.broadcasted_iota(jnp.int32, (qr.shape[0], LANE), 1) == HEAD
    for h in range(NSA_Q_HEADS):
        q_ref[0, h] = jnp.where(pad_lane, NEG, _head_lanes(qr, h, F32)).astype(BF16)
    kc_ref[0] = _rope(zkc_ref[0], cos, sin)
    ks = _rope(zks_ref[0], cos, sin)
    kw = _rope(zkw_ref[0], cos, sin)
    vst = zvs_ref[0].T
    vwt = zvw_ref[0].T
    for h in range(NSA_KV_HEADS):
        sl = slice(h * HEAD, (h + 1) * HEAD)
        ks_ref[0, h] = _head_lanes(ks, h, BF16)
        kw_ref[0, h] = _head_lanes(kw, h, BF16)
        tail = lax.broadcasted_iota(jnp.int32, (V_ROWS - HEAD, vst.shape[1]), 0)
        ones_row = jnp.where(tail == 0, 1.0, 0.0).astype(BF16)
        vst_ref[0, h, 0:HEAD, :] = vst[sl, :].astype(BF16)
        vst_ref[0, h, HEAD:V_ROWS, :] = ones_row
        vwt_ref[0, h, 0:HEAD, :] = vwt[sl, :].astype(BF16)
        vwt_ref[0, h, HEAD:V_ROWS, :] = ones_row
    gt = _sigmoid(zg_ref[0]).T
    for h in range(NSA_KV_HEADS):
        gt_ref[0, h] = gt[16 * h:16 * (h + 1), :]


def _nsa_prep(z_q, z_kc, z_ks, z_vs, z_kw, z_vw, z_g, cos_t, sin_t, tt=512):
    B, T, _ = z_q.shape
    seq = lambda w: pl.BlockSpec((1, tt, w), lambda b, i: (b, i, 0))
    hm = lambda nh: pl.BlockSpec((1, nh, tt, LANE), lambda b, i: (b, 0, i, 0))
    tr = lambda rows: pl.BlockSpec((1, NSA_KV_HEADS, rows, tt), lambda b, i: (b, 0, 0, i))
    sds = jax.ShapeDtypeStruct
    return pl.pallas_call(
        _nsa_prep_body,
        grid=(B, T // tt),
        in_specs=[seq(NSA_WIDTH)] + [seq(NSA_KV)] * 6 + [seq(LANE), seq(LANE)],
        out_specs=(hm(NSA_Q_HEADS), seq(NSA_KV), hm(NSA_KV_HEADS), tr(V_ROWS), hm(NSA_KV_HEADS), tr(V_ROWS), tr(16)),
        out_shape=(sds((B, NSA_Q_HEADS, T, LANE), BF16), sds((B, T, NSA_KV), F32),
                   sds((B, NSA_KV_HEADS, T, LANE), BF16), sds((B, NSA_KV_HEADS, V_ROWS, T), BF16),
                   sds((B, NSA_KV_HEADS, T, LANE), BF16), sds((B, NSA_KV_HEADS, V_ROWS, T), BF16),
                   sds((B, NSA_KV_HEADS, 16, T), F32)),
        compiler_params=_cparams(("parallel", "parallel")),
        name="nsa_prep",
    )(z_q, z_kc, z_ks, z_vs, z_kw, z_vw, z_g, cos_t, sin_t)


def _nsa_cmp_body(gk_ref, gv_ref, pek_ref, pev_ref, wk_ref, wv_ref, kc_ref, vct_ref):
    half = NSA_KV

    def compress(g, pe_ref, w_ref):
        n = g.shape[0]
        lo = _dot((g + pe_ref[0:1, :]).astype(BF16), w_ref[:, 0:half])
        hi = _dot((g + pe_ref[1:2, :]).astype(BF16), w_ref[:, half:2 * half])
        hi_next = pltpu.roll(hi, n - 1, axis=0)
        row = lax.broadcasted_iota(jnp.int32, lo.shape, 0)
        return lo + jnp.where(row == n - 1, 0.0, hi_next)

    kc = compress(gk_ref[0], pek_ref, wk_ref)
    vct = compress(gv_ref[0], pev_ref, wv_ref).T
    for h in range(NSA_KV_HEADS):
        sl = slice(h * HEAD, (h + 1) * HEAD)
        kc_ref[0, h] = _head_lanes(kc, h, BF16)
        vct_ref[0, h] = vct[sl, :].astype(BF16)


def _nsa_compress(kc_groups, vc_groups, pe_k2, pe_v2, wk2, wv2):
    B, ncp, gw = kc_groups.shape
    grp = pl.BlockSpec((1, ncp, gw), lambda b: (b, 0, 0))
    pe = pl.BlockSpec((2, gw), lambda b: (0, 0))
    wspec = pl.BlockSpec((gw, 2 * NSA_KV), lambda b: (0, 0))
    sds = jax.ShapeDtypeStruct
    return pl.pallas_call(
        _nsa_cmp_body,
        grid=(B,),
        in_specs=[grp, grp, pe, pe, wspec, wspec],
        out_specs=(pl.BlockSpec((1, NSA_KV_HEADS, ncp, LANE), lambda b: (b, 0, 0, 0)),
                   pl.BlockSpec((1, NSA_KV_HEADS, HEAD, ncp), lambda b: (b, 0, 0, 0))),
        out_shape=(sds((B, NSA_KV_HEADS, ncp, LANE), BF16), sds((B, NSA_KV_HEADS, HEAD, ncp), BF16)),
        compiler_params=_cparams(("parallel",)),
        name="nsa_compress",
    )(kc_groups, vc_groups, pe_k2, pe_v2, wk2, wv2)


def _nsa_attn_body(q_ref, kc_ref, vct_ref, ks_ref, vst_ref, kw_ref, vwt_ref, gt_ref, mcs_ref, blk_ref, cmpb_ref,
                   winb_ref, o_ref, *score_refs):
    jb = pl.program_id(1)
    s0 = jb * Q_BLOCK
    cols = NSA_GQA * Q_BLOCK
    heads = range(NSA_KV_HEADS)
    q = [q_ref[0, NSA_GQA * h:NSA_GQA * (h + 1)].reshape(cols, LANE) for h in heads]
    ncp = kc_ref.shape[2]
    per_gqa = lambda x: jnp.concatenate([x] * NSA_GQA, axis=1)
    col_max = lambda x: jnp.max(x, axis=0, keepdims=True)

    cmp_mask = per_gqa(cmpb_ref[pl.ds(pl.multiple_of(ncp - s0 // CMP_STRIDE, 8), ncp), :])
    t_c = s0 + lax.broadcasted_iota(jnp.int32, (1, cols), 1) % Q_BLOCK
    any_valid = jnp.where(t_c >= CMP_BLOCK - 1, 1.0, 0.0)
    mcs = mcs_ref[...]
    sc = [_dot_nt(kc_ref[0, h], q[h]) + cmp_mask for h in heads]
    e = [jnp.exp2(x - col_max(x)) for x in sc]
    pc = [x * (any_valid / jnp.sum(x, axis=0, keepdims=True)) for x in e]
    o_cmp = [_dot(vct_ref[0, h], pc[h].astype(BF16)) for h in heads]

    def importance(p):
        psum = p[:, 0:Q_BLOCK]
        for g in range(1, NSA_GQA):
            psum = psum + p[:, g * Q_BLOCK:(g + 1) * Q_BLOCK]
        p_hi = psum.astype(BF16)
        rest = psum - p_hi.astype(F32)
        p_mid = rest.astype(BF16)
        p_lo = (rest - p_mid.astype(F32)).astype(BF16)
        return _dot(mcs, p_hi) + (_dot(mcs, p_mid) + _dot(mcs, p_lo))

    imp = [importance(p) for p in pc]

    wkeys = WINDOW + Q_BLOCK
    off_w = pl.multiple_of(s0, Q_BLOCK)
    tri_old, tri_new = per_gqa(winb_ref[0]), per_gqa(winb_ref[1])
    sw = [_dot_nt(kw_ref[0, h, pl.ds(off_w, wkeys), :], q[h]) for h in heads]
    sw = [jnp.concatenate([x[0:Q_BLOCK] + tri_old, x[Q_BLOCK:WINDOW], x[WINDOW:wkeys] + tri_new], axis=0) for x in sw]
    pw = [jnp.exp2((x - col_max(x)).astype(BF16)) for x in sw]
    acc_w = [_dot(vwt_ref[0, h, :, pl.ds(off_w, wkeys)], pw[h]) for h in heads]

    j_idx = lax.broadcasted_iota(jnp.int32, (NSEL_PAD, Q_BLOCK), 0)
    jc = (s0 + lax.broadcasted_iota(jnp.int32, (NSEL_PAD, Q_BLOCK), 1)) // SEL_BLOCK
    causal = j_idx <= jc
    forced = lambda x: jnp.where(j_idx == 0, FORCE, jnp.where(j_idx == jc, FORCE, jnp.where(j_idx == jc - 1, FORCE, x)))
    work = [jnp.where(causal, forced(x), NEG) for x in imp]
    j_f = j_idx.astype(F32)
    for _ in range(SEL_TOPN):
        top = [col_max(w) for w in work]
        first = [jnp.min(jnp.where(w == t, j_f, float(NSEL_PAD)), axis=0, keepdims=True) for w, t in zip(work, top)]
        work = [jnp.where(j_f == f, -jnp.inf, w) for w, f in zip(work, first)]
    bias = [jnp.where(causal, jnp.where(w == -jnp.inf, 0.0, NEG), NEG) for w in work]

    q_aug = [jnp.concatenate([q[h], jnp.concatenate([bias[h].T.astype(BF16)] * NSA_GQA, axis=0)], axis=1)
             for h in heads]

    def sel_scores(h, off):
        k_aug = jnp.concatenate([ks_ref[0, h, pl.ds(off, KEY_GROUP), :], blk_ref[pl.ds(off, KEY_GROUP), :]], axis=1)
        return _dot_nt(k_aug, q_aug[h])

    def flash_update(v_t, st, carry):
        m, acc = carry
        m_new = jnp.maximum(m, col_max(st))
        p = jnp.exp2((st - m_new).astype(BF16))
        return m_new, acc * jnp.exp2(m - m_new) + _dot(v_t, p)

    init = (jnp.full((1, cols), NEG, F32), jnp.zeros((V_ROWS, cols), F32))
    n_groups = s0 // KEY_GROUP + 1
    n_pairs = n_groups // 2
    t_q = s0 + lax.broadcasted_iota(jnp.int32, (Q_BLOCK, cols), 1) % Q_BLOCK
    r_q = lax.broadcasted_iota(jnp.int32, (Q_BLOCK, cols), 0)

    def put_scores(ref, h, g):
        off = pl.multiple_of(g * KEY_GROUP, KEY_GROUP)
        ref[...] = sel_scores(h, off)
        r = pl.multiple_of(jnp.clip(s0 - off, 0, KEY_GROUP - Q_BLOCK), Q_BLOCK)
        tile = ref[pl.ds(r, Q_BLOCK), :]
        ref[pl.ds(r, Q_BLOCK), :] = jnp.where(off + r + r_q <= t_q, tile, NEG)

    def consume(ref, h, g, carry):
        off = pl.multiple_of(g * KEY_GROUP, KEY_GROUP)
        return flash_update(vst_ref[0, h, :, pl.ds(off, KEY_GROUP)], ref[...], carry)

    last = n_groups - 1
    buf0 = [(score_refs[4 * h], score_refs[4 * h + 1]) for h in heads]
    buf1 = [(score_refs[4 * h + 2], score_refs[4 * h + 3]) for h in heads]
    for h in heads:
        put_scores(buf0[h][0], h, 0)
    for h in heads:
        put_scores(buf0[h][1], h, jnp.minimum(1, last))

    def stage(cur, nxt, pair, carries):
        for h in heads:
            put_scores(nxt[h][0], h, jnp.minimum(2 * pair + 2, last))
        even = [consume(cur[h][0], h, 2 * pair, carries[h][0]) for h in heads]
        for h in heads:
            put_scores(nxt[h][1], h, jnp.minimum(2 * pair + 3, last))
        odd = [consume(cur[h][1], h, 2 * pair + 1, carries[h][1]) for h in heads]
        return tuple((even[h], odd[h]) for h in heads)

    def two_pairs(i, carries):
        return stage(buf1, buf0, 2 * i + 1, stage(buf0, buf1, 2 * i, carries))

    carries = lax.fori_loop(0, n_pairs // 2, two_pairs, tuple((init, init) for _ in heads))
    odd_pairs = n_pairs % 2
    carries = lax.fori_loop(n_pairs - odd_pairs, n_pairs, lambda pr, c: stage(buf0, buf1, pr, c), carries)
    left = n_groups - 2 * n_pairs

    def leftover(buf):
        return lambda _, c: tuple((consume(buf[h][0], h, last, c[h][0]), c[h][1]) for h in heads)

    carries = lax.fori_loop(0, left * (1 - odd_pairs), leftover(buf0), carries)
    carries = lax.fori_loop(0, left * odd_pairs, leftover(buf1), carries)

    for h in heads:
        (m_e, acc_e), (m_o, acc_o) = carries[h]
        m_s = jnp.maximum(m_e, m_o)
        acc_s = acc_e * jnp.exp2(m_e - m_s) + acc_o * jnp.exp2(m_o - m_s)

        def gate_row(branch):
            return jnp.concatenate([gt_ref[0, h, 3 * g + branch:3 * g + branch + 1, :] for g in range(NSA_GQA)], axis=1)

        o_t = (gate_row(0) * o_cmp[h] + (gate_row(1) / acc_s[HEAD:HEAD + 1]) * acc_s[0:HEAD]
               + (gate_row(2) / acc_w[h][HEAD:HEAD + 1]) * acc_w[h][0:HEAD])
        stacked = jnp.concatenate([o_t[:, g * Q_BLOCK:(g + 1) * Q_BLOCK] for g in range(NSA_GQA)], axis=0)
        o_ref[0, :, NSA_GQA * HEAD * h:NSA_GQA * HEAD * (h + 1)] = stacked.T


def _nsa_masks(seq_len):
    tq = np.arange(Q_BLOCK)[None, :]
    ncp = seq_len // CMP_STRIDE
    n_rel = np.arange(2 * ncp)[:, None] - ncp
    cmp_b = np.where(CMP_STRIDE * n_rel + CMP_BLOCK - 1 <= tq, 0.0, NEG).astype(np.float32)
    k_rel = np.arange(Q_BLOCK)[:, None]
    win_b = np.stack([np.where(k_rel > tq, 0.0, NEG), np.where(k_rel <= tq, 0.0, NEG)]).astype(np.float32)
    return jnp.asarray(cmp_b), jnp.asarray(win_b)


def _nsa_attention(q_r, kcmp, vcmp_t, ks_r, vs_t, kw_pad, vw_pad_t, g_t, mcs_t, blk_onehot, cmp_b, win_b):
    B, _, T, _ = q_r.shape
    ncp = kcmp.shape[2]
    nqb = T // Q_BLOCK
    tw = T + WINDOW
    nkv = NSA_KV_HEADS
    assert T % KEY_GROUP == 0 and T // SEL_BLOCK <= NSEL_PAD and WINDOW > Q_BLOCK and WINDOW % Q_BLOCK == 0
    kv_rows = lambda n: pl.BlockSpec((1, nkv, n, LANE), lambda b, j: (b, 0, 0, 0))
    kv_cols = lambda n, rows: pl.BlockSpec((1, nkv, rows, n), lambda b, j: (b, 0, 0, 0))
    const = lambda shp: pl.BlockSpec(shp, lambda b, j: (0,) * len(shp))
    return pl.pallas_call(
        _nsa_attn_body,
        grid=(B, nqb),
        in_specs=[pl.BlockSpec((1, NSA_Q_HEADS, Q_BLOCK, LANE), lambda b, j: (b, 0, j, 0)),
                  kv_rows(ncp), kv_cols(ncp, HEAD), kv_rows(T), kv_cols(T, V_ROWS), kv_rows(tw), kv_cols(tw, V_ROWS),
                  pl.BlockSpec((1, nkv, 16, Q_BLOCK), lambda b, j: (b, 0, 0, j)),
                  const((NSEL_PAD, ncp)), const((T, NSEL_PAD)), const((2 * ncp, Q_BLOCK)),
                  const((2, Q_BLOCK, Q_BLOCK))],
        out_specs=pl.BlockSpec((1, Q_BLOCK, NSA_WIDTH), lambda b, j: (b, j, 0)),
        out_shape=jax.ShapeDtypeStruct((B, T, NSA_WIDTH), F32),
        scratch_shapes=[pltpu.VMEM((KEY_GROUP, NSA_GQA * Q_BLOCK), F32)] * (4 * nkv),
        compiler_params=_cparams(("parallel", "arbitrary")),
        name="nsa_attention",
    )(q_r, kcmp, vcmp_t, ks_r, vs_t, kw_pad, vw_pad_t, g_t, mcs_t, blk_onehot, cmp_b, win_b)


def _gelu_tanh(x):
    return 0.5 * x * (1.0 + jnp.tanh(math.sqrt(2.0 / math.pi) * (x + 0.044715 * (x * x * x))))


def _tail_body(x_ref, ya_ref, yb_ref, yc_ref, p_ref, wa_ref, wb_ref, wc_ref, gffn_ref, wup_ref, cw_ref, cb_ref,
               wdn_ref, gple_ref, wg_ref, wp_ref, gfin_ref, o_ref, carry_ref, *, tiles_per_seq, f_chunk, final):
    i = pl.program_id(0)
    mix = _dot(ya_ref[...].astype(BF16), wa_ref[...])
    mix = mix + _dot(yb_ref[...].astype(BF16), wb_ref[...])
    mix = mix + _dot(yc_ref[...].astype(BF16), wc_ref[...])
    x = x_ref[...] + mix

    tm = x.shape[0]
    d_ff = wdn_ref.shape[0]
    h = _rms(x, gffn_ref[...]).astype(BF16)
    first = (i % tiles_per_seq) == 0
    row = lax.broadcasted_iota(jnp.int32, (tm, f_chunk), 0)

    def conv(col0):
        u = _dot(h, wup_ref[:, col0:col0 + f_chunk])
        p2 = jnp.where(first, 0.0, carry_ref[6:7, col0:col0 + f_chunk])
        p1 = jnp.where(first, 0.0, carry_ref[7:8, col0:col0 + f_chunk])
        u1 = jnp.where(row == 0, p1, pltpu.roll(u, 1, axis=0))
        u2 = jnp.where(row == 0, p2, jnp.where(row == 1, p1, pltpu.roll(u, 2, axis=0)))
        carry_ref[:, col0:col0 + f_chunk] = u[tm - 8:tm, :]
        cw = cw_ref[:, col0:col0 + f_chunk]
        return cw[0:1] * u2 + cw[1:2] * u1 + cw[2:3] * u + cb_ref[:, col0:col0 + f_chunk]

    for c0 in range(0, d_ff, f_chunk):
        act = _gelu_tanh(conv(c0)) * conv(d_ff + c0)
        x = x + _dot(act.astype(BF16), wdn_ref[c0:c0 + f_chunk, :])

    gate = _sigmoid(_dot(_rms(x, gple_ref[...]).astype(BF16), wg_ref[...]))
    y = x + _dot(p_ref[...].astype(BF16), wp_ref[...]) * gate
    o_ref[...] = _rms(y, gfin_ref[...]) if final else y


def _layer_tail(x2, ya, yb, yc, p2, w_out, g_ffn, w_up, conv_w, conv_b, w_down, g_ple, w_gate, w_proj, g_final,
                seq_len, final, tm=512, f_chunk=1408):
    m, d = x2.shape
    f2 = w_up.shape[1]
    d_ff = f2 // 2
    pd = p2.shape[1]
    assert d_ff % f_chunk == 0 and f_chunk % LANE == 0 and seq_len % tm == 0
    bf = lambda w: w.astype(BF16)
    wa, wb, wc = bf(w_out[:RW_WIDTH]), bf(w_out[RW_WIDTH:RW_WIDTH + POOL_WIDTH]), bf(w_out[RW_WIDTH + POOL_WIDTH:])
    tile = lambda w: pl.BlockSpec((tm, w), lambda i: (i, 0))
    full = lambda shp: pl.BlockSpec(shp, lambda i: (0,) * len(shp))
    return pl.pallas_call(
        functools.partial(_tail_body, tiles_per_seq=seq_len // tm, f_chunk=f_chunk, final=final),
        grid=(m // tm,),
        in_specs=[tile(d), tile(RW_WIDTH), tile(POOL_WIDTH), tile(NSA_WIDTH), tile(pd),
                  full((RW_WIDTH, d)), full((POOL_WIDTH, d)), full((NSA_WIDTH, d)),
                  full((1, d)), full((d, f2)), full((CONV_W, f2)), full((1, f2)), full((d_ff, d)),
                  full((1, d)), full((d, d)), full((pd, d)), full((1, d))],
        out_specs=tile(d),
        out_shape=jax.ShapeDtypeStruct((m, d), F32),
        scratch_shapes=[pltpu.VMEM((8, f2), F32)],
        compiler_params=_cparams(("arbitrary",)),
        name="layer_tail_final" if final else "layer_tail",
    )(x2, ya, yb, yc, p2, wa, wb, wc, g_ffn.reshape(1, d), bf(w_up), conv_w.reshape(CONV_W, f2),
      conv_b.reshape(1, f2), bf(w_down), g_ple.reshape(1, d), bf(w_gate), bf(w_proj), g_final.reshape(1, d))


def _overlap_matrix(seq_len):
    ncp = seq_len // CMP_STRIDE
    n_cmp = (seq_len - CMP_BLOCK) // CMP_STRIDE + 1
    cs = CMP_STRIDE * np.arange(n_cmp)
    ss = SEL_BLOCK * np.arange(seq_len // SEL_BLOCK)
    ov = (np.minimum(cs[:, None] + CMP_BLOCK - 1, ss[None] + SEL_BLOCK - 1) - np.maximum(cs[:, None], ss[None]) + 1)
    m = np.zeros((NSEL_PAD, ncp), np.float32)
    m[:ss.size, :n_cmp] = (np.clip(ov, 0, CMP_BLOCK).astype(np.float32) / CMP_BLOCK).T
    return jnp.asarray(m, dtype=BF16)


def _compress_weight(w):
    halves = w.reshape(2, CMP_STRIDE, HEAD, HEAD)
    eye = jnp.eye(NSA_KV_HEADS, dtype=w.dtype)
    w2 = jnp.einsum('sldf,hg->lhdsgf', halves, eye)
    return w2.reshape(CMP_STRIDE * NSA_KV, 2 * NSA_KV).astype(BF16)


def _compress_pe(pe):
    halves = pe.reshape(2, CMP_STRIDE, 1, HEAD)
    return jnp.broadcast_to(halves, (2, CMP_STRIDE, NSA_KV_HEADS, HEAD)).reshape(2, CMP_STRIDE * NSA_KV)


def _in_proj_layout(w_in):
    d = w_in.shape[0]
    gate0 = RW_IN + POOL_WIDTH + NSA_WIDTH + 6 * NSA_KV
    per_head = NSA_GQA * 3
    gates = jnp.zeros((d, LANE), w_in.dtype)
    for h in range(NSA_KV_HEADS):
        gates = gates.at[:, 16 * h:16 * h + per_head].set(w_in[:, gate0 + per_head * h:gate0 + per_head * (h + 1)])
    return jnp.concatenate([w_in[:, :gate0], gates], axis=1).astype(BF16)


_IN_SPLITS = ((0, RW_IN), (RW_IN, RW_IN + POOL_WIDTH), (RW_IN + POOL_WIDTH, RW_IN + POOL_WIDTH + NSA_WIDTH)) + tuple(
    (RW_IN + POOL_WIDTH + NSA_WIDTH + i * NSA_KV, RW_IN + POOL_WIDTH + NSA_WIDTH + (i + 1) * NSA_KV) for i in range(7))


def kernel(x, p, positions, g_mix, w_in, rw_mu, rw_w0, rw_w_up, rw_a0, rw_a_up, rw_g_up, rw_k_k, rw_k_a, rw_r_k, rw_gn_g, rw_gn_b, pool_w, pool_scale, nsa_pe_k, nsa_pe_v, nsa_w_ck, nsa_w_cv, w_out, g_ffn, ffn_w_up, ffn_conv_w, ffn_conv_b, ffn_w_down, g_ple, ple_w_gate, ple_w_proj, g_final):
    B, T, D = x.shape
    depth = w_in.shape[0]
    M = B * T

    half = HEAD // 2
    inv = ROPE_THETA ** (-jnp.arange(half, dtype=F32) / half)
    ang = positions.astype(F32)[..., None] * inv
    cos, sin = jnp.cos(ang), jnp.sin(ang)
    cos_t = jnp.concatenate([cos, cos] * (LANE // HEAD), axis=-1)
    sin_t = jnp.concatenate([-sin, sin] * (LANE // HEAD), axis=-1)
    mcs_t = _overlap_matrix(T)
    blk_onehot = (jnp.arange(T)[:, None] // SEL_BLOCK == jnp.arange(NSEL_PAD)[None, :]).astype(BF16)
    cmp_b, win_b = _nsa_masks(T)
    key_pad = (jnp.arange(LANE) == HEAD).astype(BF16)

    x2 = x.reshape(M, D)
    for i in range(depth):
        zs = _norm_matmul(x2, g_mix[i], _in_proj_layout(w_in[i]), _IN_SPLITS)
        z_rw, z_pool, z_q, z_kc, z_vc, z_ks, z_vs, z_kw, z_vw, z_g = (
            z.reshape(B, T, z.shape[1]) for z in zs)

        pm, qm, rh, y0, gate, bonus = _rw_chunks(z_rw, rw_mu[i], rw_w0[i], rw_w_up[i], rw_a0[i], rw_a_up[i],
                                                 rw_g_up[i], rw_k_k[i], rw_k_a[i], rw_r_k[i].reshape(-1))
        y_a = _rw_scan(pm, qm, rh, y0, gate, bonus, rw_gn_g[i], rw_gn_b[i])

        w_pool_bd = jax.scipy.linalg.block_diag(*[pool_w[i, gi] for gi in range(pool_w.shape[1])]).astype(BF16)
        y_b = _pool(z_pool, w_pool_bd, pool_scale[i])

        q_r, kc_r, ks_r, vs_t, kw_r, vw_t, g_t = _nsa_prep(z_q, z_kc, z_ks, z_vs, z_kw, z_vw, z_g, cos_t, sin_t)
        grp = (B, T // CMP_STRIDE, CMP_STRIDE * NSA_KV)
        kcmp, vcmp_t = _nsa_compress(kc_r.reshape(grp), z_vc.reshape(grp),
                                     _compress_pe(nsa_pe_k[i]), _compress_pe(nsa_pe_v[i]),
                                     _compress_weight(nsa_w_ck[i]), _compress_weight(nsa_w_cv[i]))
        kw_pad = jnp.concatenate([jnp.broadcast_to(key_pad, (B, NSA_KV_HEADS, WINDOW, LANE)), kw_r], axis=2)
        vw_pad_t = jnp.pad(vw_t, ((0, 0), (0, 0), (0, 0), (WINDOW, 0)))
        y_c = _nsa_attention(q_r, kcmp, vcmp_t, ks_r, vs_t, kw_pad, vw_pad_t, g_t, mcs_t, blk_onehot, cmp_b, win_b)

        x2 = _layer_tail(x2, y_a.reshape(M, -1), y_b.reshape(M, -1), y_c.reshape(M, -1), p[i].reshape(M, -1),
                         w_out[i], g_ffn[i], ffn_w_up[i], ffn_conv_w[i], ffn_conv_b[i], ffn_w_down[i],
                         g_ple[i], ple_w_gate[i], ple_w_proj[i], g_final, T, final=(i == depth - 1))
    return x2.reshape(B, T, D)
```

```python
import functools
import math

import numpy as np
import jax
import jax.numpy as jnp
from jax import lax
from jax.experimental import pallas as pl
from jax.experimental.pallas import tpu as pltpu

F32 = jnp.float32
BF16 = jnp.bfloat16

RMS_EPS = 1e-6
ROPE_THETA = 10000.0
HEAD = 64
RW_HEADS = 4
RW_WIDTH = RW_HEADS * HEAD
RW_DECAY_LORA = 64
RW_AAA_LORA = 64
RW_GATE_LORA = 128
RW_GN_EPS = 64e-5
RW_IN = 3 * RW_WIDTH + RW_DECAY_LORA + RW_AAA_LORA + RW_GATE_LORA
POOL_WIDTH = 256
POOL_WINDOWS = (2, 4, 8, 16)
POOL_HALO = 16
NSA_Q_HEADS = 8
NSA_KV_HEADS = 2
NSA_GQA = NSA_Q_HEADS // NSA_KV_HEADS
NSA_WIDTH = NSA_Q_HEADS * HEAD
NSA_KV = NSA_KV_HEADS * HEAD
CMP_BLOCK = 32
CMP_STRIDE = 16
SEL_BLOCK = 64
SEL_TOPN = 16
WINDOW = 512
Q_BLOCK = 128
KEY_GROUP = 512
V_ROWS = HEAD + 16
NSEL_PAD = 128
NEG = -1e30
LOG2E = math.log2(math.e)
FORCE = 1e9
CONV_W = 3
LANE = 128
VMEM_LIMIT = 56 * 1024 * 1024

RW_CHUNK = 64


def _cparams(sem):
    return pltpu.CompilerParams(dimension_semantics=sem, vmem_limit_bytes=VMEM_LIMIT)


def _dot(a, b, prec=None):
    return lax.dot_general(a, b, (((1,), (0,)), ((), ())), precision=prec, preferred_element_type=F32)


def _dot_nt(a, b, prec=None):
    return lax.dot_general(a, b, (((1,), (1,)), ((), ())), precision=prec, preferred_element_type=F32)


def _sigmoid(x):
    return 1.0 / (1.0 + jnp.exp(-x))


def _rms(x, g):
    ms = jnp.mean(x * x, axis=-1, keepdims=True)
    return (x * lax.rsqrt(ms + RMS_EPS)) * g


def _norm_mm_body(x_ref, g_ref, w_ref, *o_refs, splits):
    h = _rms(x_ref[...], g_ref[...]).astype(BF16)
    for o_ref, (lo, hi) in zip(o_refs, splits):
        o_ref[...] = _dot(h, w_ref[:, lo:hi])


def _norm_matmul(x2, g, w_bf16, splits, tm=1024):
    m, d = x2.shape
    n = w_bf16.shape[1]
    outs = tuple(jax.ShapeDtypeStruct((m, hi - lo), F32) for lo, hi in splits)
    return pl.pallas_call(
        functools.partial(_norm_mm_body, splits=splits),
        grid=(m // tm,),
        in_specs=[pl.BlockSpec((tm, d), lambda i: (i, 0)),
                  pl.BlockSpec((1, d), lambda i: (0, 0)),
                  pl.BlockSpec((d, n), lambda i: (0, 0))],
        out_specs=tuple(pl.BlockSpec((tm, hi - lo), lambda i: (i, 0)) for lo, hi in splits),
        out_shape=outs,
        compiler_params=_cparams(("parallel",)),
        name="norm_in_proj",
    )(x2, g.reshape(1, d), w_bf16)


def _split(x):
    hi = x.astype(BF16)
    return hi, (x - hi.astype(F32)).astype(BF16)


def _dot3(a, b, dot=_dot):
    return dot(a[0], b[0]) + (dot(a[0], b[1]) + dot(a[1], b[0]))


def _dot_ones(x, ones_bf16):
    hi, lo = _split(x)
    return _dot(hi, ones_bf16) + _dot(lo, ones_bf16)


def _head_ones():
    hr = lax.broadcasted_iota(jnp.int32, (RW_WIDTH, RW_WIDTH), 0) // HEAD
    hc = lax.broadcasted_iota(jnp.int32, (RW_WIDTH, RW_WIDTH), 1) // HEAD
    return hr == hc


def _rw_chunk_body(z_ref, zp_ref, mu_ref, w0_ref, wup_ref, a0_ref, aup_ref, gup_ref, kk_ref, ka_ref,
                   rk_ref, p_ref, q_ref, rh_ref, y0_ref, g_ref, bonus_ref):
    i = pl.program_id(1)
    C = RW_CHUNK
    W = RW_WIDTH
    z = z_ref[0]
    rows = z.shape[0]
    n_chunks = rows // C
    prev = jnp.where(i == 0, 0.0, zp_ref[0, 7:8, :])
    row_z = lax.broadcasted_iota(jnp.int32, z.shape, 0)
    zs = jnp.where(row_z == 0, prev, pltpu.roll(z, 1, axis=0))
    zf = z + (zs - z) * mu_ref[...]
    r = zf[:, 0:W]
    k = zf[:, W:2 * W]
    v = zf[:, 2 * W:3 * W]
    zw = zf[:, 3 * W:3 * W + RW_DECAY_LORA]
    za = zf[:, 3 * W + RW_DECAY_LORA:3 * W + RW_DECAY_LORA + RW_AAA_LORA]
    zg = zf[:, 3 * W + RW_DECAY_LORA + RW_AAA_LORA:]

    lora = lambda x, w_ref: _dot3(_split(x), (w_ref[0], w_ref[1]))
    xw = -(w0_ref[...] + lora(jnp.tanh(zw), wup_ref))
    softplus = jnp.maximum(xw, 0.0) + jnp.log(1.0 + jnp.exp(-jnp.abs(xw)))
    ld = -jnp.exp(-softplus - 0.5)
    a = _sigmoid(a0_ref[...] + lora(za, aup_ref))
    g_ref[0] = lora(_sigmoid(zg), gup_ref)
    kk = k * kk_ref[...]
    k2 = k * (1.0 + (a - 1.0) * ka_ref[...])

    same_head = _head_ones()
    head_ones = jnp.where(same_head, 1.0, 0.0).astype(BF16)
    kkn = kk / jnp.maximum(jnp.sqrt(_dot_ones(kk * kk, head_ones)), 1e-12)
    bonus_ref[0] = _dot_ones(r * k2 * rk_ref[...], head_ones) * v

    t_in = lax.broadcasted_iota(jnp.int32, (rows, W), 0) % C
    lc = ld
    shift = 1
    while shift < C:
        lc = lc + jnp.where(t_in >= shift, pltpu.roll(lc, shift, axis=0), 0.0)
        shift *= 2

    tr = lax.broadcasted_iota(jnp.int32, (W, W), 0) % C
    tc = lax.broadcasted_iota(jnp.int32, (W, W), 1) % C
    lower_incl = tr >= tc
    lower_strict = tr > tc
    eye = jnp.where(same_head, jnp.where(tr == tc, 1.0, 0.0), 0.0)

    def embed(x):
        return jnp.where(same_head, jnp.concatenate([x] * RW_HEADS, axis=0), 0.0)

    bf = lambda x: x.astype(BF16)
    A, Bt, Kt, Rt, V, BhT, KhT, gdiag, Rt_raw = [], [], [], [], [], [], [], [], []
    for c in range(n_chunks):
        rs = slice(c * C, (c + 1) * C)
        lc_c, ld_c = lc[rs], ld[rs]
        lc_end = lc_c[C - 1:C, :]
        ginv = jnp.exp(-lc_c)
        dec_end = jnp.exp(lc_end - lc_c)
        kkn_c, a_c, k2_c = kkn[rs], a[rs], k2[rs]
        r_t = r[rs] * jnp.exp(lc_c)
        A.append(bf(embed(-kkn_c * jnp.exp(lc_c - ld_c))))
        Bt.append(bf(embed(kkn_c * a_c * ginv)))
        Kt.append(bf(embed(k2_c * ginv)))
        Rt.append(bf(embed(r_t)))
        Rt_raw.append(r_t)
        V.append(bf(embed(v[rs])))
        BhT.append(bf(embed(kkn_c * a_c * dec_end).T))
        KhT.append(bf(embed(k2_c * dec_end).T))
        gdiag.append(jnp.where(eye > 0.5, jnp.broadcast_to(jnp.exp(lc_end), (W, W)), 0.0))

    each = lambda f, *lists: [f(*xs) for xs in zip(*lists)]
    L = each(lambda x, y: jnp.where(lower_strict, _dot_nt(x, y), 0.0), A, Bt)
    Lak = each(lambda x, y: bf(jnp.where(lower_strict, _dot_nt(x, y), 0.0)), A, Kt)
    Mrb = each(lambda x, y: bf(jnp.where(lower_incl, _dot_nt(x, y), 0.0)), Rt, Bt)
    Mrk = each(lambda x, y: bf(jnp.where(lower_incl, _dot_nt(x, y), 0.0)), Rt, Kt)
    Tm = [eye + x for x in L]
    Pw = each(bf, L)
    for _ in range(int(math.log2(C)) - 1):
        Pw = each(lambda x: bf(_dot(x, x)), Pw)
        Tm = each(lambda t, x: t + _dot(bf(t), x), Tm, Pw)
    Tm = each(bf, Tm)
    Wm = each(lambda t, x: bf(_dot(t, x)), Tm, A)
    LakV = each(lambda x, y: bf(_dot(x, y)), Lak, V)
    U0 = each(lambda t, x: bf(_dot(t, x)), Tm, LakV)
    Pm = each(lambda d, x, y: d + _dot(x, y), gdiag, BhT, Wm)
    Qm = each(lambda x, y, s, t: _dot(x, y) + _dot(s, t), BhT, U0, KhT, V)
    Rh = each(_dot, Mrb, Wm)
    Y0 = each(lambda x, y, s, t: _dot(x, y) + _dot(s, t), Mrb, U0, Mrk, V)

    def flatten(x_bd):
        out = x_bd[0:C]
        for h in range(1, RW_HEADS):
            out = out + x_bd[h * C:(h + 1) * C]
        return out

    for c in range(n_chunks):
        rs = slice(c * C, (c + 1) * C)
        p_hi, p_lo = _split(Pm[c])
        p_ref[0, c, 0] = p_hi
        p_ref[0, c, 1] = p_lo
        q_ref[0, c] = Qm[c]
        rh_ref[0, rs, :] = Rt_raw[c] + flatten(Rh[c])
        y0_ref[0, rs, :] = flatten(Y0[c])


def _rw_chunks(z_rw, mu, w0, w_up, a0, a_up, g_up, k_k, k_a, r_k, chunks_per_step=8):
    B, T, _ = z_rw.shape
    C = RW_CHUNK
    assert C == HEAD, "the chunk build shares one index grid between time and head-dim masks"
    rows = C * chunks_per_step
    nc = T // C
    row = lambda a: a.reshape(1, -1)
    pair = lambda w: jnp.stack(_split(w))
    full = lambda shp: pl.BlockSpec(shp, lambda b, i: (0,) * len(shp))
    seq = lambda w: pl.BlockSpec((1, rows, w), lambda b, i: (b, i, 0))
    sds = jax.ShapeDtypeStruct
    seq_shape = sds((B, T, RW_WIDTH), F32)
    return pl.pallas_call(
        _rw_chunk_body,
        grid=(B, T // rows),
        in_specs=[seq(RW_IN),
                  pl.BlockSpec((1, 8, RW_IN), lambda b, i: (b, jnp.maximum(i * (rows // 8) - 1, 0), 0)),
                  full((1, RW_IN)), full((1, RW_WIDTH)), full((2, RW_DECAY_LORA, RW_WIDTH)),
                  full((1, RW_WIDTH)), full((2, RW_AAA_LORA, RW_WIDTH)), full((2, RW_GATE_LORA, RW_WIDTH)),
                  full((1, RW_WIDTH)), full((1, RW_WIDTH)), full((1, RW_WIDTH))],
        out_specs=(pl.BlockSpec((1, chunks_per_step, 2, RW_WIDTH, RW_WIDTH), lambda b, i: (b, i, 0, 0, 0)),
                   pl.BlockSpec((1, chunks_per_step, RW_WIDTH, RW_WIDTH), lambda b, i: (b, i, 0, 0)),
                   seq(RW_WIDTH), seq(RW_WIDTH), seq(RW_WIDTH), seq(RW_WIDTH)),
        out_shape=(sds((B, nc, 2, RW_WIDTH, RW_WIDTH), BF16), sds((B, nc, RW_WIDTH, RW_WIDTH), F32),
                   seq_shape, seq_shape, seq_shape, seq_shape),
        compiler_params=_cparams(("parallel", "parallel")),
        name="rwkv_chunk_build",
    )(z_rw, z_rw, row(mu), row(w0), pair(w_up), row(a0), pair(a_up), pair(g_up), row(k_k), row(k_a), row(r_k))


def _rw_scan_body(p_ref, q_ref, rh_ref, y0_ref, g_ref, bonus_ref, gng_ref, gnb_ref, o_ref, h_ref):
    @pl.when(pl.program_id(0) == 0)
    def _():
        h_ref[...] = jnp.zeros_like(h_ref)

    nb = h_ref.shape[0]
    head_ones = jnp.where(_head_ones(), 1.0, 0.0).astype(BF16)
    hs = [_split(h_ref[b]) for b in range(nb)]
    ys = [_dot3(_split(rh_ref[b]), hs[b]) + y0_ref[b] for b in range(nb)]
    for b in range(nb):
        h_ref[b] = _dot3((p_ref[b, 0, 0], p_ref[b, 0, 1]), hs[b]) + q_ref[b, 0]
    inv_n = 1.0 / HEAD
    for b in range(nb):
        y = ys[b]
        d = y - _dot_ones(y, head_ones) * inv_n
        var = _dot_ones(d * d, head_ones) * inv_n
        yn = d * lax.rsqrt(var + RW_GN_EPS) * gng_ref[...] + gnb_ref[...]
        o_ref[b] = (yn + bonus_ref[b]) * g_ref[b]


def _rw_scan(p, q, rh, y0, g, bonus, gn_g, gn_b):
    B, T, _ = rh.shape
    C = RW_CHUNK
    nc = T // C
    seq = pl.BlockSpec((B, C, RW_WIDTH), lambda c: (0, c, 0))
    vec = pl.BlockSpec((1, RW_WIDTH), lambda c: (0, 0))
    return pl.pallas_call(
        _rw_scan_body,
        grid=(nc,),
        in_specs=[pl.BlockSpec((B, 1, 2, RW_WIDTH, RW_WIDTH), lambda c: (0, c, 0, 0, 0)),
                  pl.BlockSpec((B, 1, RW_WIDTH, RW_WIDTH), lambda c: (0, c, 0, 0)),
                  seq, seq, seq, seq, vec, vec],
        out_specs=seq,
        out_shape=jax.ShapeDtypeStruct((B, T, RW_WIDTH), F32),
        scratch_shapes=[pltpu.VMEM((B, RW_WIDTH, RW_WIDTH), F32)],
        compiler_params=_cparams(("arbitrary",)),
        name="rwkv_chunk_scan",
    )(p, q, rh, y0, g, bonus, gn_g.reshape(1, -1), gn_b.reshape(1, -1))


def _pool_body(z_ref, zp_ref, w_ref, scale_ref, o_ref):
    i = pl.program_id(1)
    z = z_ref[0]
    tt = z.shape[0]
    halo = jnp.where(i == 0, 0.0, zp_ref[0])
    e = jnp.concatenate([halo, z], axis=0)
    sums = []
    s = e
    for shift in (1, 2, 4, 8):
        s = s + pltpu.roll(s, shift, axis=0)
        sums.append(s[POOL_HALO:, :])
    t_idx = i * tt + lax.broadcasted_iota(jnp.int32, (tt, POOL_WIDTH), 0)
    lane_group = lax.broadcasted_iota(jnp.int32, (tt, POOL_WIDTH), 1) // HEAD
    pooled = jnp.zeros_like(z)
    for gi, win in enumerate(POOL_WINDOWS):
        cnt = jnp.minimum(t_idx + 1, win).astype(F32)
        pooled = jnp.where(lane_group == gi, sums[gi] / cnt - z, pooled)
    o_ref[0] = _dot(pooled.astype(BF16), w_ref[...]) * scale_ref[...]


def _pool(z_pool, w_blockdiag_bf16, scale, tt=512):
    B, T, _ = z_pool.shape
    return pl.pallas_call(
        _pool_body,
        grid=(B, T // tt),
        in_specs=[pl.BlockSpec((1, tt, POOL_WIDTH), lambda b, i: (b, i, 0)),
                  pl.BlockSpec((1, POOL_HALO, POOL_WIDTH),
                               lambda b, i: (b, jnp.maximum(i * (tt // POOL_HALO) - 1, 0), 0)),
                  pl.BlockSpec((POOL_WIDTH, POOL_WIDTH), lambda b, i: (0, 0)),
                  pl.BlockSpec((1, POOL_WIDTH), lambda b, i: (0, 0))],
        out_specs=pl.BlockSpec((1, tt, POOL_WIDTH), lambda b, i: (b, i, 0)),
        out_shape=jax.ShapeDtypeStruct((B, T, POOL_WIDTH), F32),
        compiler_params=_cparams(("parallel", "parallel")),
        name="pool_mixer",
    )(z_pool, z_pool, w_blockdiag_bf16, scale.reshape(1, -1))


def _rope(x, cos_w, sin_w):
    w = x.shape[1]
    lane = lax.broadcasted_iota(jnp.int32, x.shape, 1) % HEAD
    partner = jnp.where(lane < HEAD // 2, pltpu.roll(x, w - HEAD // 2, axis=1), pltpu.roll(x, HEAD // 2, axis=1))
    return x * cos_w + partner * sin_w


def _head_lanes(a, h, dtype):
    part = a[:, h * HEAD:(h + 1) * HEAD]
    return jnp.concatenate([part, jnp.zeros_like(part)], axis=1).astype(dtype)


def _nsa_prep_body(zq_ref, zkc_ref, zks_ref, zvs_ref, zkw_ref, zvw_ref, zg_ref, cos_ref, sin_ref,
                   q_ref, kc_ref, ks_ref, vst_ref, kw_ref, vwt_ref, gt_ref):
    cos = cos_ref[0]
    sin = sin_ref[0]
    cos_q = jnp.concatenate([cos] * (NSA_WIDTH // LANE), axis=1)
    sin_q = jnp.concatenate([sin] * (NSA_WIDTH // LANE), axis=1)
    qr = _rope(zq_ref[0], cos_q, sin_q) * (HEAD ** -0.5 * LOG2E)
    pad_lane = lax.broadcasted_iota(jnp.int32, (qr.shape[0], LANE), 1) == HEAD
    for h in range(NSA_Q_HEADS):
        q_ref[0, h] = jnp.where(pad_lane, NEG, _head_lanes(qr, h, F32)).astype(BF16)
    kc_ref[0] = _rope(zkc_ref[0], cos, sin)
    ks = _rope(zks_ref[0], cos, sin)
    kw = _rope(zkw_ref[0], cos, sin)
    vst = zvs_ref[0].T
    vwt = zvw_ref[0].T
    for h in range(NSA_KV_HEADS):
        sl = slice(h * HEAD, (h + 1) * HEAD)
        ks_ref[0, h] = _head_lanes(ks, h, BF16)
        kw_ref[0, h] = _head_lanes(kw, h, BF16)
        tail = lax.broadcasted_iota(jnp.int32, (V_ROWS - HEAD, vst.shape[1]), 0)
        ones_row = jnp.where(tail == 0, 1.0, 0.0).astype(BF16)
        vst_ref[0, h, 0:HEAD, :] = vst[sl, :].astype(BF16)
        vst_ref[0, h, HEAD:V_ROWS, :] = ones_row
        vwt_ref[0, h, 0:HEAD, :] = vwt[sl, :].astype(BF16)
        vwt_ref[0, h, HEAD:V_ROWS, :] = ones_row
    gt = _sigmoid(zg_ref[0]).T
    for h in range(NSA_KV_HEADS):
        gt_ref[0, h] = gt[16 * h:16 * (h + 1), :]


def _nsa_prep(z_q, z_kc, z_ks, z_vs, z_kw, z_vw, z_g, cos_t, sin_t, tt=512):
    B, T, _ = z_q.shape
    seq = lambda w: pl.BlockSpec((1, tt, w), lambda b, i: (b, i, 0))
    hm = lambda nh: pl.BlockSpec((1, nh, tt, LANE), lambda b, i: (b, 0, i, 0))
    tr = lambda rows: pl.BlockSpec((1, NSA_KV_HEADS, rows, tt), lambda b, i: (b, 0, 0, i))
    sds = jax.ShapeDtypeStruct
    return pl.pallas_call(
        _nsa_prep_body,
        grid=(B, T // tt),
        in_specs=[seq(NSA_WIDTH)] + [seq(NSA_KV)] * 6 + [seq(LANE), seq(LANE)],
        out_specs=(hm(NSA_Q_HEADS), seq(NSA_KV), hm(NSA_KV_HEADS), tr(V_ROWS), hm(NSA_KV_HEADS), tr(V_ROWS), tr(16)),
        out_shape=(sds((B, NSA_Q_HEADS, T, LANE), BF16), sds((B, T, NSA_KV), F32),
                   sds((B, NSA_KV_HEADS, T, LANE), BF16), sds((B, NSA_KV_HEADS, V_ROWS, T), BF16),
                   sds((B, NSA_KV_HEADS, T, LANE), BF16), sds((B, NSA_KV_HEADS, V_ROWS, T), BF16),
                   sds((B, NSA_KV_HEADS, 16, T), F32)),
        compiler_params=_cparams(("parallel", "parallel")),
        name="nsa_prep",
    )(z_q, z_kc, z_ks, z_vs, z_kw, z_vw, z_g, cos_t, sin_t)


def _nsa_cmp_body(xk_ref, xv_ref, pek_ref, pev_ref, wk_ref, wv_ref, kc_ref, vct_ref):
    half = NSA_KV
    n = kc_ref.shape[2]

    def compress(x_ref, pe_ref, w_ref):
        lo = hi = None
        for l in range(CMP_STRIDE):
            x = x_ref[0, pl.ds(l, n, stride=CMP_STRIDE), :]
            rows = slice(l * NSA_KV, (l + 1) * NSA_KV)
            d_lo = _dot((x + pe_ref[0:1, rows]).astype(BF16), w_ref[rows, 0:half])
            d_hi = _dot((x + pe_ref[1:2, rows]).astype(BF16), w_ref[rows, half:2 * half])
            lo = d_lo if lo is None else lo + d_lo
            hi = d_hi if hi is None else hi + d_hi
        hi_next = pltpu.roll(hi, n - 1, axis=0)
        row = lax.broadcasted_iota(jnp.int32, lo.shape, 0)
        return lo + jnp.where(row == n - 1, 0.0, hi_next)

    kc = compress(xk_ref, pek_ref, wk_ref)
    vct = compress(xv_ref, pev_ref, wv_ref).T
    for h in range(NSA_KV_HEADS):
        sl = slice(h * HEAD, (h + 1) * HEAD)
        kc_ref[0, h] = _head_lanes(kc, h, BF16)
        vct_ref[0, h] = vct[sl, :].astype(BF16)


def _nsa_compress(kc_tokens, vc_tokens, pe_k2, pe_v2, wk2, wv2):
    B, T, _ = kc_tokens.shape
    ncp, gw = T // CMP_STRIDE, CMP_STRIDE * NSA_KV
    grp = pl.BlockSpec((1, T, NSA_KV), lambda b: (b, 0, 0))
    pe = pl.BlockSpec((2, gw), lambda b: (0, 0))
    wspec = pl.BlockSpec((gw, 2 * NSA_KV), lambda b: (0, 0))
    sds = jax.ShapeDtypeStruct
    return pl.pallas_call(
        _nsa_cmp_body,
        grid=(B,),
        in_specs=[grp, grp, pe, pe, wspec, wspec],
        out_specs=(pl.BlockSpec((1, NSA_KV_HEADS, ncp, LANE), lambda b: (b, 0, 0, 0)),
                   pl.BlockSpec((1, NSA_KV_HEADS, HEAD, ncp), lambda b: (b, 0, 0, 0))),
        out_shape=(sds((B, NSA_KV_HEADS, ncp, LANE), BF16), sds((B, NSA_KV_HEADS, HEAD, ncp), BF16)),
        compiler_params=_cparams(("parallel",)),
        name="nsa_compress",
    )(kc_tokens, vc_tokens, pe_k2, pe_v2, wk2, wv2)


def _nsa_attn_body(q_ref, kc_ref, vct_ref, ks_ref, vst_ref, kw_ref, vwt_ref, gt_ref, mcs_ref, blk_ref, cmpb_ref,
                   winb_ref, o_ref, *score_refs):
    jb = pl.program_id(1)
    s0 = jb * Q_BLOCK
    cols = NSA_GQA * Q_BLOCK
    heads = range(NSA_KV_HEADS)
    q = [q_ref[0, NSA_GQA * h:NSA_GQA * (h + 1)].reshape(cols, LANE) for h in heads]
    ncp = kc_ref.shape[2]
    per_gqa = lambda x: jnp.concatenate([x] * NSA_GQA, axis=1)
    col_max = lambda x: jnp.max(x, axis=0, keepdims=True)

    cmp_mask = per_gqa(cmpb_ref[pl.ds(pl.multiple_of(ncp - s0 // CMP_STRIDE, 8), ncp), :])
    t_c = s0 + lax.broadcasted_iota(jnp.int32, (1, cols), 1) % Q_BLOCK
    any_valid = jnp.where(t_c >= CMP_BLOCK - 1, 1.0, 0.0)
    mcs = mcs_ref[...]
    sc = [_dot_nt(kc_ref[0, h], q[h]) + cmp_mask for h in heads]
    e = [jnp.exp2(x - col_max(x)) for x in sc]
    pc = [x * (any_valid / jnp.sum(x, axis=0, keepdims=True)) for x in e]
    o_cmp = [_dot(vct_ref[0, h], pc[h].astype(BF16)) for h in heads]

    def importance(p):
        psum = p[:, 0:Q_BLOCK]
        for g in range(1, NSA_GQA):
            psum = psum + p[:, g * Q_BLOCK:(g + 1) * Q_BLOCK]
        p_hi = psum.astype(BF16)
        rest = psum - p_hi.astype(F32)
        p_mid = rest.astype(BF16)
        p_lo = (rest - p_mid.astype(F32)).astype(BF16)
        return _dot(mcs, p_hi) + (_dot(mcs, p_mid) + _dot(mcs, p_lo))

    imp = [importance(p) for p in pc]

    wkeys = WINDOW + Q_BLOCK
    off_w = pl.multiple_of(s0, Q_BLOCK)
    tri_old, tri_new = per_gqa(winb_ref[0]), per_gqa(winb_ref[1])
    sw = [_dot_nt(kw_ref[0, h, pl.ds(off_w, wkeys), :], q[h]) for h in heads]
    sw = [jnp.concatenate([x[0:Q_BLOCK] + tri_old, x[Q_BLOCK:WINDOW], x[WINDOW:wkeys] + tri_new], axis=0) for x in sw]
    pw = [jnp.exp2((x - col_max(x)).astype(BF16)) for x in sw]
    acc_w = [_dot(vwt_ref[0, h, :, pl.ds(off_w, wkeys)], pw[h]) for h in heads]

    j_idx = lax.broadcasted_iota(jnp.int32, (NSEL_PAD, Q_BLOCK), 0)
    jc = (s0 + lax.broadcasted_iota(jnp.int32, (NSEL_PAD, Q_BLOCK), 1)) // SEL_BLOCK
    causal = j_idx <= jc
    forced = lambda x: jnp.where(j_idx == 0, FORCE, jnp.where(j_idx == jc, FORCE, jnp.where(j_idx == jc - 1, FORCE, x)))
    work = [jnp.where(causal, forced(x), NEG) for x in imp]
    j_f = j_idx.astype(F32)
    for _ in range(SEL_TOPN):
        top = [col_max(w) for w in work]
        first = [jnp.min(jnp.where(w == t, j_f, float(NSEL_PAD)), axis=0, keepdims=True) for w, t in zip(work, top)]
        work = [jnp.where(j_f == f, -jnp.inf, w) for w, f in zip(work, first)]
    bias = [jnp.where(causal, jnp.where(w == -jnp.inf, 0.0, NEG), NEG) for w in work]

    q_aug = [jnp.concatenate([q[h], jnp.concatenate([bias[h].T.astype(BF16)] * NSA_GQA, axis=0)], axis=1)
             for h in heads]

    def sel_scores(h, off):
        k_aug = jnp.concatenate([ks_ref[0, h, pl.ds(off, KEY_GROUP), :], blk_ref[pl.ds(off, KEY_GROUP), :]], axis=1)
        return _dot_nt(k_aug, q_aug[h])

    def flash_update(v_t, st, carry):
        m, acc = carry
        m_new = jnp.maximum(m, col_max(st))
        p = jnp.exp2((st - m_new).astype(BF16))
        return m_new, acc * jnp.exp2(m - m_new) + _dot(v_t, p)

    init = (jnp.full((1, cols), NEG, F32), jnp.zeros((V_ROWS, cols), F32))
    n_groups = s0 // KEY_GROUP + 1
    n_pairs = n_groups // 2
    t_q = s0 + lax.broadcasted_iota(jnp.int32, (Q_BLOCK, cols), 1) % Q_BLOCK
    r_q = lax.broadcasted_iota(jnp.int32, (Q_BLOCK, cols), 0)

    def put_scores(ref, h, g):
        off = pl.multiple_of(g * KEY_GROUP, KEY_GROUP)
        ref[...] = sel_scores(h, off)
        r = pl.multiple_of(jnp.clip(s0 - off, 0, KEY_GROUP - Q_BLOCK), Q_BLOCK)
        tile = ref[pl.ds(r, Q_BLOCK), :]
        ref[pl.ds(r, Q_BLOCK), :] = jnp.where(off + r + r_q <= t_q, tile, NEG)

    def consume(ref, h, g, carry):
        off = pl.multiple_of(g * KEY_GROUP, KEY_GROUP)
        return flash_update(vst_ref[0, h, :, pl.ds(off, KEY_GROUP)], ref[...], carry)

    last = n_groups - 1
    buf0 = [(score_refs[4 * h], score_refs[4 * h + 1]) for h in heads]
    buf1 = [(score_refs[4 * h + 2], score_refs[4 * h + 3]) for h in heads]
    for h in heads:
        put_scores(buf0[h][0], h, 0)
    for h in heads:
        put_scores(buf0[h][1], h, jnp.minimum(1, last))

    def stage(cur, nxt, pair, carries):
        for h in heads:
            put_scores(nxt[h][0], h, jnp.minimum(2 * pair + 2, last))
        even = [consume(cur[h][0], h, 2 * pair, carries[h][0]) for h in heads]
        for h in heads:
            put_scores(nxt[h][1], h, jnp.minimum(2 * pair + 3, last))
        odd = [consume(cur[h][1], h, 2 * pair + 1, carries[h][1]) for h in heads]
        return tuple((even[h], odd[h]) for h in heads)

    def two_pairs(i, carries):
        return stage(buf1, buf0, 2 * i + 1, stage(buf0, buf1, 2 * i, carries))

    carries = lax.fori_loop(0, n_pairs // 2, two_pairs, tuple((init, init) for _ in heads))
    odd_pairs = n_pairs % 2
    carries = lax.fori_loop(n_pairs - odd_pairs, n_pairs, lambda pr, c: stage(buf0, buf1, pr, c), carries)
    left = n_groups - 2 * n_pairs

    def leftover(buf):
        return lambda _, c: tuple((consume(buf[h][0], h, last, c[h][0]), c[h][1]) for h in heads)

    carries = lax.fori_loop(0, left * (1 - odd_pairs), leftover(buf0), carries)
    carries = lax.fori_loop(0, left * odd_pairs, leftover(buf1), carries)

    for h in heads:
        (m_e, acc_e), (m_o, acc_o) = carries[h]
        m_s = jnp.maximum(m_e, m_o)
        acc_s = acc_e * jnp.exp2(m_e - m_s) + acc_o * jnp.exp2(m_o - m_s)

        def gate_row(branch):
            return jnp.concatenate([gt_ref[0, h, 3 * g + branch:3 * g + branch + 1, :] for g in range(NSA_GQA)], axis=1)

        o_t = (gate_row(0) * o_cmp[h] + (gate_row(1) / acc_s[HEAD:HEAD + 1]) * acc_s[0:HEAD]
               + (gate_row(2) / acc_w[h][HEAD:HEAD + 1]) * acc_w[h][0:HEAD])
        stacked = jnp.concatenate([o_t[:, g * Q_BLOCK:(g + 1) * Q_BLOCK] for g in range(NSA_GQA)], axis=0)
        o_ref[0, :, NSA_GQA * HEAD * h:NSA_GQA * HEAD * (h + 1)] = stacked.T


def _nsa_masks(seq_len):
    tq = np.arange(Q_BLOCK)[None, :]
    ncp = seq_len // CMP_STRIDE
    n_rel = np.arange(2 * ncp)[:, None] - ncp
    cmp_b = np.where(CMP_STRIDE * n_rel + CMP_BLOCK - 1 <= tq, 0.0, NEG).astype(np.float32)
    k_rel = np.arange(Q_BLOCK)[:, None]
    win_b = np.stack([np.where(k_rel > tq, 0.0, NEG), np.where(k_rel <= tq, 0.0, NEG)]).astype(np.float32)
    return jnp.asarray(cmp_b), jnp.asarray(win_b)


def _nsa_attention(q_r, kcmp, vcmp_t, ks_r, vs_t, kw_pad, vw_pad_t, g_t, mcs_t, blk_onehot, cmp_b, win_b):
    B, _, T, _ = q_r.shape
    ncp = kcmp.shape[2]
    nqb = T // Q_BLOCK
    tw = T + WINDOW
    nkv = NSA_KV_HEADS
    assert T % KEY_GROUP == 0 and T // SEL_BLOCK <= NSEL_PAD and WINDOW > Q_BLOCK and WINDOW % Q_BLOCK == 0
    kv_rows = lambda n: pl.BlockSpec((1, nkv, n, LANE), lambda b, j: (b, 0, 0, 0))
    kv_cols = lambda n, rows: pl.BlockSpec((1, nkv, rows, n), lambda b, j: (b, 0, 0, 0))
    const = lambda shp: pl.BlockSpec(shp, lambda b, j: (0,) * len(shp))
    return pl.pallas_call(
        _nsa_attn_body,
        grid=(B, nqb),
        in_specs=[pl.BlockSpec((1, NSA_Q_HEADS, Q_BLOCK, LANE), lambda b, j: (b, 0, j, 0)),
                  kv_rows(ncp), kv_cols(ncp, HEAD), kv_rows(T), kv_cols(T, V_ROWS), kv_rows(tw), kv_cols(tw, V_ROWS),
                  pl.BlockSpec((1, nkv, 16, Q_BLOCK), lambda b, j: (b, 0, 0, j)),
                  const((NSEL_PAD, ncp)), const((T, NSEL_PAD)), const((2 * ncp, Q_BLOCK)),
                  const((2, Q_BLOCK, Q_BLOCK))],
        out_specs=pl.BlockSpec((1, Q_BLOCK, NSA_WIDTH), lambda b, j: (b, j, 0)),
        out_shape=jax.ShapeDtypeStruct((B, T, NSA_WIDTH), F32),
        scratch_shapes=[pltpu.VMEM((KEY_GROUP, NSA_GQA * Q_BLOCK), F32)] * (4 * nkv),
        compiler_params=_cparams(("parallel", "arbitrary")),
        name="nsa_attention",
    )(q_r, kcmp, vcmp_t, ks_r, vs_t, kw_pad, vw_pad_t, g_t, mcs_t, blk_onehot, cmp_b, win_b)


def _gelu_tanh(x):
    return 0.5 * x * (1.0 + jnp.tanh(math.sqrt(2.0 / math.pi) * (x + 0.044715 * (x * x * x))))


def _tail_body(x_ref, ya_ref, yb_ref, yc_ref, p_ref, wa_ref, wb_ref, wc_ref, gffn_ref, wup_ref, cw_ref, cb_ref,
               wdn_ref, gple_ref, wg_ref, wp_ref, gfin_ref, o_ref, carry_ref, *, tiles_per_seq, f_chunk, final):
    i = pl.program_id(0)
    mix = _dot(ya_ref[...].astype(BF16), wa_ref[...])
    mix = mix + _dot(yb_ref[...].astype(BF16), wb_ref[...])
    mix = mix + _dot(yc_ref[...].astype(BF16), wc_ref[...])
    x = x_ref[...] + mix

    tm = x.shape[0]
    d_ff = wdn_ref.shape[0]
    h = _rms(x, gffn_ref[...]).astype(BF16)
    first = (i % tiles_per_seq) == 0
    row = lax.broadcasted_iota(jnp.int32, (tm, f_chunk), 0)

    def conv(col0):
        u = _dot(h, wup_ref[:, col0:col0 + f_chunk])
        p2 = jnp.where(first, 0.0, carry_ref[6:7, col0:col0 + f_chunk])
        p1 = jnp.where(first, 0.0, carry_ref[7:8, col0:col0 + f_chunk])
        u1 = jnp.where(row == 0, p1, pltpu.roll(u, 1, axis=0))
        u2 = jnp.where(row == 0, p2, jnp.where(row == 1, p1, pltpu.roll(u, 2, axis=0)))
        carry_ref[:, col0:col0 + f_chunk] = u[tm - 8:tm, :]
        cw = cw_ref[:, col0:col0 + f_chunk]
        return cw[0:1] * u2 + cw[1:2] * u1 + cw[2:3] * u + cb_ref[:, col0:col0 + f_chunk]

    for c0 in range(0, d_ff, f_chunk):
        act = _gelu_tanh(conv(c0)) * conv(d_ff + c0)
        x = x + _dot(act.astype(BF16), wdn_ref[c0:c0 + f_chunk, :])

    gate = _sigmoid(_dot(_rms(x, gple_ref[...]).astype(BF16), wg_ref[...]))
    y = x + _dot(p_ref[...].astype(BF16), wp_ref[...]) * gate
    o_ref[...] = _rms(y, gfin_ref[...]) if final else y


def _layer_tail(x2, ya, yb, yc, p2, w_out, g_ffn, w_up, conv_w, conv_b, w_down, g_ple, w_gate, w_proj, g_final,
                seq_len, final, tm=512, f_chunk=1408):
    m, d = x2.shape
    f2 = w_up.shape[1]
    d_ff = f2 // 2
    pd = p2.shape[1]
    assert d_ff % f_chunk == 0 and f_chunk % LANE == 0 and seq_len % tm == 0
    bf = lambda w: w.astype(BF16)
    wa, wb, wc = bf(w_out[:RW_WIDTH]), bf(w_out[RW_WIDTH:RW_WIDTH + POOL_WIDTH]), bf(w_out[RW_WIDTH + POOL_WIDTH:])
    tile = lambda w: pl.BlockSpec((tm, w), lambda i: (i, 0))
    full = lambda shp: pl.BlockSpec(shp, lambda i: (0,) * len(shp))
    return pl.pallas_call(
        functools.partial(_tail_body, tiles_per_seq=seq_len // tm, f_chunk=f_chunk, final=final),
        grid=(m // tm,),
        in_specs=[tile(d), tile(RW_WIDTH), tile(POOL_WIDTH), tile(NSA_WIDTH), tile(pd),
                  full((RW_WIDTH, d)), full((POOL_WIDTH, d)), full((NSA_WIDTH, d)),
                  full((1, d)), full((d, f2)), full((CONV_W, f2)), full((1, f2)), full((d_ff, d)),
                  full((1, d)), full((d, d)), full((pd, d)), full((1, d))],
        out_specs=tile(d),
        out_shape=jax.ShapeDtypeStruct((m, d), F32),
        scratch_shapes=[pltpu.VMEM((8, f2), F32)],
        compiler_params=_cparams(("arbitrary",)),
        name="layer_tail_final" if final else "layer_tail",
    )(x2, ya, yb, yc, p2, wa, wb, wc, g_ffn.reshape(1, d), bf(w_up), conv_w.reshape(CONV_W, f2),
      conv_b.reshape(1, f2), bf(w_down), g_ple.reshape(1, d), bf(w_gate), bf(w_proj), g_final.reshape(1, d))


def _overlap_matrix(seq_len):
    ncp = seq_len // CMP_STRIDE
    n_cmp = (seq_len - CMP_BLOCK) // CMP_STRIDE + 1
    cs = CMP_STRIDE * np.arange(n_cmp)
    ss = SEL_BLOCK * np.arange(seq_len // SEL_BLOCK)
    ov = (np.minimum(cs[:, None] + CMP_BLOCK - 1, ss[None] + SEL_BLOCK - 1) - np.maximum(cs[:, None], ss[None]) + 1)
    m = np.zeros((NSEL_PAD, ncp), np.float32)
    m[:ss.size, :n_cmp] = (np.clip(ov, 0, CMP_BLOCK).astype(np.float32) / CMP_BLOCK).T
    return jnp.asarray(m, dtype=BF16)


def _compress_weight(w):
    halves = w.reshape(2, CMP_STRIDE, HEAD, HEAD)
    eye = jnp.eye(NSA_KV_HEADS, dtype=w.dtype)
    w2 = jnp.einsum('sldf,hg->lhdsgf', halves, eye)
    return w2.reshape(CMP_STRIDE * NSA_KV, 2 * NSA_KV).astype(BF16)


def _compress_pe(pe):
    halves = pe.reshape(2, CMP_STRIDE, 1, HEAD)
    return jnp.broadcast_to(halves, (2, CMP_STRIDE, NSA_KV_HEADS, HEAD)).reshape(2, CMP_STRIDE * NSA_KV)


def _in_proj_layout(w_in):
    d = w_in.shape[0]
    gate0 = RW_IN + POOL_WIDTH + NSA_WIDTH + 6 * NSA_KV
    per_head = NSA_GQA * 3
    gates = jnp.zeros((d, LANE), w_in.dtype)
    for h in range(NSA_KV_HEADS):
        gates = gates.at[:, 16 * h:16 * h + per_head].set(w_in[:, gate0 + per_head * h:gate0 + per_head * (h + 1)])
    return jnp.concatenate([w_in[:, :gate0], gates], axis=1).astype(BF16)


_IN_SPLITS = ((0, RW_IN), (RW_IN, RW_IN + POOL_WIDTH), (RW_IN + POOL_WIDTH, RW_IN + POOL_WIDTH + NSA_WIDTH)) + tuple(
    (RW_IN + POOL_WIDTH + NSA_WIDTH + i * NSA_KV, RW_IN + POOL_WIDTH + NSA_WIDTH + (i + 1) * NSA_KV) for i in range(7))


def kernel(x, p, positions, g_mix, w_in, rw_mu, rw_w0, rw_w_up, rw_a0, rw_a_up, rw_g_up, rw_k_k, rw_k_a, rw_r_k, rw_gn_g, rw_gn_b, pool_w, pool_scale, nsa_pe_k, nsa_pe_v, nsa_w_ck, nsa_w_cv, w_out, g_ffn, ffn_w_up, ffn_conv_w, ffn_conv_b, ffn_w_down, g_ple, ple_w_gate, ple_w_proj, g_final):
    B, T, D = x.shape
    depth = w_in.shape[0]
    M = B * T

    half = HEAD // 2
    inv = ROPE_THETA ** (-jnp.arange(half, dtype=F32) / half)
    ang = positions.astype(F32)[..., None] * inv
    cos, sin = jnp.cos(ang), jnp.sin(ang)
    cos_t = jnp.concatenate([cos, cos] * (LANE // HEAD), axis=-1)
    sin_t = jnp.concatenate([-sin, sin] * (LANE // HEAD), axis=-1)
    mcs_t = _overlap_matrix(T)
    blk_onehot = (jnp.arange(T)[:, None] // SEL_BLOCK == jnp.arange(NSEL_PAD)[None, :]).astype(BF16)
    cmp_b, win_b = _nsa_masks(T)
    key_pad = (jnp.arange(LANE) == HEAD).astype(BF16)

    x2 = x.reshape(M, D)
    for i in range(depth):
        zs = _norm_matmul(x2, g_mix[i], _in_proj_layout(w_in[i]), _IN_SPLITS)
        z_rw, z_pool, z_q, z_kc, z_vc, z_ks, z_vs, z_kw, z_vw, z_g = (
            z.reshape(B, T, z.shape[1]) for z in zs)

        pm, qm, rh, y0, gate, bonus = _rw_chunks(z_rw, rw_mu[i], rw_w0[i], rw_w_up[i], rw_a0[i], rw_a_up[i],
                                                 rw_g_up[i], rw_k_k[i], rw_k_a[i], rw_r_k[i].reshape(-1))
        y_a = _rw_scan(pm, qm, rh, y0, gate, bonus, rw_gn_g[i], rw_gn_b[i])

        w_pool_bd = jax.scipy.linalg.block_diag(*[pool_w[i, gi] for gi in range(pool_w.shape[1])]).astype(BF16)
        y_b = _pool(z_pool, w_pool_bd, pool_scale[i])

        q_r, kc_r, ks_r, vs_t, kw_r, vw_t, g_t = _nsa_prep(z_q, z_kc, z_ks, z_vs, z_kw, z_vw, z_g, cos_t, sin_t)
        kcmp, vcmp_t = _nsa_compress(kc_r, z_vc, _compress_pe(nsa_pe_k[i]), _compress_pe(nsa_pe_v[i]),
                                     _compress_weight(nsa_w_ck[i]), _compress_weight(nsa_w_cv[i]))
        kw_pad = jnp.concatenate([jnp.broadcast_to(key_pad, (B, NSA_KV_HEADS, WINDOW, LANE)), kw_r], axis=2)
        vw_pad_t = jnp.pad(vw_t, ((0, 0), (0, 0), (0, 0), (WINDOW, 0)))
        y_c = _nsa_attention(q_r, kcmp, vcmp_t, ks_r, vs_t, kw_pad, vw_pad_t, g_t, mcs_t, blk_onehot, cmp_b, win_b)

        x2 = _layer_tail(x2, y_a.reshape(M, -1), y_b.reshape(M, -1), y_c.reshape(M, -1), p[i].reshape(M, -1),
                         w_out[i], g_ffn[i], ffn_w_up[i], ffn_conv_w[i], ffn_conv_b[i], ffn_w_down[i],
                         g_ple[i], ple_w_gate[i], ple_w_proj[i], g_final, T, final=(i == depth - 1))
    return x2.reshape(B, T, D)
```

```python
import functools
import math

import numpy as np
import jax
import jax.numpy as jnp
from jax import lax
from jax.experimental import pallas as pl
from jax.experimental.pallas import tpu as pltpu

F32 = jnp.float32
BF16 = jnp.bfloat16

RMS_EPS = 1e-6
ROPE_THETA = 10000.0
HEAD = 64
RW_HEADS = 4
RW_WIDTH = RW_HEADS * HEAD
RW_DECAY_LORA = 64
RW_AAA_LORA = 64
RW_GATE_LORA = 128
RW_GN_EPS = 64e-5
RW_IN = 3 * RW_WIDTH + RW_DECAY_LORA + RW_AAA_LORA + RW_GATE_LORA
POOL_WIDTH = 256
POOL_WINDOWS = (2, 4, 8, 16)
POOL_HALO = 16
NSA_Q_HEADS = 8
NSA_KV_HEADS = 2
NSA_GQA = NSA_Q_HEADS // NSA_KV_HEADS
NSA_WIDTH = NSA_Q_HEADS * HEAD
NSA_KV = NSA_KV_HEADS * HEAD
CMP_BLOCK = 32
CMP_STRIDE = 16
SEL_BLOCK = 64
SEL_TOPN = 16
WINDOW = 512
Q_BLOCK = 128
KEY_GROUP = 512
V_ROWS = HEAD + 16
NSEL_PAD = 128
NEG = -1e30
LOG2E = math.log2(math.e)
FORCE = 1e9
CONV_W = 3
LANE = 128
VMEM_LIMIT = 56 * 1024 * 1024

RW_CHUNK = 64


def _cparams(sem):
    return pltpu.CompilerParams(dimension_semantics=sem, vmem_limit_bytes=VMEM_LIMIT)


def _dot(a, b, prec=None):
    return lax.dot_general(a, b, (((1,), (0,)), ((), ())), precision=prec, preferred_element_type=F32)


def _dot_nt(a, b, prec=None):
    return lax.dot_general(a, b, (((1,), (1,)), ((), ())), precision=prec, preferred_element_type=F32)


def _sigmoid(x):
    return 1.0 / (1.0 + jnp.exp(-x))


def _rms(x, g):
    ms = jnp.mean(x * x, axis=-1, keepdims=True)
    return (x * lax.rsqrt(ms + RMS_EPS)) * g


def _norm_mm_body(x_ref, g_ref, w_ref, *o_refs, splits):
    h = _rms(x_ref[...], g_ref[...]).astype(BF16)
    for o_ref, (lo, hi) in zip(o_refs, splits):
        o_ref[...] = _dot(h, w_ref[:, lo:hi])


def _norm_matmul(x2, g, w_bf16, splits, tm=1024):
    m, d = x2.shape
    n = w_bf16.shape[1]
    outs = tuple(jax.ShapeDtypeStruct((m, hi - lo), F32) for lo, hi in splits)
    return pl.pallas_call(
        functools.partial(_norm_mm_body, splits=splits),
        grid=(m // tm,),
        in_specs=[pl.BlockSpec((tm, d), lambda i: (i, 0)),
                  pl.BlockSpec((1, d), lambda i: (0, 0)),
                  pl.BlockSpec((d, n), lambda i: (0, 0))],
        out_specs=tuple(pl.BlockSpec((tm, hi - lo), lambda i: (i, 0)) for lo, hi in splits),
        out_shape=outs,
        compiler_params=_cparams(("parallel",)),
        name="norm_in_proj",
    )(x2, g.reshape(1, d), w_bf16)


def _split(x):
    hi = x.astype(BF16)
    return hi, (x - hi.astype(F32)).astype(BF16)


def _dot3(a, b, dot=_dot):
    return dot(a[0], b[0]) + (dot(a[0], b[1]) + dot(a[1], b[0]))


def _dot_ones(x, ones_bf16):
    hi, lo = _split(x)
    return _dot(hi, ones_bf16) + _dot(lo, ones_bf16)


def _head_ones():
    hr = lax.broadcasted_iota(jnp.int32, (RW_WIDTH, RW_WIDTH), 0) // HEAD
    hc = lax.broadcasted_iota(jnp.int32, (RW_WIDTH, RW_WIDTH), 1) // HEAD
    return hr == hc


def _rw_chunk_body(z_ref, zp_ref, mu_ref, w0_ref, wup_ref, a0_ref, aup_ref, gup_ref, kk_ref, ka_ref,
                   rk_ref, p_ref, q_ref, rh_ref, y0_ref, g_ref, bonus_ref):
    i = pl.program_id(1)
    C = RW_CHUNK
    W = RW_WIDTH
    z = z_ref[0]
    rows = z.shape[0]
    n_chunks = rows // C
    prev = jnp.where(i == 0, 0.0, zp_ref[0, 7:8, :])
    row_z = lax.broadcasted_iota(jnp.int32, z.shape, 0)
    zs = jnp.where(row_z == 0, prev, pltpu.roll(z, 1, axis=0))
    zf = z + (zs - z) * mu_ref[...]
    r = zf[:, 0:W]
    k = zf[:, W:2 * W]
    v = zf[:, 2 * W:3 * W]
    zw = zf[:, 3 * W:3 * W + RW_DECAY_LORA]
    za = zf[:, 3 * W + RW_DECAY_LORA:3 * W + RW_DECAY_LORA + RW_AAA_LORA]
    zg = zf[:, 3 * W + RW_DECAY_LORA + RW_AAA_LORA:]

    lora = lambda x, w_ref: _dot3(_split(x), (w_ref[0], w_ref[1]))
    xw = -(w0_ref[...] + lora(jnp.tanh(zw), wup_ref))
    softplus = jnp.maximum(xw, 0.0) + jnp.log(1.0 + jnp.exp(-jnp.abs(xw)))
    ld = -jnp.exp(-softplus - 0.5)
    a = _sigmoid(a0_ref[...] + lora(za, aup_ref))
    g_ref[0] = lora(_sigmoid(zg), gup_ref)
    kk = k * kk_ref[...]
    k2 = k * (1.0 + (a - 1.0) * ka_ref[...])

    same_head = _head_ones()
    head_ones = jnp.where(same_head, 1.0, 0.0).astype(BF16)
    kkn = kk / jnp.maximum(jnp.sqrt(_dot_ones(kk * kk, head_ones)), 1e-12)
    bonus_ref[0] = _dot_ones(r * k2 * rk_ref[...], head_ones) * v

    t_in = lax.broadcasted_iota(jnp.int32, (rows, W), 0) % C
    lc = ld
    shift = 1
    while shift < C:
        lc = lc + jnp.where(t_in >= shift, pltpu.roll(lc, shift, axis=0), 0.0)
        shift *= 2

    tr = lax.broadcasted_iota(jnp.int32, (W, W), 0) % C
    tc = lax.broadcasted_iota(jnp.int32, (W, W), 1) % C
    lower_incl = tr >= tc
    lower_strict = tr > tc
    eye = jnp.where(same_head, jnp.where(tr == tc, 1.0, 0.0), 0.0)

    def embed(x):
        return jnp.where(same_head, jnp.concatenate([x] * RW_HEADS, axis=0), 0.0)

    bf = lambda x: x.astype(BF16)
    A, Bt, Kt, Rt, V, BhT, KhT, gdiag, Rt_raw = [], [], [], [], [], [], [], [], []
    for c in range(n_chunks):
        rs = slice(c * C, (c + 1) * C)
        lc_c, ld_c = lc[rs], ld[rs]
        lc_end = lc_c[C - 1:C, :]
        ginv = jnp.exp(-lc_c)
        dec_end = jnp.exp(lc_end - lc_c)
        kkn_c, a_c, k2_c = kkn[rs], a[rs], k2[rs]
        r_t = r[rs] * jnp.exp(lc_c)
        A.append(bf(embed(-kkn_c * jnp.exp(lc_c - ld_c))))
        Bt.append(bf(embed(kkn_c * a_c * ginv)))
        Kt.append(bf(embed(k2_c * ginv)))
        Rt.append(bf(embed(r_t)))
        Rt_raw.append(r_t)
        V.append(bf(embed(v[rs])))
        BhT.append(bf(embed(kkn_c * a_c * dec_end).T))
        KhT.append(bf(embed(k2_c * dec_end).T))
        gdiag.append(jnp.where(eye > 0.5, jnp.broadcast_to(jnp.exp(lc_end), (W, W)), 0.0))

    each = lambda f, *lists: [f(*xs) for xs in zip(*lists)]
    L = each(lambda x, y: jnp.where(lower_strict, _dot_nt(x, y), 0.0), A, Bt)
    Lak = each(lambda x, y: bf(jnp.where(lower_strict, _dot_nt(x, y), 0.0)), A, Kt)
    Mrb = each(lambda x, y: bf(jnp.where(lower_incl, _dot_nt(x, y), 0.0)), Rt, Bt)
    Mrk = each(lambda x, y: bf(jnp.where(lower_incl, _dot_nt(x, y), 0.0)), Rt, Kt)
    Tm = [eye + x for x in L]
    Pw = each(bf, L)
    for _ in range(int(math.log2(C)) - 1):
        Pw = each(lambda x: bf(_dot(x, x)), Pw)
        Tm = each(lambda t, x: t + _dot(bf(t), x), Tm, Pw)
    Tm = each(bf, Tm)
    Wm = each(lambda t, x: bf(_dot(t, x)), Tm, A)
    LakV = each(lambda x, y: bf(_dot(x, y)), Lak, V)
    U0 = each(lambda t, x: bf(_dot(t, x)), Tm, LakV)
    Pm = each(lambda d, x, y: d + _dot(x, y), gdiag, BhT, Wm)
    Qm = each(lambda x, y, s, t: _dot(x, y) + _dot(s, t), BhT, U0, KhT, V)
    Rh = each(_dot, Mrb, Wm)
    Y0 = each(lambda x, y, s, t: _dot(x, y) + _dot(s, t), Mrb, U0, Mrk, V)

    def flatten(x_bd):
        out = x_bd[0:C]
        for h in range(1, RW_HEADS):
            out = out + x_bd[h * C:(h + 1) * C]
        return out

    for c in range(n_chunks):
        rs = slice(c * C, (c + 1) * C)
        p_hi, p_lo = _split(Pm[c])
        p_ref[0, c, 0] = p_hi
        p_ref[0, c, 1] = p_lo
        q_ref[0, c] = Qm[c]
        rh_ref[0, rs, :] = Rt_raw[c] + flatten(Rh[c])
        y0_ref[0, rs, :] = flatten(Y0[c])


def _rw_chunks(z_rw, mu, w0, w_up, a0, a_up, g_up, k_k, k_a, r_k, chunks_per_step=8):
    B, T, _ = z_rw.shape
    C = RW_CHUNK
    assert C == HEAD, "the chunk build shares one index grid between time and head-dim masks"
    rows = C * chunks_per_step
    nc = T // C
    row = lambda a: a.reshape(1, -1)
    pair = lambda w: jnp.stack(_split(w))
    full = lambda shp: pl.BlockSpec(shp, lambda b, i: (0,) * len(shp))
    seq = lambda w: pl.BlockSpec((1, rows, w), lambda b, i: (b, i, 0))
    sds = jax.ShapeDtypeStruct
    seq_shape = sds((B, T, RW_WIDTH), F32)
    return pl.pallas_call(
        _rw_chunk_body,
        grid=(B, T // rows),
        in_specs=[seq(RW_IN),
                  pl.BlockSpec((1, 8, RW_IN), lambda b, i: (b, jnp.maximum(i * (rows // 8) - 1, 0), 0)),
                  full((1, RW_IN)), full((1, RW_WIDTH)), full((2, RW_DECAY_LORA, RW_WIDTH)),
                  full((1, RW_WIDTH)), full((2, RW_AAA_LORA, RW_WIDTH)), full((2, RW_GATE_LORA, RW_WIDTH)),
                  full((1, RW_WIDTH)), full((1, RW_WIDTH)), full((1, RW_WIDTH))],
        out_specs=(pl.BlockSpec((1, chunks_per_step, 2, RW_WIDTH, RW_WIDTH), lambda b, i: (b, i, 0, 0, 0)),
                   pl.BlockSpec((1, chunks_per_step, RW_WIDTH, RW_WIDTH), lambda b, i: (b, i, 0, 0)),
                   seq(RW_WIDTH), seq(RW_WIDTH), seq(RW_WIDTH), seq(RW_WIDTH)),
        out_shape=(sds((B, nc, 2, RW_WIDTH, RW_WIDTH), BF16), sds((B, nc, RW_WIDTH, RW_WIDTH), F32),
                   seq_shape, seq_shape, seq_shape, seq_shape),
        compiler_params=_cparams(("parallel", "parallel")),
        name="rwkv_chunk_build",
    )(z_rw, z_rw, row(mu), row(w0), pair(w_up), row(a0), pair(a_up), pair(g_up), row(k_k), row(k_a), row(r_k))


def _rw_scan_body(p_ref, q_ref, rh_ref, y0_ref, g_ref, bonus_ref, gng_ref, gnb_ref, o_ref, h_ref):
    @pl.when(pl.program_id(0) == 0)
    def _():
        h_ref[...] = jnp.zeros_like(h_ref)

    nb = h_ref.shape[0]
    head_ones = jnp.where(_head_ones(), 1.0, 0.0).astype(BF16)
    hs = [_split(h_ref[b]) for b in range(nb)]
    ys = [_dot3(_split(rh_ref[b]), hs[b]) + y0_ref[b] for b in range(nb)]
    for b in range(nb):
        h_ref[b] = _dot3((p_ref[b, 0, 0], p_ref[b, 0, 1]), hs[b]) + q_ref[b, 0]
    inv_n = 1.0 / HEAD
    for b in range(nb):
        y = ys[b]
        d = y - _dot_ones(y, head_ones) * inv_n
        var = _dot_ones(d * d, head_ones) * inv_n
        yn = d * lax.rsqrt(var + RW_GN_EPS) * gng_ref[...] + gnb_ref[...]
        o_ref[b] = (yn + bonus_ref[b]) * g_ref[b]


def _rw_scan(p, q, rh, y0, g, bonus, gn_g, gn_b):
    B, T, _ = rh.shape
    C = RW_CHUNK
    nc = T // C
    seq = pl.BlockSpec((B, C, RW_WIDTH), lambda c: (0, c, 0))
    vec = pl.BlockSpec((1, RW_WIDTH), lambda c: (0, 0))
    return pl.pallas_call(
        _rw_scan_body,
        grid=(nc,),
        in_specs=[pl.BlockSpec((B, 1, 2, RW_WIDTH, RW_WIDTH), lambda c: (0, c, 0, 0, 0)),
                  pl.BlockSpec((B, 1, RW_WIDTH, RW_WIDTH), lambda c: (0, c, 0, 0)),
                  seq, seq, seq, seq, vec, vec],
        out_specs=seq,
        out_shape=jax.ShapeDtypeStruct((B, T, RW_WIDTH), F32),
        scratch_shapes=[pltpu.VMEM((B, RW_WIDTH, RW_WIDTH), F32)],
        compiler_params=_cparams(("arbitrary",)),
        name="rwkv_chunk_scan",
    )(p, q, rh, y0, g, bonus, gn_g.reshape(1, -1), gn_b.reshape(1, -1))


def _pool_body(z_ref, zp_ref, w_ref, scale_ref, o_ref):
    i = pl.program_id(1)
    z = z_ref[0]
    tt = z.shape[0]
    halo = jnp.where(i == 0, 0.0, zp_ref[0])
    e = jnp.concatenate([halo, z], axis=0)
    sums = []
    s = e
    for shift in (1, 2, 4, 8):
        s = s + pltpu.roll(s, shift, axis=0)
        sums.append(s[POOL_HALO:, :])
    t_idx = i * tt + lax.broadcasted_iota(jnp.int32, (tt, POOL_WIDTH), 0)
    lane_group = lax.broadcasted_iota(jnp.int32, (tt, POOL_WIDTH), 1) // HEAD
    pooled = jnp.zeros_like(z)
    for gi, win in enumerate(POOL_WINDOWS):
        cnt = jnp.minimum(t_idx + 1, win).astype(F32)
        pooled = jnp.where(lane_group == gi, sums[gi] / cnt - z, pooled)
    o_ref[0] = _dot(pooled.astype(BF16), w_ref[...]) * scale_ref[...]


def _pool(z_pool, w_blockdiag_bf16, scale, tt=512):
    B, T, _ = z_pool.shape
    return pl.pallas_call(
        _pool_body,
        grid=(B, T // tt),
        in_specs=[pl.BlockSpec((1, tt, POOL_WIDTH), lambda b, i: (b, i, 0)),
                  pl.BlockSpec((1, POOL_HALO, POOL_WIDTH),
                               lambda b, i: (b, jnp.maximum(i * (tt // POOL_HALO) - 1, 0), 0)),
                  pl.BlockSpec((POOL_WIDTH, POOL_WIDTH), lambda b, i: (0, 0)),
                  pl.BlockSpec((1, POOL_WIDTH), lambda b, i: (0, 0))],
        out_specs=pl.BlockSpec((1, tt, POOL_WIDTH), lambda b, i: (b, i, 0)),
        out_shape=jax.ShapeDtypeStruct((B, T, POOL_WIDTH), F32),
        compiler_params=_cparams(("parallel", "parallel")),
        name="pool_mixer",
    )(z_pool, z_pool, w_blockdiag_bf16, scale.reshape(1, -1))


def _rope(x, cos_w, sin_w):
    w = x.shape[1]
    lane = lax.broadcasted_iota(jnp.int32, x.shape, 1) % HEAD
    partner = jnp.where(lane < HEAD // 2, pltpu.roll(x, w - HEAD // 2, axis=1), pltpu.roll(x, HEAD // 2, axis=1))
    return x * cos_w + partner * sin_w


def _head_lanes(a, h, dtype):
    part = a[:, h * HEAD:(h + 1) * HEAD]
    return jnp.concatenate([part, jnp.zeros_like(part)], axis=1).astype(dtype)


def _nsa_prep_body(zq_ref, zkc_ref, zks_ref, zvs_ref, zkw_ref, zvw_ref, zg_ref, cos_ref, sin_ref,
                   q_ref, kc_ref, ks_ref, vst_ref, kw_ref, vwt_ref, gt_ref):
    cos = cos_ref[0]
    sin = sin_ref[0]
    cos_q = jnp.concatenate([cos] * (NSA_WIDTH // LANE), axis=1)
    sin_q = jnp.concatenate([sin] * (NSA_WIDTH // LANE), axis=1)
    qr = _rope(zq_ref[0], cos_q, sin_q) * (HEAD ** -0.5 * LOG2E)
    pad_lane = lax.broadcasted_iota(jnp.int32, (qr.shape[0], LANE), 1) == HEAD
    for h in range(NSA_Q_HEADS):
        q_ref[0, h] = jnp.where(pad_lane, NEG, _head_lanes(qr, h, F32)).astype(BF16)
    kc_ref[0] = _rope(zkc_ref[0], cos, sin)
    ks = _rope(zks_ref[0], cos, sin)
    kw = _rope(zkw_ref[0], cos, sin)
    vst = zvs_ref[0].T
    vwt = zvw_ref[0].T
    for h in range(NSA_KV_HEADS):
        sl = slice(h * HEAD, (h + 1) * HEAD)
        ks_ref[0, h] = _head_lanes(ks, h, BF16)
        kw_ref[0, h] = _head_lanes(kw, h, BF16)
        tail = lax.broadcasted_iota(jnp.int32, (V_ROWS - HEAD, vst.shape[1]), 0)
        ones_row = jnp.where(tail == 0, 1.0, 0.0).astype(BF16)
        vst_ref[0, h, 0:HEAD, :] = vst[sl, :].astype(BF16)
        vst_ref[0, h, HEAD:V_ROWS, :] = ones_row
        vwt_ref[0, h, 0:HEAD, :] = vwt[sl, :].astype(BF16)
        vwt_ref[0, h, HEAD:V_ROWS, :] = ones_row
    gt = _sigmoid(zg_ref[0]).T
    for h in range(NSA_KV_HEADS):
        gt_ref[0, h] = gt[16 * h:16 * (h + 1), :]


def _nsa_prep(z_q, z_kc, z_ks, z_vs, z_kw, z_vw, z_g, cos_t, sin_t, tt=512):
    B, T, _ = z_q.shape
    seq = lambda w: pl.BlockSpec((1, tt, w), lambda b, i: (b, i, 0))
    hm = lambda nh: pl.BlockSpec((1, nh, tt, LANE), lambda b, i: (b, 0, i, 0))
    tr = lambda rows: pl.BlockSpec((1, NSA_KV_HEADS, rows, tt), lambda b, i: (b, 0, 0, i))
    sds = jax.ShapeDtypeStruct
    return pl.pallas_call(
        _nsa_prep_body,
        grid=(B, T // tt),
        in_specs=[seq(NSA_WIDTH)] + [seq(NSA_KV)] * 6 + [seq(LANE), seq(LANE)],
        out_specs=(hm(NSA_Q_HEADS), seq(NSA_KV), hm(NSA_KV_HEADS), tr(V_ROWS), hm(NSA_KV_HEADS), tr(V_ROWS), tr(16)),
        out_shape=(sds((B, NSA_Q_HEADS, T, LANE), BF16), sds((B, T, NSA_KV), F32),
                   sds((B, NSA_KV_HEADS, T, LANE), BF16), sds((B, NSA_KV_HEADS, V_ROWS, T), BF16),
                   sds((B, NSA_KV_HEADS, T, LANE), BF16), sds((B, NSA_KV_HEADS, V_ROWS, T), BF16),
                   sds((B, NSA_KV_HEADS, 16, T), F32)),
        compiler_params=_cparams(("parallel", "parallel")),
        name="nsa_prep",
    )(z_q, z_kc, z_ks, z_vs, z_kw, z_vw, z_g, cos_t, sin_t)


def _nsa_cmp_body(xk_ref, xv_ref, pek_ref, pev_ref, wk_ref, wv_ref, kc_ref, vct_ref):
    half = NSA_KV
    n = kc_ref.shape[2]

    def compress(x_ref, pe_ref, w_ref):
        lo = hi = None
        for l in range(CMP_STRIDE):
            x = x_ref[0, pl.ds(l, n, stride=CMP_STRIDE), :]
            rows = slice(l * NSA_KV, (l + 1) * NSA_KV)
            d_lo = _dot((x + pe_ref[0:1, rows]).astype(BF16), w_ref[rows, 0:half])
            d_hi = _dot((x + pe_ref[1:2, rows]).astype(BF16), w_ref[rows, half:2 * half])
            lo = d_lo if lo is None else lo + d_lo
            hi = d_hi if hi is None else hi + d_hi
        hi_next = pltpu.roll(hi, n - 1, axis=0)
        row = lax.broadcasted_iota(jnp.int32, lo.shape, 0)
        return lo + jnp.where(row == n - 1, 0.0, hi_next)

    kc = compress(xk_ref, pek_ref, wk_ref)
    vct = compress(xv_ref, pev_ref, wv_ref).T
    for h in range(NSA_KV_HEADS):
        sl = slice(h * HEAD, (h + 1) * HEAD)
        kc_ref[0, h] = _head_lanes(kc, h, BF16)
        vct_ref[0, h] = vct[sl, :].astype(BF16)


def _nsa_compress(kc_tokens, vc_tokens, pe_k2, pe_v2, wk2, wv2):
    B, T, _ = kc_tokens.shape
    ncp, gw = T // CMP_STRIDE, CMP_STRIDE * NSA_KV
    grp = pl.BlockSpec((1, T, NSA_KV), lambda b: (b, 0, 0))
    pe = pl.BlockSpec((2, gw), lambda b: (0, 0))
    wspec = pl.BlockSpec((gw, 2 * NSA_KV), lambda b: (0, 0))
    sds = jax.ShapeDtypeStruct
    return pl.pallas_call(
        _nsa_cmp_body,
        grid=(B,),
        in_specs=[grp, grp, pe, pe, wspec, wspec],
        out_specs=(pl.BlockSpec((1, NSA_KV_HEADS, ncp, LANE), lambda b: (b, 0, 0, 0)),
                   pl.BlockSpec((1, NSA_KV_HEADS, HEAD, ncp), lambda b: (b, 0, 0, 0))),
        out_shape=(sds((B, NSA_KV_HEADS, ncp, LANE), BF16), sds((B, NSA_KV_HEADS, HEAD, ncp), BF16)),
        compiler_params=_cparams(("parallel",)),
        name="nsa_compress",
    )(kc_tokens, vc_tokens, pe_k2, pe_v2, wk2, wv2)


def _nsa_attn_body(q_ref, kc_ref, vct_ref, ks_ref, vst_ref, kw_ref, vwt_ref, gt_ref, mcs_ref, blk_ref, cmpb_ref,
                   winb_ref, o_ref, *score_refs):
    jb = pl.program_id(1)
    s0 = jb * Q_BLOCK
    cols = NSA_GQA * Q_BLOCK
    heads = range(NSA_KV_HEADS)
    q = [q_ref[0, NSA_GQA * h:NSA_GQA * (h + 1)].reshape(cols, LANE) for h in heads]
    ncp = kc_ref.shape[2]
    per_gqa = lambda x: jnp.concatenate([x] * NSA_GQA, axis=1)
    col_max = lambda x: jnp.max(x, axis=0, keepdims=True)

    cmp_mask = per_gqa(cmpb_ref[pl.ds(pl.multiple_of(ncp - s0 // CMP_STRIDE, 8), ncp), :])
    t_c = s0 + lax.broadcasted_iota(jnp.int32, (1, cols), 1) % Q_BLOCK
    any_valid = jnp.where(t_c >= CMP_BLOCK - 1, 1.0, 0.0)
    mcs = mcs_ref[...]
    sc = [_dot_nt(kc_ref[0, h], q[h]) + cmp_mask for h in heads]
    e = [jnp.exp2(x - col_max(x)) for x in sc]
    pc = [x * (any_valid / jnp.sum(x, axis=0, keepdims=True)) for x in e]
    o_cmp = [_dot(vct_ref[0, h], pc[h].astype(BF16)) for h in heads]

    def importance(p):
        psum = p[:, 0:Q_BLOCK]
        for g in range(1, NSA_GQA):
            psum = psum + p[:, g * Q_BLOCK:(g + 1) * Q_BLOCK]
        p_hi = psum.astype(BF16)
        rest = psum - p_hi.astype(F32)
        p_mid = rest.astype(BF16)
        p_lo = (rest - p_mid.astype(F32)).astype(BF16)
        return _dot(mcs, p_hi) + (_dot(mcs, p_mid) + _dot(mcs, p_lo))

    imp = [importance(p) for p in pc]

    wkeys = WINDOW + Q_BLOCK
    off_w = pl.multiple_of(s0, Q_BLOCK)
    tri_old, tri_new = per_gqa(winb_ref[0]), per_gqa(winb_ref[1])
    sw = [_dot_nt(kw_ref[0, h, pl.ds(off_w, wkeys), :], q[h]) for h in heads]
    sw = [jnp.concatenate([x[0:Q_BLOCK] + tri_old, x[Q_BLOCK:WINDOW], x[WINDOW:wkeys] + tri_new], axis=0) for x in sw]
    pw = [jnp.exp2((x - col_max(x)).astype(BF16)) for x in sw]
    acc_w = [_dot(vwt_ref[0, h, :, pl.ds(off_w, wkeys)], pw[h]) for h in heads]

    j_idx = lax.broadcasted_iota(jnp.int32, (NSEL_PAD, Q_BLOCK), 0)
    jc = (s0 + lax.broadcasted_iota(jnp.int32, (NSEL_PAD, Q_BLOCK), 1)) // SEL_BLOCK
    causal = j_idx <= jc
    forced = lambda x: jnp.where(j_idx == 0, FORCE, jnp.where(j_idx == jc, FORCE, jnp.where(j_idx == jc - 1, FORCE, x)))
    work = [jnp.where(causal, forced(x), NEG) for x in imp]
    j_f = j_idx.astype(F32)
    for _ in range(SEL_TOPN):
        top = [col_max(w) for w in work]
        first = [jnp.min(jnp.where(w == t, j_f, float(NSEL_PAD)), axis=0, keepdims=True) for w, t in zip(work, top)]
        work = [jnp.where(j_f == f, -jnp.inf, w) for w, f in zip(work, first)]
    bias = [jnp.where(causal, jnp.where(w == -jnp.inf, 0.0, NEG), NEG) for w in work]

    q_aug = [jnp.concatenate([q[h], jnp.concatenate([bias[h].T.astype(BF16)] * NSA_GQA, axis=0)], axis=1)
             for h in heads]

    def sel_scores(h, off):
        k_aug = jnp.concatenate([ks_ref[0, h, pl.ds(off, KEY_GROUP), :], blk_ref[pl.ds(off, KEY_GROUP), :]], axis=1)
        return _dot_nt(k_aug, q_aug[h])

    def flash_update(v_t, st, carry):
        m, acc = carry
        m_new = jnp.maximum(m, col_max(st))
        p = jnp.exp2((st - m_new).astype(BF16))
        return m_new, acc * jnp.exp2(m - m_new) + _dot(v_t, p)

    init = (jnp.full((1, cols), NEG, F32), jnp.zeros((V_ROWS, cols), F32))
    n_groups = s0 // KEY_GROUP + 1
    n_pairs = n_groups // 2
    t_q = s0 + lax.broadcasted_iota(jnp.int32, (Q_BLOCK, cols), 1) % Q_BLOCK
    r_q = lax.broadcasted_iota(jnp.int32, (Q_BLOCK, cols), 0)

    def put_scores(ref, h, g):
        off = pl.multiple_of(g * KEY_GROUP, KEY_GROUP)
        ref[...] = sel_scores(h, off)
        r = pl.multiple_of(jnp.clip(s0 - off, 0, KEY_GROUP - Q_BLOCK), Q_BLOCK)
        tile = ref[pl.ds(r, Q_BLOCK), :]
        ref[pl.ds(r, Q_BLOCK), :] = jnp.where(off + r + r_q <= t_q, tile, NEG)

    def consume(ref, h, g, carry):
        off = pl.multiple_of(g * KEY_GROUP, KEY_GROUP)
        return flash_update(vst_ref[0, h, :, pl.ds(off, KEY_GROUP)], ref[...], carry)

    last = n_groups - 1
    buf0 = [(score_refs[4 * h], score_refs[4 * h + 1]) for h in heads]
    buf1 = [(score_refs[4 * h + 2], score_refs[4 * h + 3]) for h in heads]
    for h in heads:
        put_scores(buf0[h][0], h, 0)
    for h in heads:
        put_scores(buf0[h][1], h, jnp.minimum(1, last))

    def stage(cur, nxt, pair, carries):
        for h in heads:
            put_scores(nxt[h][0], h, jnp.minimum(2 * pair + 2, last))
        carries = [consume(cur[h][0], h, 2 * pair, carries[h]) for h in heads]
        for h in heads:
            put_scores(nxt[h][1], h, jnp.minimum(2 * pair + 3, last))
        return tuple(consume(cur[h][1], h, 2 * pair + 1, carries[h]) for h in heads)

    def two_pairs(i, carries):
        return stage(buf1, buf0, 2 * i + 1, stage(buf0, buf1, 2 * i, carries))

    carries = lax.fori_loop(0, n_pairs // 2, two_pairs, tuple(init for _ in heads))
    odd_pairs = n_pairs % 2
    carries = lax.fori_loop(n_pairs - odd_pairs, n_pairs, lambda pr, c: stage(buf0, buf1, pr, c), carries)
    left = n_groups - 2 * n_pairs

    def leftover(buf):
        return lambda _, c: tuple(consume(buf[h][0], h, last, c[h]) for h in heads)

    carries = lax.fori_loop(0, left * (1 - odd_pairs), leftover(buf0), carries)
    carries = lax.fori_loop(0, left * odd_pairs, leftover(buf1), carries)

    for h in heads:
        _, acc_s = carries[h]

        def gate_row(branch):
            return jnp.concatenate([gt_ref[0, h, 3 * g + branch:3 * g + branch + 1, :] for g in range(NSA_GQA)], axis=1)

        o_t = (gate_row(0) * o_cmp[h] + (gate_row(1) / acc_s[HEAD:HEAD + 1]) * acc_s[0:HEAD]
               + (gate_row(2) / acc_w[h][HEAD:HEAD + 1]) * acc_w[h][0:HEAD])
        stacked = jnp.concatenate([o_t[:, g * Q_BLOCK:(g + 1) * Q_BLOCK] for g in range(NSA_GQA)], axis=0)
        o_ref[0, :, NSA_GQA * HEAD * h:NSA_GQA * HEAD * (h + 1)] = stacked.T


def _nsa_masks(seq_len):
    tq = np.arange(Q_BLOCK)[None, :]
    ncp = seq_len // CMP_STRIDE
    n_rel = np.arange(2 * ncp)[:, None] - ncp
    cmp_b = np.where(CMP_STRIDE * n_rel + CMP_BLOCK - 1 <= tq, 0.0, NEG).astype(np.float32)
    k_rel = np.arange(Q_BLOCK)[:, None]
    win_b = np.stack([np.where(k_rel > tq, 0.0, NEG), np.where(k_rel <= tq, 0.0, NEG)]).astype(np.float32)
    return jnp.asarray(cmp_b), jnp.asarray(win_b)


def _nsa_attention(q_r, kcmp, vcmp_t, ks_r, vs_t, kw_pad, vw_pad_t, g_t, mcs_t, blk_onehot, cmp_b, win_b):
    B, _, T, _ = q_r.shape
    ncp = kcmp.shape[2]
    nqb = T // Q_BLOCK
    tw = T + WINDOW
    nkv = NSA_KV_HEADS
    assert T % KEY_GROUP == 0 and T // SEL_BLOCK <= NSEL_PAD and WINDOW > Q_BLOCK and WINDOW % Q_BLOCK == 0
    kv_rows = lambda n: pl.BlockSpec((1, nkv, n, LANE), lambda b, j: (b, 0, 0, 0))
    kv_cols = lambda n, rows: pl.BlockSpec((1, nkv, rows, n), lambda b, j: (b, 0, 0, 0))
    const = lambda shp: pl.BlockSpec(shp, lambda b, j: (0,) * len(shp))
    return pl.pallas_call(
        _nsa_attn_body,
        grid=(B, nqb),
        in_specs=[pl.BlockSpec((1, NSA_Q_HEADS, Q_BLOCK, LANE), lambda b, j: (b, 0, j, 0)),
                  kv_rows(ncp), kv_cols(ncp, HEAD), kv_rows(T), kv_cols(T, V_ROWS), kv_rows(tw), kv_cols(tw, V_ROWS),
                  pl.BlockSpec((1, nkv, 16, Q_BLOCK), lambda b, j: (b, 0, 0, j)),
                  const((NSEL_PAD, ncp)), const((T, NSEL_PAD)), const((2 * ncp, Q_BLOCK)),
                  const((2, Q_BLOCK, Q_BLOCK))],
        out_specs=pl.BlockSpec((1, Q_BLOCK, NSA_WIDTH), lambda b, j: (b, j, 0)),
        out_shape=jax.ShapeDtypeStruct((B, T, NSA_WIDTH), F32),
        scratch_shapes=[pltpu.VMEM((KEY_GROUP, NSA_GQA * Q_BLOCK), F32)] * (4 * nkv),
        compiler_params=_cparams(("parallel", "arbitrary")),
        name="nsa_attention",
    )(q_r, kcmp, vcmp_t, ks_r, vs_t, kw_pad, vw_pad_t, g_t, mcs_t, blk_onehot, cmp_b, win_b)


def _gelu_tanh(x):
    return 0.5 * x * (1.0 + jnp.tanh(math.sqrt(2.0 / math.pi) * (x + 0.044715 * (x * x * x))))


def _tail_body(x_ref, ya_ref, yb_ref, yc_ref, p_ref, wa_ref, wb_ref, wc_ref, gffn_ref, wup_ref, cw_ref, cb_ref,
               wdn_ref, gple_ref, wg_ref, wp_ref, gfin_ref, o_ref, carry_ref, *, tiles_per_seq, f_chunk, final):
    i = pl.program_id(0)
    mix = _dot(ya_ref[...].astype(BF16), wa_ref[...])
    mix = mix + _dot(yb_ref[...].astype(BF16), wb_ref[...])
    mix = mix + _dot(yc_ref[...].astype(BF16), wc_ref[...])
    x = x_ref[...] + mix

    tm = x.shape[0]
    d_ff = wdn_ref.shape[0]
    h = _rms(x, gffn_ref[...]).astype(BF16)
    first = (i % tiles_per_seq) == 0
    row = lax.broadcasted_iota(jnp.int32, (tm, f_chunk), 0)

    def conv(col0):
        u = _dot(h, wup_ref[:, col0:col0 + f_chunk])
        p2 = jnp.where(first, 0.0, carry_ref[6:7, col0:col0 + f_chunk])
        p1 = jnp.where(first, 0.0, carry_ref[7:8, col0:col0 + f_chunk])
        u1 = jnp.where(row == 0, p1, pltpu.roll(u, 1, axis=0))
        u2 = jnp.where(row == 0, p2, jnp.where(row == 1, p1, pltpu.roll(u, 2, axis=0)))
        carry_ref[:, col0:col0 + f_chunk] = u[tm - 8:tm, :]
        cw = cw_ref[:, col0:col0 + f_chunk]
        return cw[0:1] * u2 + cw[1:2] * u1 + cw[2:3] * u + cb_ref[:, col0:col0 + f_chunk]

    for c0 in range(0, d_ff, f_chunk):
        act = _gelu_tanh(conv(c0)) * conv(d_ff + c0)
        x = x + _dot(act.astype(BF16), wdn_ref[c0:c0 + f_chunk, :])

    gate = _sigmoid(_dot(_rms(x, gple_ref[...]).astype(BF16), wg_ref[...]))
    y = x + _dot(p_ref[...].astype(BF16), wp_ref[...]) * gate
    o_ref[...] = _rms(y, gfin_ref[...]) if final else y


def _layer_tail(x2, ya, yb, yc, p2, w_out, g_ffn, w_up, conv_w, conv_b, w_down, g_ple, w_gate, w_proj, g_final,
                seq_len, final, tm=512, f_chunk=1408):
    m, d = x2.shape
    f2 = w_up.shape[1]
    d_ff = f2 // 2
    pd = p2.shape[1]
    assert d_ff % f_chunk == 0 and f_chunk % LANE == 0 and seq_len % tm == 0
    bf = lambda w: w.astype(BF16)
    wa, wb, wc = bf(w_out[:RW_WIDTH]), bf(w_out[RW_WIDTH:RW_WIDTH + POOL_WIDTH]), bf(w_out[RW_WIDTH + POOL_WIDTH:])
    tile = lambda w: pl.BlockSpec((tm, w), lambda i: (i, 0))
    full = lambda shp: pl.BlockSpec(shp, lambda i: (0,) * len(shp))
    return pl.pallas_call(
        functools.partial(_tail_body, tiles_per_seq=seq_len // tm, f_chunk=f_chunk, final=final),
        grid=(m // tm,),
        in_specs=[tile(d), tile(RW_WIDTH), tile(POOL_WIDTH), tile(NSA_WIDTH), tile(pd),
                  full((RW_WIDTH, d)), full((POOL_WIDTH, d)), full((NSA_WIDTH, d)),
                  full((1, d)), full((d, f2)), full((CONV_W, f2)), full((1, f2)), full((d_ff, d)),
                  full((1, d)), full((d, d)), full((pd, d)), full((1, d))],
        out_specs=tile(d),
        out_shape=jax.ShapeDtypeStruct((m, d), F32),
        scratch_shapes=[pltpu.VMEM((8, f2), F32)],
        compiler_params=_cparams(("arbitrary",)),
        name="layer_tail_final" if final else "layer_tail",
    )(x2, ya, yb, yc, p2, wa, wb, wc, g_ffn.reshape(1, d), bf(w_up), conv_w.reshape(CONV_W, f2),
      conv_b.reshape(1, f2), bf(w_down), g_ple.reshape(1, d), bf(w_gate), bf(w_proj), g_final.reshape(1, d))


def _overlap_matrix(seq_len):
    ncp = seq_len // CMP_STRIDE
    n_cmp = (seq_len - CMP_BLOCK) // CMP_STRIDE + 1
    cs = CMP_STRIDE * np.arange(n_cmp)
    ss = SEL_BLOCK * np.arange(seq_len // SEL_BLOCK)
    ov = (np.minimum(cs[:, None] + CMP_BLOCK - 1, ss[None] + SEL_BLOCK - 1) - np.maximum(cs[:, None], ss[None]) + 1)
    m = np.zeros((NSEL_PAD, ncp), np.float32)
    m[:ss.size, :n_cmp] = (np.clip(ov, 0, CMP_BLOCK).astype(np.float32) / CMP_BLOCK).T
    return jnp.asarray(m, dtype=BF16)


def _compress_weight(w):
    halves = w.reshape(2, CMP_STRIDE, HEAD, HEAD)
    eye = jnp.eye(NSA_KV_HEADS, dtype=w.dtype)
    w2 = jnp.einsum('sldf,hg->lhdsgf', halves, eye)
    return w2.reshape(CMP_STRIDE * NSA_KV, 2 * NSA_KV).astype(BF16)


def _compress_pe(pe):
    halves = pe.reshape(2, CMP_STRIDE, 1, HEAD)
    return jnp.broadcast_to(halves, (2, CMP_STRIDE, NSA_KV_HEADS, HEAD)).reshape(2, CMP_STRIDE * NSA_KV)


def _in_proj_layout(w_in):
    d = w_in.shape[0]
    gate0 = RW_IN + POOL_WIDTH + NSA_WIDTH + 6 * NSA_KV
    per_head = NSA_GQA * 3
    gates = jnp.zeros((d, LANE), w_in.dtype)
    for h in range(NSA_KV_HEADS):
        gates = gates.at[:, 16 * h:16 * h + per_head].set(w_in[:, gate0 + per_head * h:gate0 + per_head * (h + 1)])
    return jnp.concatenate([w_in[:, :gate0], gates], axis=1).astype(BF16)


_IN_SPLITS = ((0, RW_IN), (RW_IN, RW_IN + POOL_WIDTH), (RW_IN + POOL_WIDTH, RW_IN + POOL_WIDTH + NSA_WIDTH)) + tuple(
    (RW_IN + POOL_WIDTH + NSA_WIDTH + i * NSA_KV, RW_IN + POOL_WIDTH + NSA_WIDTH + (i + 1) * NSA_KV) for i in range(7))


def kernel(x, p, positions, g_mix, w_in, rw_mu, rw_w0, rw_w_up, rw_a0, rw_a_up, rw_g_up, rw_k_k, rw_k_a, rw_r_k, rw_gn_g, rw_gn_b, pool_w, pool_scale, nsa_pe_k, nsa_pe_v, nsa_w_ck, nsa_w_cv, w_out, g_ffn, ffn_w_up, ffn_conv_w, ffn_conv_b, ffn_w_down, g_ple, ple_w_gate, ple_w_proj, g_final):
    B, T, D = x.shape
    depth = w_in.shape[0]
    M = B * T

    half = HEAD // 2
    inv = ROPE_THETA ** (-jnp.arange(half, dtype=F32) / half)
    ang = positions.astype(F32)[..., None] * inv
    cos, sin = jnp.cos(ang), jnp.sin(ang)
    cos_t = jnp.concatenate([cos, cos] * (LANE // HEAD), axis=-1)
    sin_t = jnp.concatenate([-sin, sin] * (LANE // HEAD), axis=-1)
    mcs_t = _overlap_matrix(T)
    blk_onehot = (jnp.arange(T)[:, None] // SEL_BLOCK == jnp.arange(NSEL_PAD)[None, :]).astype(BF16)
    cmp_b, win_b = _nsa_masks(T)
    key_pad = (jnp.arange(LANE) == HEAD).astype(BF16)

    x2 = x.reshape(M, D)
    for i in range(depth):
        zs = _norm_matmul(x2, g_mix[i], _in_proj_layout(w_in[i]), _IN_SPLITS)
        z_rw, z_pool, z_q, z_kc, z_vc, z_ks, z_vs, z_kw, z_vw, z_g = (
            z.reshape(B, T, z.shape[1]) for z in zs)

        pm, qm, rh, y0, gate, bonus = _rw_chunks(z_rw, rw_mu[i], rw_w0[i], rw_w_up[i], rw_a0[i], rw_a_up[i],
                                                 rw_g_up[i], rw_k_k[i], rw_k_a[i], rw_r_k[i].reshape(-1))
        y_a = _rw_scan(pm, qm, rh, y0, gate, bonus, rw_gn_g[i], rw_gn_b[i])

        w_pool_bd = jax.scipy.linalg.block_diag(*[pool_w[i, gi] for gi in range(pool_w.shape[1])]).astype(BF16)
        y_b = _pool(z_pool, w_pool_bd, pool_scale[i])

        q_r, kc_r, ks_r, vs_t, kw_r, vw_t, g_t = _nsa_prep(z_q, z_kc, z_ks, z_vs, z_kw, z_vw, z_g, cos_t, sin_t)
        kcmp, vcmp_t = _nsa_compress(kc_r, z_vc, _compress_pe(nsa_pe_k[i]), _compress_pe(nsa_pe_v[i]),
                                     _compress_weight(nsa_w_ck[i]), _compress_weight(nsa_w_cv[i]))
        kw_pad = jnp.concatenate([jnp.broadcast_to(key_pad, (B, NSA_KV_HEADS, WINDOW, LANE)), kw_r], axis=2)
        vw_pad_t = jnp.pad(vw_t, ((0, 0), (0, 0), (0, 0), (WINDOW, 0)))
        y_c = _nsa_attention(q_r, kcmp, vcmp_t, ks_r, vs_t, kw_pad, vw_pad_t, g_t, mcs_t, blk_onehot, cmp_b, win_b)

        x2 = _layer_tail(x2, y_a.reshape(M, -1), y_b.reshape(M, -1), y_c.reshape(M, -1), p[i].reshape(M, -1),
                         w_out[i], g_ffn[i], ffn_w_up[i], ffn_conv_w[i], ffn_conv_b[i], ffn_w_down[i],
                         g_ple[i], ple_w_gate[i], ple_w_proj[i], g_final, T, final=(i == depth - 1))
    return x2.reshape(B, T, D)
```

```python
import functools
import math

import numpy as np
import jax
import jax.numpy as jnp
from jax import lax
from jax.experimental import pallas as pl
from jax.experimental.pallas import tpu as pltpu

F32 = jnp.float32
BF16 = jnp.bfloat16

RMS_EPS = 1e-6
ROPE_THETA = 10000.0
HEAD = 64
RW_HEADS = 4
RW_WIDTH = RW_HEADS * HEAD
RW_DECAY_LORA = 64
RW_AAA_LORA = 64
RW_GATE_LORA = 128
RW_GN_EPS = 64e-5
RW_IN = 3 * RW_WIDTH + RW_DECAY_LORA + RW_AAA_LORA + RW_GATE_LORA
POOL_WIDTH = 256
POOL_WINDOWS = (2, 4, 8, 16)
POOL_HALO = 16
NSA_Q_HEADS = 8
NSA_KV_HEADS = 2
NSA_GQA = NSA_Q_HEADS // NSA_KV_HEADS
NSA_WIDTH = NSA_Q_HEADS * HEAD
NSA_KV = NSA_KV_HEADS * HEAD
CMP_BLOCK = 32
CMP_STRIDE = 16
SEL_BLOCK = 64
SEL_TOPN = 16
WINDOW = 512
Q_BLOCK = 128
KEY_GROUP = 512
V_ROWS = HEAD + 16
NSEL_PAD = 128
NEG = -1e30
LOG2E = math.log2(math.e)
CONV_W = 3
LANE = 128
VMEM_LIMIT = 56 * 1024 * 1024

RW_CHUNK = 64


def _cparams(sem):
    return pltpu.CompilerParams(dimension_semantics=sem, vmem_limit_bytes=VMEM_LIMIT)


def _dot(a, b, prec=None):
    return lax.dot_general(a, b, (((1,), (0,)), ((), ())), precision=prec, preferred_element_type=F32)


def _dot_nt(a, b, prec=None):
    return lax.dot_general(a, b, (((1,), (1,)), ((), ())), precision=prec, preferred_element_type=F32)


def _sigmoid(x):
    return 1.0 / (1.0 + jnp.exp(-x))


def _rms(x, g):
    ms = jnp.mean(x * x, axis=-1, keepdims=True)
    return (x * lax.rsqrt(ms + RMS_EPS)) * g


def _norm_mm_body(x_ref, g_ref, w_ref, *o_refs, splits):
    h = _rms(x_ref[...], g_ref[...]).astype(BF16)
    for o_ref, (lo, hi) in zip(o_refs, splits):
        o_ref[...] = _dot(h, w_ref[:, lo:hi])


def _norm_matmul(x2, g, w_bf16, splits, tm=1024):
    m, d = x2.shape
    n = w_bf16.shape[1]
    outs = tuple(jax.ShapeDtypeStruct((m, hi - lo), F32) for lo, hi in splits)
    return pl.pallas_call(
        functools.partial(_norm_mm_body, splits=splits),
        grid=(m // tm,),
        in_specs=[pl.BlockSpec((tm, d), lambda i: (i, 0)),
                  pl.BlockSpec((1, d), lambda i: (0, 0)),
                  pl.BlockSpec((d, n), lambda i: (0, 0))],
        out_specs=tuple(pl.BlockSpec((tm, hi - lo), lambda i: (i, 0)) for lo, hi in splits),
        out_shape=outs,
        compiler_params=_cparams(("parallel",)),
        name="norm_in_proj",
    )(x2, g.reshape(1, d), w_bf16)


def _split(x):
    hi = x.astype(BF16)
    return hi, (x - hi.astype(F32)).astype(BF16)


def _dot3(a, b, dot=_dot):
    return dot(a[0], b[0]) + (dot(a[0], b[1]) + dot(a[1], b[0]))


def _dot_ones(x, ones_bf16):
    hi, lo = _split(x)
    return _dot(hi, ones_bf16) + _dot(lo, ones_bf16)


def _head_ones():
    hr = lax.broadcasted_iota(jnp.int32, (RW_WIDTH, RW_WIDTH), 0) // HEAD
    hc = lax.broadcasted_iota(jnp.int32, (RW_WIDTH, RW_WIDTH), 1) // HEAD
    return hr == hc


def _rw_chunk_body(z_ref, zp_ref, mu_ref, w0_ref, wup_ref, a0_ref, aup_ref, gup_ref, kk_ref, ka_ref,
                   rk_ref, p_ref, q_ref, rh_ref, y0_ref, g_ref, bonus_ref):
    i = pl.program_id(1)
    C = RW_CHUNK
    W = RW_WIDTH
    z = z_ref[0]
    rows = z.shape[0]
    n_chunks = rows // C
    prev = jnp.where(i == 0, 0.0, zp_ref[0, 7:8, :])
    row_z = lax.broadcasted_iota(jnp.int32, z.shape, 0)
    zs = jnp.where(row_z == 0, prev, pltpu.roll(z, 1, axis=0))
    zf = z + (zs - z) * mu_ref[...]
    r = zf[:, 0:W]
    k = zf[:, W:2 * W]
    v = zf[:, 2 * W:3 * W]
    zw = zf[:, 3 * W:3 * W + RW_DECAY_LORA]
    za = zf[:, 3 * W + RW_DECAY_LORA:3 * W + RW_DECAY_LORA + RW_AAA_LORA]
    zg = zf[:, 3 * W + RW_DECAY_LORA + RW_AAA_LORA:]

    lora = lambda x, w_ref: _dot3(_split(x), (w_ref[0], w_ref[1]))
    xw = -(w0_ref[...] + lora(jnp.tanh(zw), wup_ref))
    softplus = jnp.maximum(xw, 0.0) + jnp.log(1.0 + jnp.exp(-jnp.abs(xw)))
    ld = -jnp.exp(-softplus - 0.5)
    a = _sigmoid(a0_ref[...] + lora(za, aup_ref))
    g_ref[0] = lora(_sigmoid(zg), gup_ref)
    kk = k * kk_ref[...]
    k2 = k * (1.0 + (a - 1.0) * ka_ref[...])

    same_head = _head_ones()
    head_ones = jnp.where(same_head, 1.0, 0.0).astype(BF16)
    kkn = kk / jnp.maximum(jnp.sqrt(_dot_ones(kk * kk, head_ones)), 1e-12)
    bonus_ref[0] = _dot_ones(r * k2 * rk_ref[...], head_ones) * v

    t_in = lax.broadcasted_iota(jnp.int32, (rows, W), 0) % C
    lc = ld
    shift = 1
    while shift < C:
        lc = lc + jnp.where(t_in >= shift, pltpu.roll(lc, shift, axis=0), 0.0)
        shift *= 2

    tr = lax.broadcasted_iota(jnp.int32, (W, W), 0) % C
    tc = lax.broadcasted_iota(jnp.int32, (W, W), 1) % C
    lower_incl = tr >= tc
    lower_strict = tr > tc
    eye = jnp.where(same_head, jnp.where(tr == tc, 1.0, 0.0), 0.0)

    def embed(x):
        return jnp.where(same_head, jnp.concatenate([x] * RW_HEADS, axis=0), 0.0)

    bf = lambda x: x.astype(BF16)
    A, Bt, Kt, Rt, V, BhT, KhT, gdiag, Rt_raw = [], [], [], [], [], [], [], [], []
    for c in range(n_chunks):
        rs = slice(c * C, (c + 1) * C)
        lc_c, ld_c = lc[rs], ld[rs]
        lc_end = lc_c[C - 1:C, :]
        ginv = jnp.exp(-lc_c)
        dec_end = jnp.exp(lc_end - lc_c)
        kkn_c, a_c, k2_c = kkn[rs], a[rs], k2[rs]
        r_t = r[rs] * jnp.exp(lc_c)
        A.append(bf(embed(-kkn_c * jnp.exp(lc_c - ld_c))))
        Bt.append(bf(embed(kkn_c * a_c * ginv)))
        Kt.append(bf(embed(k2_c * ginv)))
        Rt.append(bf(embed(r_t)))
        Rt_raw.append(r_t)
        V.append(bf(embed(v[rs])))
        BhT.append(bf(embed(kkn_c * a_c * dec_end).T))
        KhT.append(bf(embed(k2_c * dec_end).T))
        gdiag.append(jnp.where(eye > 0.5, jnp.broadcast_to(jnp.exp(lc_end), (W, W)), 0.0))

    each = lambda f, *lists: [f(*xs) for xs in zip(*lists)]
    L = each(lambda x, y: jnp.where(lower_strict, _dot_nt(x, y), 0.0), A, Bt)
    Lak = each(lambda x, y: bf(jnp.where(lower_strict, _dot_nt(x, y), 0.0)), A, Kt)
    Mrb = each(lambda x, y: bf(jnp.where(lower_incl, _dot_nt(x, y), 0.0)), Rt, Bt)
    Mrk = each(lambda x, y: bf(jnp.where(lower_incl, _dot_nt(x, y), 0.0)), Rt, Kt)
    Tm = [eye + x for x in L]
    Pw = each(bf, L)
    for _ in range(int(math.log2(C)) - 1):
        Pw = each(lambda x: bf(_dot(x, x)), Pw)
        Tm = each(lambda t, x: t + _dot(bf(t), x), Tm, Pw)
    Tm = each(bf, Tm)
    Wm = each(lambda t, x: bf(_dot(t, x)), Tm, A)
    LakV = each(lambda x, y: bf(_dot(x, y)), Lak, V)
    U0 = each(lambda t, x: bf(_dot(t, x)), Tm, LakV)
    Pm = each(lambda d, x, y: d + _dot(x, y), gdiag, BhT, Wm)
    Qm = each(lambda x, y, s, t: _dot(x, y) + _dot(s, t), BhT, U0, KhT, V)
    Rh = each(_dot, Mrb, Wm)
    Y0 = each(lambda x, y, s, t: _dot(x, y) + _dot(s, t), Mrb, U0, Mrk, V)

    def flatten(x_bd):
        out = x_bd[0:C]
        for h in range(1, RW_HEADS):
            out = out + x_bd[h * C:(h + 1) * C]
        return out

    for c in range(n_chunks):
        rs = slice(c * C, (c + 1) * C)
        p_hi, p_lo = _split(Pm[c])
        p_ref[0, c, 0] = p_hi
        p_ref[0, c, 1] = p_lo
        q_ref[0, c] = Qm[c]
        rh_ref[0, rs, :] = Rt_raw[c] + flatten(Rh[c])
        y0_ref[0, rs, :] = flatten(Y0[c])


def _rw_chunks(z_rw, mu, w0, w_up, a0, a_up, g_up, k_k, k_a, r_k, chunks_per_step=8):
    B, T, _ = z_rw.shape
    C = RW_CHUNK
    assert C == HEAD, "the chunk build shares one index grid between time and head-dim masks"
    rows = C * chunks_per_step
    nc = T // C
    row = lambda a: a.reshape(1, -1)
    pair = lambda w: jnp.stack(_split(w))
    full = lambda shp: pl.BlockSpec(shp, lambda b, i: (0,) * len(shp))
    seq = lambda w: pl.BlockSpec((1, rows, w), lambda b, i: (b, i, 0))
    sds = jax.ShapeDtypeStruct
    seq_shape = sds((B, T, RW_WIDTH), F32)
    return pl.pallas_call(
        _rw_chunk_body,
        grid=(B, T // rows),
        in_specs=[seq(RW_IN),
                  pl.BlockSpec((1, 8, RW_IN), lambda b, i: (b, jnp.maximum(i * (rows // 8) - 1, 0), 0)),
                  full((1, RW_IN)), full((1, RW_WIDTH)), full((2, RW_DECAY_LORA, RW_WIDTH)),
                  full((1, RW_WIDTH)), full((2, RW_AAA_LORA, RW_WIDTH)), full((2, RW_GATE_LORA, RW_WIDTH)),
                  full((1, RW_WIDTH)), full((1, RW_WIDTH)), full((1, RW_WIDTH))],
        out_specs=(pl.BlockSpec((1, chunks_per_step, 2, RW_WIDTH, RW_WIDTH), lambda b, i: (b, i, 0, 0, 0)),
                   pl.BlockSpec((1, chunks_per_step, RW_WIDTH, RW_WIDTH), lambda b, i: (b, i, 0, 0)),
                   seq(RW_WIDTH), seq(RW_WIDTH), seq(RW_WIDTH), seq(RW_WIDTH)),
        out_shape=(sds((B, nc, 2, RW_WIDTH, RW_WIDTH), BF16), sds((B, nc, RW_WIDTH, RW_WIDTH), F32),
                   seq_shape, seq_shape, seq_shape, seq_shape),
        compiler_params=_cparams(("parallel", "parallel")),
        name="rwkv_chunk_build",
    )(z_rw, z_rw, row(mu), row(w0), pair(w_up), row(a0), pair(a_up), pair(g_up), row(k_k), row(k_a), row(r_k))


def _rw_scan_body(p_ref, q_ref, rh_ref, y0_ref, g_ref, bonus_ref, gng_ref, gnb_ref, o_ref, h_ref):
    @pl.when(pl.program_id(0) == 0)
    def _():
        h_ref[...] = jnp.zeros_like(h_ref)

    nb = h_ref.shape[0]
    head_ones = jnp.where(_head_ones(), 1.0, 0.0).astype(BF16)
    hs = [_split(h_ref[b]) for b in range(nb)]
    ys = [_dot3(_split(rh_ref[b]), hs[b]) + y0_ref[b] for b in range(nb)]
    for b in range(nb):
        h_ref[b] = _dot3((p_ref[b, 0, 0], p_ref[b, 0, 1]), hs[b]) + q_ref[b, 0]
    inv_n = 1.0 / HEAD
    for b in range(nb):
        y = ys[b]
        d = y - _dot_ones(y, head_ones) * inv_n
        var = _dot_ones(d * d, head_ones) * inv_n
        yn = d * lax.rsqrt(var + RW_GN_EPS) * gng_ref[...] + gnb_ref[...]
        o_ref[b] = (yn + bonus_ref[b]) * g_ref[b]


def _rw_scan(p, q, rh, y0, g, bonus, gn_g, gn_b):
    B, T, _ = rh.shape
    C = RW_CHUNK
    nc = T // C
    seq = pl.BlockSpec((B, C, RW_WIDTH), lambda c: (0, c, 0))
    vec = pl.BlockSpec((1, RW_WIDTH), lambda c: (0, 0))
    return pl.pallas_call(
        _rw_scan_body,
        grid=(nc,),
        in_specs=[pl.BlockSpec((B, 1, 2, RW_WIDTH, RW_WIDTH), lambda c: (0, c, 0, 0, 0)),
                  pl.BlockSpec((B, 1, RW_WIDTH, RW_WIDTH), lambda c: (0, c, 0, 0)),
                  seq, seq, seq, seq, vec, vec],
        out_specs=seq,
        out_shape=jax.ShapeDtypeStruct((B, T, RW_WIDTH), F32),
        scratch_shapes=[pltpu.VMEM((B, RW_WIDTH, RW_WIDTH), F32)],
        compiler_params=_cparams(("arbitrary",)),
        name="rwkv_chunk_scan",
    )(p, q, rh, y0, g, bonus, gn_g.reshape(1, -1), gn_b.reshape(1, -1))


def _pool_body(z_ref, zp_ref, w_ref, scale_ref, o_ref):
    i = pl.program_id(1)
    z = z_ref[0]
    tt = z.shape[0]
    halo = jnp.where(i == 0, 0.0, zp_ref[0])
    e = jnp.concatenate([halo, z], axis=0)
    sums = []
    s = e
    for shift in (1, 2, 4, 8):
        s = s + pltpu.roll(s, shift, axis=0)
        sums.append(s[POOL_HALO:, :])
    t_idx = i * tt + lax.broadcasted_iota(jnp.int32, (tt, POOL_WIDTH), 0)
    lane_group = lax.broadcasted_iota(jnp.int32, (tt, POOL_WIDTH), 1) // HEAD
    pooled = jnp.zeros_like(z)
    for gi, win in enumerate(POOL_WINDOWS):
        cnt = jnp.minimum(t_idx + 1, win).astype(F32)
        pooled = jnp.where(lane_group == gi, sums[gi] / cnt - z, pooled)
    o_ref[0] = _dot(pooled.astype(BF16), w_ref[...]) * scale_ref[...]


def _pool(z_pool, w_blockdiag_bf16, scale, tt=512):
    B, T, _ = z_pool.shape
    return pl.pallas_call(
        _pool_body,
        grid=(B, T // tt),
        in_specs=[pl.BlockSpec((1, tt, POOL_WIDTH), lambda b, i: (b, i, 0)),
                  pl.BlockSpec((1, POOL_HALO, POOL_WIDTH),
                               lambda b, i: (b, jnp.maximum(i * (tt // POOL_HALO) - 1, 0), 0)),
                  pl.BlockSpec((POOL_WIDTH, POOL_WIDTH), lambda b, i: (0, 0)),
                  pl.BlockSpec((1, POOL_WIDTH), lambda b, i: (0, 0))],
        out_specs=pl.BlockSpec((1, tt, POOL_WIDTH), lambda b, i: (b, i, 0)),
        out_shape=jax.ShapeDtypeStruct((B, T, POOL_WIDTH), F32),
        compiler_params=_cparams(("parallel", "parallel")),
        name="pool_mixer",
    )(z_pool, z_pool, w_blockdiag_bf16, scale.reshape(1, -1))


def _rope(x, cos_w, sin_w):
    w = x.shape[1]
    lane = lax.broadcasted_iota(jnp.int32, x.shape, 1) % HEAD
    partner = jnp.where(lane < HEAD // 2, pltpu.roll(x, w - HEAD // 2, axis=1), pltpu.roll(x, HEAD // 2, axis=1))
    return x * cos_w + partner * sin_w


def _head_lanes(a, h, dtype):
    part = a[:, h * HEAD:(h + 1) * HEAD]
    return jnp.concatenate([part, jnp.zeros_like(part)], axis=1).astype(dtype)


def _nsa_prep_body(zq_ref, zkc_ref, zks_ref, zvs_ref, zkw_ref, zvw_ref, zg_ref, cos_ref, sin_ref,
                   q_ref, kc_ref, ks_ref, vst_ref, kw_ref, vwt_ref, gt_ref):
    cos = cos_ref[0]
    sin = sin_ref[0]
    cos_q = jnp.concatenate([cos] * (NSA_WIDTH // LANE), axis=1)
    sin_q = jnp.concatenate([sin] * (NSA_WIDTH // LANE), axis=1)
    qr = _rope(zq_ref[0], cos_q, sin_q) * (HEAD ** -0.5 * LOG2E)
    pad_lane = lax.broadcasted_iota(jnp.int32, (qr.shape[0], LANE), 1) == HEAD
    for h in range(NSA_Q_HEADS):
        q_ref[0, h] = jnp.where(pad_lane, NEG, _head_lanes(qr, h, F32)).astype(BF16)
    kc_ref[0] = _rope(zkc_ref[0], cos, sin)
    ks = _rope(zks_ref[0], cos, sin)
    kw = _rope(zkw_ref[0], cos, sin)
    vst = zvs_ref[0].T
    vwt = zvw_ref[0].T
    for h in range(NSA_KV_HEADS):
        sl = slice(h * HEAD, (h + 1) * HEAD)
        ks_ref[0, h] = _head_lanes(ks, h, BF16)
        kw_ref[0, h] = _head_lanes(kw, h, BF16)
        tail = lax.broadcasted_iota(jnp.int32, (V_ROWS - HEAD, vst.shape[1]), 0)
        ones_row = jnp.where(tail == 0, 1.0, 0.0).astype(BF16)
        vst_ref[0, h, 0:HEAD, :] = vst[sl, :].astype(BF16)
        vst_ref[0, h, HEAD:V_ROWS, :] = ones_row
        vwt_ref[0, h, 0:HEAD, :] = vwt[sl, :].astype(BF16)
        vwt_ref[0, h, HEAD:V_ROWS, :] = ones_row
    gt = _sigmoid(zg_ref[0]).T
    for h in range(NSA_KV_HEADS):
        gt_ref[0, h] = gt[16 * h:16 * (h + 1), :]


def _nsa_prep(z_q, z_kc, z_ks, z_vs, z_kw, z_vw, z_g, cos_t, sin_t, tt=512):
    B, T, _ = z_q.shape
    seq = lambda w: pl.BlockSpec((1, tt, w), lambda b, i: (b, i, 0))
    hm = lambda nh: pl.BlockSpec((1, nh, tt, LANE), lambda b, i: (b, 0, i, 0))
    tr = lambda rows: pl.BlockSpec((1, NSA_KV_HEADS, rows, tt), lambda b, i: (b, 0, 0, i))
    sds = jax.ShapeDtypeStruct
    return pl.pallas_call(
        _nsa_prep_body,
        grid=(B, T // tt),
        in_specs=[seq(NSA_WIDTH)] + [seq(NSA_KV)] * 6 + [seq(LANE), seq(LANE)],
        out_specs=(hm(NSA_Q_HEADS), seq(NSA_KV), hm(NSA_KV_HEADS), tr(V_ROWS), hm(NSA_KV_HEADS), tr(V_ROWS), tr(16)),
        out_shape=(sds((B, NSA_Q_HEADS, T, LANE), BF16), sds((B, T, NSA_KV), F32),
                   sds((B, NSA_KV_HEADS, T, LANE), BF16), sds((B, NSA_KV_HEADS, V_ROWS, T), BF16),
                   sds((B, NSA_KV_HEADS, T, LANE), BF16), sds((B, NSA_KV_HEADS, V_ROWS, T), BF16),
                   sds((B, NSA_KV_HEADS, 16, T), F32)),
        compiler_params=_cparams(("parallel", "parallel")),
        name="nsa_prep",
    )(z_q, z_kc, z_ks, z_vs, z_kw, z_vw, z_g, cos_t, sin_t)


def _nsa_cmp_body(xk_ref, xv_ref, pek_ref, pev_ref, wk_ref, wv_ref, kc_ref, vct_ref):
    half = NSA_KV
    n = kc_ref.shape[2]

    def compress(x_ref, pe_ref, w_ref):
        lo = hi = None
        for l in range(CMP_STRIDE):
            x = x_ref[0, pl.ds(l, n, stride=CMP_STRIDE), :]
            rows = slice(l * NSA_KV, (l + 1) * NSA_KV)
            d_lo = _dot((x + pe_ref[0:1, rows]).astype(BF16), w_ref[rows, 0:half])
            d_hi = _dot((x + pe_ref[1:2, rows]).astype(BF16), w_ref[rows, half:2 * half])
            lo = d_lo if lo is None else lo + d_lo
            hi = d_hi if hi is None else hi + d_hi
        hi_next = pltpu.roll(hi, n - 1, axis=0)
        row = lax.broadcasted_iota(jnp.int32, lo.shape, 0)
        return lo + jnp.where(row == n - 1, 0.0, hi_next)

    kc = compress(xk_ref, pek_ref, wk_ref)
    vct = compress(xv_ref, pev_ref, wv_ref).T
    for h in range(NSA_KV_HEADS):
        sl = slice(h * HEAD, (h + 1) * HEAD)
        kc_ref[0, h] = _head_lanes(kc, h, BF16)
        vct_ref[0, h] = vct[sl, :].astype(BF16)


def _nsa_compress(kc_tokens, vc_tokens, pe_k2, pe_v2, wk2, wv2):
    B, T, _ = kc_tokens.shape
    ncp, gw = T // CMP_STRIDE, CMP_STRIDE * NSA_KV
    grp = pl.BlockSpec((1, T, NSA_KV), lambda b: (b, 0, 0))
    pe = pl.BlockSpec((2, gw), lambda b: (0, 0))
    wspec = pl.BlockSpec((gw, 2 * NSA_KV), lambda b: (0, 0))
    sds = jax.ShapeDtypeStruct
    return pl.pallas_call(
        _nsa_cmp_body,
        grid=(B,),
        in_specs=[grp, grp, pe, pe, wspec, wspec],
        out_specs=(pl.BlockSpec((1, NSA_KV_HEADS, ncp, LANE), lambda b: (b, 0, 0, 0)),
                   pl.BlockSpec((1, NSA_KV_HEADS, HEAD, ncp), lambda b: (b, 0, 0, 0))),
        out_shape=(sds((B, NSA_KV_HEADS, ncp, LANE), BF16), sds((B, NSA_KV_HEADS, HEAD, ncp), BF16)),
        compiler_params=_cparams(("parallel",)),
        name="nsa_compress",
    )(kc_tokens, vc_tokens, pe_k2, pe_v2, wk2, wv2)


def _nsa_attn_body(q_ref, kc_ref, vct_ref, ks_ref, vst_ref, kw_ref, vwt_ref, gt_ref, mcs_ref, blk_ref, cmpb_ref,
                   winb_ref, o_ref, *score_refs):
    jb = pl.program_id(1)
    s0 = jb * Q_BLOCK
    cols = NSA_GQA * Q_BLOCK
    heads = range(NSA_KV_HEADS)
    q = [q_ref[0, NSA_GQA * h:NSA_GQA * (h + 1)].reshape(cols, LANE) for h in heads]
    ncp = kc_ref.shape[2]
    per_gqa = lambda x: jnp.concatenate([x] * NSA_GQA, axis=1)
    col_max = lambda x: jnp.max(x, axis=0, keepdims=True)

    cmp_mask = per_gqa(cmpb_ref[pl.ds(pl.multiple_of(ncp - s0 // CMP_STRIDE, 8), ncp), :])
    t_c = s0 + lax.broadcasted_iota(jnp.int32, (1, cols), 1) % Q_BLOCK
    any_valid = jnp.where(t_c >= CMP_BLOCK - 1, 1.0, 0.0)
    mcs = mcs_ref[...]
    sc = [_dot_nt(kc_ref[0, h], q[h]) + cmp_mask for h in heads]
    e = [jnp.exp2(x - col_max(x)) for x in sc]
    pc = [x * (any_valid / jnp.sum(x, axis=0, keepdims=True)) for x in e]
    o_cmp = [_dot(vct_ref[0, h], pc[h].astype(BF16)) for h in heads]

    def importance(p):
        psum = p[:, 0:Q_BLOCK]
        for g in range(1, NSA_GQA):
            psum = psum + p[:, g * Q_BLOCK:(g + 1) * Q_BLOCK]
        p_hi = psum.astype(BF16)
        rest = psum - p_hi.astype(F32)
        p_mid = rest.astype(BF16)
        p_lo = (rest - p_mid.astype(F32)).astype(BF16)
        return _dot(mcs, p_hi) + (_dot(mcs, p_mid) + _dot(mcs, p_lo))

    imp = [importance(p) for p in pc]

    wkeys = WINDOW + Q_BLOCK
    off_w = pl.multiple_of(s0, Q_BLOCK)
    tri_old, tri_new = per_gqa(winb_ref[0]), per_gqa(winb_ref[1])
    sw = [_dot_nt(kw_ref[0, h, pl.ds(off_w, wkeys), :], q[h]) for h in heads]
    sw = [jnp.concatenate([x[0:Q_BLOCK] + tri_old, x[Q_BLOCK:WINDOW], x[WINDOW:wkeys] + tri_new], axis=0) for x in sw]
    pw = [jnp.exp2((x - col_max(x)).astype(BF16)) for x in sw]
    acc_w = [_dot(vwt_ref[0, h, :, pl.ds(off_w, wkeys)], pw[h]) for h in heads]

    j_idx = lax.broadcasted_iota(jnp.int32, (NSEL_PAD, Q_BLOCK), 0)
    jc = (s0 + lax.broadcasted_iota(jnp.int32, (NSEL_PAD, Q_BLOCK), 1)) // SEL_BLOCK
    causal = j_idx <= jc
    taken = -jnp.inf
    forced = lambda x: jnp.where(j_idx == 0, taken, jnp.where(j_idx == jc, taken, jnp.where(j_idx == jc - 1, taken, x)))
    work = [jnp.where(causal, forced(x), NEG) for x in imp]
    j_f = j_idx.astype(F32)
    for _ in range(SEL_TOPN - 3):
        top = [col_max(w) for w in work]
        first = [jnp.min(jnp.where(w == t, j_f, float(NSEL_PAD)), axis=0, keepdims=True) for w, t in zip(work, top)]
        work = [jnp.where(j_f == f, -jnp.inf, w) for w, f in zip(work, first)]
    bias = [jnp.where(causal, jnp.where(w == -jnp.inf, 0.0, NEG), NEG) for w in work]

    q_aug = [jnp.concatenate([q[h], jnp.concatenate([bias[h].T.astype(BF16)] * NSA_GQA, axis=0)], axis=1)
             for h in heads]

    def sel_scores(h, off):
        k_aug = jnp.concatenate([ks_ref[0, h, pl.ds(off, KEY_GROUP), :], blk_ref[pl.ds(off, KEY_GROUP), :]], axis=1)
        return _dot_nt(k_aug, q_aug[h])

    def flash_update(v_t, st, carry):
        m, acc = carry
        m_new = jnp.maximum(m, col_max(st))
        p = jnp.exp2((st - m_new).astype(BF16))
        return m_new, acc * jnp.exp2(m - m_new) + _dot(v_t, p)

    init = (jnp.full((1, cols), NEG, F32), jnp.zeros((V_ROWS, cols), F32))
    n_groups = s0 // KEY_GROUP + 1
    n_pairs = n_groups // 2
    t_q = s0 + lax.broadcasted_iota(jnp.int32, (Q_BLOCK, cols), 1) % Q_BLOCK
    r_q = lax.broadcasted_iota(jnp.int32, (Q_BLOCK, cols), 0)

    def put_scores(ref, h, g):
        off = pl.multiple_of(g * KEY_GROUP, KEY_GROUP)
        ref[...] = sel_scores(h, off)
        r = pl.multiple_of(jnp.clip(s0 - off, 0, KEY_GROUP - Q_BLOCK), Q_BLOCK)
        tile = ref[pl.ds(r, Q_BLOCK), :]
        ref[pl.ds(r, Q_BLOCK), :] = jnp.where(off + r + r_q <= t_q, tile, NEG)

    def consume(ref, h, g, carry):
        off = pl.multiple_of(g * KEY_GROUP, KEY_GROUP)
        return flash_update(vst_ref[0, h, :, pl.ds(off, KEY_GROUP)], ref[...], carry)

    last = n_groups - 1
    buf0 = [(score_refs[4 * h], score_refs[4 * h + 1]) for h in heads]
    buf1 = [(score_refs[4 * h + 2], score_refs[4 * h + 3]) for h in heads]
    for h in heads:
        put_scores(buf0[h][0], h, 0)
    for h in heads:
        put_scores(buf0[h][1], h, jnp.minimum(1, last))

    def stage(cur, nxt, pair, carries):
        for h in heads:
            put_scores(nxt[h][0], h, jnp.minimum(2 * pair + 2, last))
        carries = [consume(cur[h][0], h, 2 * pair, carries[h]) for h in heads]
        for h in heads:
            put_scores(nxt[h][1], h, jnp.minimum(2 * pair + 3, last))
        return tuple(consume(cur[h][1], h, 2 * pair + 1, carries[h]) for h in heads)

    def two_pairs(i, carries):
        return stage(buf1, buf0, 2 * i + 1, stage(buf0, buf1, 2 * i, carries))

    carries = lax.fori_loop(0, n_pairs // 2, two_pairs, tuple(init for _ in heads))
    odd_pairs = n_pairs % 2
    carries = lax.fori_loop(n_pairs - odd_pairs, n_pairs, lambda pr, c: stage(buf0, buf1, pr, c), carries)
    left = n_groups - 2 * n_pairs

    def leftover(buf):
        return lambda _, c: tuple(consume(buf[h][0], h, last, c[h]) for h in heads)

    carries = lax.fori_loop(0, left * (1 - odd_pairs), leftover(buf0), carries)
    carries = lax.fori_loop(0, left * odd_pairs, leftover(buf1), carries)

    for h in heads:
        _, acc_s = carries[h]

        def gate_row(branch):
            return jnp.concatenate([gt_ref[0, h, 3 * g + branch:3 * g + branch + 1, :] for g in range(NSA_GQA)], axis=1)

        o_t = (gate_row(0) * o_cmp[h] + (gate_row(1) / acc_s[HEAD:HEAD + 1]) * acc_s[0:HEAD]
               + (gate_row(2) / acc_w[h][HEAD:HEAD + 1]) * acc_w[h][0:HEAD])
        stacked = jnp.concatenate([o_t[:, g * Q_BLOCK:(g + 1) * Q_BLOCK] for g in range(NSA_GQA)], axis=0)
        o_ref[0, :, NSA_GQA * HEAD * h:NSA_GQA * HEAD * (h + 1)] = stacked.T


def _nsa_masks(seq_len):
    tq = np.arange(Q_BLOCK)[None, :]
    ncp = seq_len // CMP_STRIDE
    n_rel = np.arange(2 * ncp)[:, None] - ncp
    cmp_b = np.where(CMP_STRIDE * n_rel + CMP_BLOCK - 1 <= tq, 0.0, NEG).astype(np.float32)
    k_rel = np.arange(Q_BLOCK)[:, None]
    win_b = np.stack([np.where(k_rel > tq, 0.0, NEG), np.where(k_rel <= tq, 0.0, NEG)]).astype(np.float32)
    return jnp.asarray(cmp_b), jnp.asarray(win_b)


def _nsa_attention(q_r, kcmp, vcmp_t, ks_r, vs_t, kw_pad, vw_pad_t, g_t, mcs_t, blk_onehot, cmp_b, win_b):
    B, _, T, _ = q_r.shape
    ncp = kcmp.shape[2]
    nqb = T // Q_BLOCK
    tw = T + WINDOW
    nkv = NSA_KV_HEADS
    assert T % KEY_GROUP == 0 and T // SEL_BLOCK <= NSEL_PAD and WINDOW > Q_BLOCK and WINDOW % Q_BLOCK == 0
    kv_rows = lambda n: pl.BlockSpec((1, nkv, n, LANE), lambda b, j: (b, 0, 0, 0))
    kv_cols = lambda n, rows: pl.BlockSpec((1, nkv, rows, n), lambda b, j: (b, 0, 0, 0))
    const = lambda shp: pl.BlockSpec(shp, lambda b, j: (0,) * len(shp))
    return pl.pallas_call(
        _nsa_attn_body,
        grid=(B, nqb),
        in_specs=[pl.BlockSpec((1, NSA_Q_HEADS, Q_BLOCK, LANE), lambda b, j: (b, 0, j, 0)),
                  kv_rows(ncp), kv_cols(ncp, HEAD), kv_rows(T), kv_cols(T, V_ROWS), kv_rows(tw), kv_cols(tw, V_ROWS),
                  pl.BlockSpec((1, nkv, 16, Q_BLOCK), lambda b, j: (b, 0, 0, j)),
                  const((NSEL_PAD, ncp)), const((T, NSEL_PAD)), const((2 * ncp, Q_BLOCK)),
                  const((2, Q_BLOCK, Q_BLOCK))],
        out_specs=pl.BlockSpec((1, Q_BLOCK, NSA_WIDTH), lambda b, j: (b, j, 0)),
        out_shape=jax.ShapeDtypeStruct((B, T, NSA_WIDTH), F32),
        scratch_shapes=[pltpu.VMEM((KEY_GROUP, NSA_GQA * Q_BLOCK), F32)] * (4 * nkv),
        compiler_params=_cparams(("parallel", "arbitrary")),
        name="nsa_attention",
    )(q_r, kcmp, vcmp_t, ks_r, vs_t, kw_pad, vw_pad_t, g_t, mcs_t, blk_onehot, cmp_b, win_b)


def _gelu_tanh(x):
    return 0.5 * x * (1.0 + jnp.tanh(math.sqrt(2.0 / math.pi) * (x + 0.044715 * (x * x * x))))


def _tail_body(x_ref, ya_ref, yb_ref, yc_ref, p_ref, wa_ref, wb_ref, wc_ref, gffn_ref, wup_ref, cw_ref, cb_ref,
               wdn_ref, gple_ref, wg_ref, wp_ref, gfin_ref, o_ref, carry_ref, *, tiles_per_seq, f_chunk, final):
    i = pl.program_id(0)
    mix = _dot(ya_ref[...].astype(BF16), wa_ref[...])
    mix = mix + _dot(yb_ref[...].astype(BF16), wb_ref[...])
    mix = mix + _dot(yc_ref[...].astype(BF16), wc_ref[...])
    x = x_ref[...] + mix

    tm = x.shape[0]
    d_ff = wdn_ref.shape[0]
    h = _rms(x, gffn_ref[...]).astype(BF16)
    first = (i % tiles_per_seq) == 0
    row = lax.broadcasted_iota(jnp.int32, (tm, f_chunk), 0)

    def conv(col0):
        u = _dot(h, wup_ref[:, col0:col0 + f_chunk])
        p2 = jnp.where(first, 0.0, carry_ref[6:7, col0:col0 + f_chunk])
        p1 = jnp.where(first, 0.0, carry_ref[7:8, col0:col0 + f_chunk])
        u1 = jnp.where(row == 0, p1, pltpu.roll(u, 1, axis=0))
        u2 = jnp.where(row == 0, p2, jnp.where(row == 1, p1, pltpu.roll(u, 2, axis=0)))
        carry_ref[:, col0:col0 + f_chunk] = u[tm - 8:tm, :]
        cw = cw_ref[:, col0:col0 + f_chunk]
        return cw[0:1] * u2 + cw[1:2] * u1 + cw[2:3] * u + cb_ref[:, col0:col0 + f_chunk]

    for c0 in range(0, d_ff, f_chunk):
        act = _gelu_tanh(conv(c0)) * conv(d_ff + c0)
        x = x + _dot(act.astype(BF16), wdn_ref[c0:c0 + f_chunk, :])

    gate = _sigmoid(_dot(_rms(x, gple_ref[...]).astype(BF16), wg_ref[...]))
    y = x + _dot(p_ref[...].astype(BF16), wp_ref[...]) * gate
    o_ref[...] = _rms(y, gfin_ref[...]) if final else y


def _layer_tail(x2, ya, yb, yc, p2, w_out, g_ffn, w_up, conv_w, conv_b, w_down, g_ple, w_gate, w_proj, g_final,
                seq_len, final, tm=512, f_chunk=1408):
    m, d = x2.shape
    f2 = w_up.shape[1]
    d_ff = f2 // 2
    pd = p2.shape[1]
    assert d_ff % f_chunk == 0 and f_chunk % LANE == 0 and seq_len % tm == 0
    bf = lambda w: w.astype(BF16)
    wa, wb, wc = bf(w_out[:RW_WIDTH]), bf(w_out[RW_WIDTH:RW_WIDTH + POOL_WIDTH]), bf(w_out[RW_WIDTH + POOL_WIDTH:])
    tile = lambda w: pl.BlockSpec((tm, w), lambda i: (i, 0))
    full = lambda shp: pl.BlockSpec(shp, lambda i: (0,) * len(shp))
    return pl.pallas_call(
        functools.partial(_tail_body, tiles_per_seq=seq_len // tm, f_chunk=f_chunk, final=final),
        grid=(m // tm,),
        in_specs=[tile(d), tile(RW_WIDTH), tile(POOL_WIDTH), tile(NSA_WIDTH), tile(pd),
                  full((RW_WIDTH, d)), full((POOL_WIDTH, d)), full((NSA_WIDTH, d)),
                  full((1, d)), full((d, f2)), full((CONV_W, f2)), full((1, f2)), full((d_ff, d)),
                  full((1, d)), full((d, d)), full((pd, d)), full((1, d))],
        out_specs=tile(d),
        out_shape=jax.ShapeDtypeStruct((m, d), F32),
        scratch_shapes=[pltpu.VMEM((8, f2), F32)],
        compiler_params=_cparams(("arbitrary",)),
        name="layer_tail_final" if final else "layer_tail",
    )(x2, ya, yb, yc, p2, wa, wb, wc, g_ffn.reshape(1, d), bf(w_up), conv_w.reshape(CONV_W, f2),
      conv_b.reshape(1, f2), bf(w_down), g_ple.reshape(1, d), bf(w_gate), bf(w_proj), g_final.reshape(1, d))


def _overlap_matrix(seq_len):
    ncp = seq_len // CMP_STRIDE
    n_cmp = (seq_len - CMP_BLOCK) // CMP_STRIDE + 1
    cs = CMP_STRIDE * np.arange(n_cmp)
    ss = SEL_BLOCK * np.arange(seq_len // SEL_BLOCK)
    ov = (np.minimum(cs[:, None] + CMP_BLOCK - 1, ss[None] + SEL_BLOCK - 1) - np.maximum(cs[:, None], ss[None]) + 1)
    m = np.zeros((NSEL_PAD, ncp), np.float32)
    m[:ss.size, :n_cmp] = (np.clip(ov, 0, CMP_BLOCK).astype(np.float32) / CMP_BLOCK).T
    return jnp.asarray(m, dtype=BF16)


def _compress_weight(w):
    halves = w.reshape(2, CMP_STRIDE, HEAD, HEAD)
    eye = jnp.eye(NSA_KV_HEADS, dtype=w.dtype)
    w2 = jnp.einsum('sldf,hg->lhdsgf', halves, eye)
    return w2.reshape(CMP_STRIDE * NSA_KV, 2 * NSA_KV).astype(BF16)


def _compress_pe(pe):
    halves = pe.reshape(2, CMP_STRIDE, 1, HEAD)
    return jnp.broadcast_to(halves, (2, CMP_STRIDE, NSA_KV_HEADS, HEAD)).reshape(2, CMP_STRIDE * NSA_KV)


def _in_proj_layout(w_in):
    d = w_in.shape[0]
    gate0 = RW_IN + POOL_WIDTH + NSA_WIDTH + 6 * NSA_KV
    per_head = NSA_GQA * 3
    gates = jnp.zeros((d, LANE), w_in.dtype)
    for h in range(NSA_KV_HEADS):
        gates = gates.at[:, 16 * h:16 * h + per_head].set(w_in[:, gate0 + per_head * h:gate0 + per_head * (h + 1)])
    return jnp.concatenate([w_in[:, :gate0], gates], axis=1).astype(BF16)


_IN_SPLITS = ((0, RW_IN), (RW_IN, RW_IN + POOL_WIDTH), (RW_IN + POOL_WIDTH, RW_IN + POOL_WIDTH + NSA_WIDTH)) + tuple(
    (RW_IN + POOL_WIDTH + NSA_WIDTH + i * NSA_KV, RW_IN + POOL_WIDTH + NSA_WIDTH + (i + 1) * NSA_KV) for i in range(7))


def kernel(x, p, positions, g_mix, w_in, rw_mu, rw_w0, rw_w_up, rw_a0, rw_a_up, rw_g_up, rw_k_k, rw_k_a, rw_r_k, rw_gn_g, rw_gn_b, pool_w, pool_scale, nsa_pe_k, nsa_pe_v, nsa_w_ck, nsa_w_cv, w_out, g_ffn, ffn_w_up, ffn_conv_w, ffn_conv_b, ffn_w_down, g_ple, ple_w_gate, ple_w_proj, g_final):
    B, T, D = x.shape
    depth = w_in.shape[0]
    M = B * T

    half = HEAD // 2
    inv = ROPE_THETA ** (-jnp.arange(half, dtype=F32) / half)
    ang = positions.astype(F32)[..., None] * inv
    cos, sin = jnp.cos(ang), jnp.sin(ang)
    cos_t = jnp.concatenate([cos, cos] * (LANE // HEAD), axis=-1)
    sin_t = jnp.concatenate([-sin, sin] * (LANE // HEAD), axis=-1)
    mcs_t = _overlap_matrix(T)
    blk_onehot = (jnp.arange(T)[:, None] // SEL_BLOCK == jnp.arange(NSEL_PAD)[None, :]).astype(BF16)
    cmp_b, win_b = _nsa_masks(T)
    key_pad = (jnp.arange(LANE) == HEAD).astype(BF16)

    x2 = x.reshape(M, D)
    for i in range(depth):
        zs = _norm_matmul(x2, g_mix[i], _in_proj_layout(w_in[i]), _IN_SPLITS)
        z_rw, z_pool, z_q, z_kc, z_vc, z_ks, z_vs, z_kw, z_vw, z_g = (
            z.reshape(B, T, z.shape[1]) for z in zs)

        pm, qm, rh, y0, gate, bonus = _rw_chunks(z_rw, rw_mu[i], rw_w0[i], rw_w_up[i], rw_a0[i], rw_a_up[i],
                                                 rw_g_up[i], rw_k_k[i], rw_k_a[i], rw_r_k[i].reshape(-1))
        y_a = _rw_scan(pm, qm, rh, y0, gate, bonus, rw_gn_g[i], rw_gn_b[i])

        w_pool_bd = jax.scipy.linalg.block_diag(*[pool_w[i, gi] for gi in range(pool_w.shape[1])]).astype(BF16)
        y_b = _pool(z_pool, w_pool_bd, pool_scale[i])

        q_r, kc_r, ks_r, vs_t, kw_r, vw_t, g_t = _nsa_prep(z_q, z_kc, z_ks, z_vs, z_kw, z_vw, z_g, cos_t, sin_t)
        kcmp, vcmp_t = _nsa_compress(kc_r, z_vc, _compress_pe(nsa_pe_k[i]), _compress_pe(nsa_pe_v[i]),
                                     _compress_weight(nsa_w_ck[i]), _compress_weight(nsa_w_cv[i]))
        kw_pad = jnp.concatenate([jnp.broadcast_to(key_pad, (B, NSA_KV_HEADS, WINDOW, LANE)), kw_r], axis=2)
        vw_pad_t = jnp.pad(vw_t, ((0, 0), (0, 0), (0, 0), (WINDOW, 0)))
        y_c = _nsa_attention(q_r, kcmp, vcmp_t, ks_r, vs_t, kw_pad, vw_pad_t, g_t, mcs_t, blk_onehot, cmp_b, win_b)

        x2 = _layer_tail(x2, y_a.reshape(M, -1), y_b.reshape(M, -1), y_c.reshape(M, -1), p[i].reshape(M, -1),
                         w_out[i], g_ffn[i], ffn_w_up[i], ffn_conv_w[i], ffn_conv_b[i], ffn_w_down[i],
                         g_ple[i], ple_w_gate[i], ple_w_proj[i], g_final, T, final=(i == depth - 1))
    return x2.reshape(B, T, D)
```

```python
import functools
import math

import numpy as np
import jax
import jax.numpy as jnp
from jax import lax
from jax.experimental import pallas as pl
from jax.experimental.pallas import tpu as pltpu

F32 = jnp.float32
BF16 = jnp.bfloat16

RMS_EPS = 1e-6
ROPE_THETA = 10000.0
HEAD = 64
RW_HEADS = 4
RW_WIDTH = RW_HEADS * HEAD
RW_DECAY_LORA = 64
RW_AAA_LORA = 64
RW_GATE_LORA = 128
RW_GN_EPS = 64e-5
RW_IN = 3 * RW_WIDTH + RW_DECAY_LORA + RW_AAA_LORA + RW_GATE_LORA
POOL_WIDTH = 256
POOL_WINDOWS = (2, 4, 8, 16)
POOL_HALO = 16
NSA_Q_HEADS = 8
NSA_KV_HEADS = 2
NSA_GQA = NSA_Q_HEADS // NSA_KV_HEADS
NSA_WIDTH = NSA_Q_HEADS * HEAD
NSA_KV = NSA_KV_HEADS * HEAD
CMP_BLOCK = 32
CMP_STRIDE = 16
SEL_BLOCK = 64
SEL_TOPN = 16
WINDOW = 512
Q_BLOCK = 128
KEY_GROUP = 512
V_ROWS = HEAD + 16
NSEL_PAD = 128
NEG = -1e30
LOG2E = math.log2(math.e)
CONV_W = 3
LANE = 128
VMEM_LIMIT = 56 * 1024 * 1024

RW_CHUNK = 64


def _cparams(sem):
    return pltpu.CompilerParams(dimension_semantics=sem, vmem_limit_bytes=VMEM_LIMIT)


def _dot(a, b, prec=None):
    return lax.dot_general(a, b, (((1,), (0,)), ((), ())), precision=prec, preferred_element_type=F32)


def _dot_nt(a, b, prec=None):
    return lax.dot_general(a, b, (((1,), (1,)), ((), ())), precision=prec, preferred_element_type=F32)


def _sigmoid(x):
    return 1.0 / (1.0 + jnp.exp(-x))


def _rms(x, g):
    ms = jnp.mean(x * x, axis=-1, keepdims=True)
    return (x * lax.rsqrt(ms + RMS_EPS)) * g


def _norm_mm_body(x_ref, g_ref, w_ref, *o_refs, splits):
    h = _rms(x_ref[...], g_ref[...]).astype(BF16)
    for o_ref, (lo, hi) in zip(o_refs, splits):
        o_ref[...] = _dot(h, w_ref[:, lo:hi])


def _norm_matmul(x2, g, w_bf16, splits, tm=1024):
    m, d = x2.shape
    n = w_bf16.shape[1]
    outs = tuple(jax.ShapeDtypeStruct((m, hi - lo), F32) for lo, hi in splits)
    return pl.pallas_call(
        functools.partial(_norm_mm_body, splits=splits),
        grid=(m // tm,),
        in_specs=[pl.BlockSpec((tm, d), lambda i: (i, 0)),
                  pl.BlockSpec((1, d), lambda i: (0, 0)),
                  pl.BlockSpec((d, n), lambda i: (0, 0))],
        out_specs=tuple(pl.BlockSpec((tm, hi - lo), lambda i: (i, 0)) for lo, hi in splits),
        out_shape=outs,
        compiler_params=_cparams(("parallel",)),
        name="norm_in_proj",
    )(x2, g.reshape(1, d), w_bf16)


def _split(x):
    hi = x.astype(BF16)
    return hi, (x - hi.astype(F32)).astype(BF16)


def _dot3(a, b, dot=_dot):
    return dot(a[0], b[0]) + (dot(a[0], b[1]) + dot(a[1], b[0]))


def _dot_ones(x, ones_bf16):
    hi, lo = _split(x)
    return _dot(hi, ones_bf16) + _dot(lo, ones_bf16)


def _head_ones():
    hr = lax.broadcasted_iota(jnp.int32, (RW_WIDTH, RW_WIDTH), 0) // HEAD
    hc = lax.broadcasted_iota(jnp.int32, (RW_WIDTH, RW_WIDTH), 1) // HEAD
    return hr == hc


def _rw_chunk_body(z_ref, zp_ref, mu_ref, w0_ref, wup_ref, a0_ref, aup_ref, gup_ref, kk_ref, ka_ref,
                   rk_ref, p_ref, q_ref, rh_ref, y0_ref, g_ref, bonus_ref):
    i = pl.program_id(1)
    C = RW_CHUNK
    W = RW_WIDTH
    z = z_ref[0]
    rows = z.shape[0]
    n_chunks = rows // C
    prev = jnp.where(i == 0, 0.0, zp_ref[0, 7:8, :])
    row_z = lax.broadcasted_iota(jnp.int32, z.shape, 0)
    zs = jnp.where(row_z == 0, prev, pltpu.roll(z, 1, axis=0))
    zf = z + (zs - z) * mu_ref[...]
    r = zf[:, 0:W]
    k = zf[:, W:2 * W]
    v = zf[:, 2 * W:3 * W]
    zw = zf[:, 3 * W:3 * W + RW_DECAY_LORA]
    za = zf[:, 3 * W + RW_DECAY_LORA:3 * W + RW_DECAY_LORA + RW_AAA_LORA]
    zg = zf[:, 3 * W + RW_DECAY_LORA + RW_AAA_LORA:]

    lora = lambda x, w_ref: _dot3(_split(x), (w_ref[0], w_ref[1]))
    xw = -(w0_ref[...] + lora(jnp.tanh(zw), wup_ref))
    softplus = jnp.maximum(xw, 0.0) + jnp.log(1.0 + jnp.exp(-jnp.abs(xw)))
    ld = -jnp.exp(-softplus - 0.5)
    a = _sigmoid(a0_ref[...] + lora(za, aup_ref))
    g_ref[0] = lora(_sigmoid(zg), gup_ref)
    kk = k * kk_ref[...]
    k2 = k * (1.0 + (a - 1.0) * ka_ref[...])

    same_head = _head_ones()
    head_ones = jnp.where(same_head, 1.0, 0.0).astype(BF16)
    kkn = kk / jnp.maximum(jnp.sqrt(_dot_ones(kk * kk, head_ones)), 1e-12)
    bonus_ref[0] = _dot_ones(r * k2 * rk_ref[...], head_ones) * v

    t_in = lax.broadcasted_iota(jnp.int32, (rows, W), 0) % C
    lc = ld
    shift = 1
    while shift < C:
        lc = lc + jnp.where(t_in >= shift, pltpu.roll(lc, shift, axis=0), 0.0)
        shift *= 2

    tr = lax.broadcasted_iota(jnp.int32, (W, W), 0) % C
    tc = lax.broadcasted_iota(jnp.int32, (W, W), 1) % C
    lower_incl = tr >= tc
    lower_strict = tr > tc
    eye = jnp.where(same_head, jnp.where(tr == tc, 1.0, 0.0), 0.0)

    def embed(x):
        return jnp.where(same_head, jnp.concatenate([x] * RW_HEADS, axis=0), 0.0)

    bf = lambda x: x.astype(BF16)
    A, Bt, Kt, Rt, V, BhT, KhT, gdiag, Rt_raw = [], [], [], [], [], [], [], [], []
    for c in range(n_chunks):
        rs = slice(c * C, (c + 1) * C)
        lc_c, ld_c = lc[rs], ld[rs]
        lc_end = lc_c[C - 1:C, :]
        ginv = jnp.exp(-lc_c)
        dec_end = jnp.exp(lc_end - lc_c)
        kkn_c, a_c, k2_c = kkn[rs], a[rs], k2[rs]
        r_t = r[rs] * jnp.exp(lc_c)
        A.append(bf(embed(-kkn_c * jnp.exp(lc_c - ld_c))))
        Bt.append(bf(embed(kkn_c * a_c * ginv)))
        Kt.append(bf(embed(k2_c * ginv)))
        Rt.append(bf(embed(r_t)))
        Rt_raw.append(r_t)
        V.append(bf(embed(v[rs])))
        BhT.append(bf(embed(kkn_c * a_c * dec_end).T))
        KhT.append(bf(embed(k2_c * dec_end).T))
        gdiag.append(jnp.where(eye > 0.5, jnp.broadcast_to(jnp.exp(lc_end), (W, W)), 0.0))

    each = lambda f, *lists: [f(*xs) for xs in zip(*lists)]
    L = each(lambda x, y: jnp.where(lower_strict, _dot_nt(x, y), 0.0), A, Bt)
    Lak = each(lambda x, y: bf(jnp.where(lower_strict, _dot_nt(x, y), 0.0)), A, Kt)
    Mrb = each(lambda x, y: bf(jnp.where(lower_incl, _dot_nt(x, y), 0.0)), Rt, Bt)
    Mrk = each(lambda x, y: bf(jnp.where(lower_incl, _dot_nt(x, y), 0.0)), Rt, Kt)
    Tm = [eye + x for x in L]
    Pw = each(bf, L)
    for _ in range(int(math.log2(C)) - 1):
        Pw = each(lambda x: bf(_dot(x, x)), Pw)
        Tm = each(lambda t, x: t + _dot(bf(t), x), Tm, Pw)
    Tm = each(bf, Tm)
    Wm = each(lambda t, x: bf(_dot(t, x)), Tm, A)
    LakV = each(lambda x, y: bf(_dot(x, y)), Lak, V)
    U0 = each(lambda t, x: bf(_dot(t, x)), Tm, LakV)
    Pm = each(lambda d, x, y: d + _dot(x, y), gdiag, BhT, Wm)
    Qm = each(lambda x, y, s, t: _dot(x, y) + _dot(s, t), BhT, U0, KhT, V)
    Rh = each(_dot, Mrb, Wm)
    Y0 = each(lambda x, y, s, t: _dot(x, y) + _dot(s, t), Mrb, U0, Mrk, V)

    def flatten(x_bd):
        out = x_bd[0:C]
        for h in range(1, RW_HEADS):
            out = out + x_bd[h * C:(h + 1) * C]
        return out

    for c in range(n_chunks):
        rs = slice(c * C, (c + 1) * C)
        p_hi, p_lo = _split(Pm[c])
        p_ref[0, c, 0] = p_hi
        p_ref[0, c, 1] = p_lo
        q_ref[0, c] = Qm[c]
        rh_ref[0, rs, :] = Rt_raw[c] + flatten(Rh[c])
        y0_ref[0, rs, :] = flatten(Y0[c])


def _rw_chunks(z_rw, mu, w0, w_up, a0, a_up, g_up, k_k, k_a, r_k, chunks_per_step=8):
    B, T, _ = z_rw.shape
    C = RW_CHUNK
    assert C == HEAD, "the chunk build shares one index grid between time and head-dim masks"
    rows = C * chunks_per_step
    nc = T // C
    row = lambda a: a.reshape(1, -1)
    pair = lambda w: jnp.stack(_split(w))
    full = lambda shp: pl.BlockSpec(shp, lambda b, i: (0,) * len(shp))
    seq = lambda w: pl.BlockSpec((1, rows, w), lambda b, i: (b, i, 0))
    sds = jax.ShapeDtypeStruct
    seq_shape = sds((B, T, RW_WIDTH), F32)
    return pl.pallas_call(
        _rw_chunk_body,
        grid=(B, T // rows),
        in_specs=[seq(RW_IN),
                  pl.BlockSpec((1, 8, RW_IN), lambda b, i: (b, jnp.maximum(i * (rows // 8) - 1, 0), 0)),
                  full((1, RW_IN)), full((1, RW_WIDTH)), full((2, RW_DECAY_LORA, RW_WIDTH)),
                  full((1, RW_WIDTH)), full((2, RW_AAA_LORA, RW_WIDTH)), full((2, RW_GATE_LORA, RW_WIDTH)),
                  full((1, RW_WIDTH)), full((1, RW_WIDTH)), full((1, RW_WIDTH))],
        out_specs=(pl.BlockSpec((1, chunks_per_step, 2, RW_WIDTH, RW_WIDTH), lambda b, i: (b, i, 0, 0, 0)),
                   pl.BlockSpec((1, chunks_per_step, RW_WIDTH, RW_WIDTH), lambda b, i: (b, i, 0, 0)),
                   seq(RW_WIDTH), seq(RW_WIDTH), seq(RW_WIDTH), seq(RW_WIDTH)),
        out_shape=(sds((B, nc, 2, RW_WIDTH, RW_WIDTH), BF16), sds((B, nc, RW_WIDTH, RW_WIDTH), F32),
                   seq_shape, seq_shape, seq_shape, seq_shape),
        compiler_params=_cparams(("parallel", "parallel")),
        name="rwkv_chunk_build",
    )(z_rw, z_rw, row(mu), row(w0), pair(w_up), row(a0), pair(a_up), pair(g_up), row(k_k), row(k_a), row(r_k))


def _rw_scan_body(p_ref, q_ref, rh_ref, y0_ref, g_ref, bonus_ref, gng_ref, gnb_ref, o_ref, h_ref):
    @pl.when(pl.program_id(0) == 0)
    def _():
        h_ref[...] = jnp.zeros_like(h_ref)

    nb = h_ref.shape[0]
    head_ones = jnp.where(_head_ones(), 1.0, 0.0).astype(BF16)
    hs = [_split(h_ref[b]) for b in range(nb)]
    ys = [_dot3(_split(rh_ref[b]), hs[b]) + y0_ref[b] for b in range(nb)]
    for b in range(nb):
        h_ref[b] = _dot3((p_ref[b, 0, 0], p_ref[b, 0, 1]), hs[b]) + q_ref[b, 0]
    inv_n = 1.0 / HEAD
    for b in range(nb):
        y = ys[b]
        d = y - _dot_ones(y, head_ones) * inv_n
        var = _dot_ones(d * d, head_ones) * inv_n
        yn = d * lax.rsqrt(var + RW_GN_EPS) * gng_ref[...] + gnb_ref[...]
        o_ref[b] = (yn + bonus_ref[b]) * g_ref[b]


def _rw_scan(p, q, rh, y0, g, bonus, gn_g, gn_b):
    B, T, _ = rh.shape
    C = RW_CHUNK
    nc = T // C
    seq = pl.BlockSpec((B, C, RW_WIDTH), lambda c: (0, c, 0))
    vec = pl.BlockSpec((1, RW_WIDTH), lambda c: (0, 0))
    return pl.pallas_call(
        _rw_scan_body,
        grid=(nc,),
        in_specs=[pl.BlockSpec((B, 1, 2, RW_WIDTH, RW_WIDTH), lambda c: (0, c, 0, 0, 0)),
                  pl.BlockSpec((B, 1, RW_WIDTH, RW_WIDTH), lambda c: (0, c, 0, 0)),
                  seq, seq, seq, seq, vec, vec],
        out_specs=seq,
        out_shape=jax.ShapeDtypeStruct((B, T, RW_WIDTH), F32),
        scratch_shapes=[pltpu.VMEM((B, RW_WIDTH, RW_WIDTH), F32)],
        compiler_params=_cparams(("arbitrary",)),
        name="rwkv_chunk_scan",
    )(p, q, rh, y0, g, bonus, gn_g.reshape(1, -1), gn_b.reshape(1, -1))


def _pool_body(z_ref, zp_ref, w_ref, scale_ref, o_ref):
    i = pl.program_id(1)
    z = z_ref[0]
    tt = z.shape[0]
    halo = jnp.where(i == 0, 0.0, zp_ref[0])
    e = jnp.concatenate([halo, z], axis=0)
    sums = []
    s = e
    for shift in (1, 2, 4, 8):
        s = s + pltpu.roll(s, shift, axis=0)
        sums.append(s[POOL_HALO:, :])
    t_idx = i * tt + lax.broadcasted_iota(jnp.int32, (tt, POOL_WIDTH), 0)
    lane_group = lax.broadcasted_iota(jnp.int32, (tt, POOL_WIDTH), 1) // HEAD
    pooled = jnp.zeros_like(z)
    for gi, win in enumerate(POOL_WINDOWS):
        cnt = jnp.minimum(t_idx + 1, win).astype(F32)
        pooled = jnp.where(lane_group == gi, sums[gi] / cnt - z, pooled)
    o_ref[0] = _dot(pooled.astype(BF16), w_ref[...]) * scale_ref[...]


def _pool(z_pool, w_blockdiag_bf16, scale, tt=512):
    B, T, _ = z_pool.shape
    return pl.pallas_call(
        _pool_body,
        grid=(B, T // tt),
        in_specs=[pl.BlockSpec((1, tt, POOL_WIDTH), lambda b, i: (b, i, 0)),
                  pl.BlockSpec((1, POOL_HALO, POOL_WIDTH),
                               lambda b, i: (b, jnp.maximum(i * (tt // POOL_HALO) - 1, 0), 0)),
                  pl.BlockSpec((POOL_WIDTH, POOL_WIDTH), lambda b, i: (0, 0)),
                  pl.BlockSpec((1, POOL_WIDTH), lambda b, i: (0, 0))],
        out_specs=pl.BlockSpec((1, tt, POOL_WIDTH), lambda b, i: (b, i, 0)),
        out_shape=jax.ShapeDtypeStruct((B, T, POOL_WIDTH), F32),
        compiler_params=_cparams(("parallel", "parallel")),
        name="pool_mixer",
    )(z_pool, z_pool, w_blockdiag_bf16, scale.reshape(1, -1))


def _rope(x, cos_w, sin_w):
    w = x.shape[1]
    lane = lax.broadcasted_iota(jnp.int32, x.shape, 1) % HEAD
    partner = jnp.where(lane < HEAD // 2, pltpu.roll(x, w - HEAD // 2, axis=1), pltpu.roll(x, HEAD // 2, axis=1))
    return x * cos_w + partner * sin_w


def _head_lanes(a, h, dtype):
    part = a[:, h * HEAD:(h + 1) * HEAD]
    return jnp.concatenate([part, jnp.zeros_like(part)], axis=1).astype(dtype)


def _nsa_prep_body(zq_ref, zkc_ref, zks_ref, zvs_ref, zkw_ref, zvw_ref, zg_ref, cos_ref, sin_ref,
                   q_ref, kc_ref, ks_ref, vst_ref, kw_ref, vwt_ref, gt_ref):
    cos = cos_ref[0]
    sin = sin_ref[0]
    cos_q = jnp.concatenate([cos] * (NSA_WIDTH // LANE), axis=1)
    sin_q = jnp.concatenate([sin] * (NSA_WIDTH // LANE), axis=1)
    qr = _rope(zq_ref[0], cos_q, sin_q) * (HEAD ** -0.5 * LOG2E)
    pad_lane = lax.broadcasted_iota(jnp.int32, (qr.shape[0], LANE), 1) == HEAD
    for h in range(NSA_Q_HEADS):
        q_ref[0, h] = jnp.where(pad_lane, NEG, _head_lanes(qr, h, F32)).astype(BF16)
    kc_ref[0] = _rope(zkc_ref[0], cos, sin)
    ks = _rope(zks_ref[0], cos, sin)
    kw = _rope(zkw_ref[0], cos, sin)
    vst = zvs_ref[0].T
    vwt = zvw_ref[0].T
    for h in range(NSA_KV_HEADS):
        sl = slice(h * HEAD, (h + 1) * HEAD)
        ks_ref[0, h] = _head_lanes(ks, h, BF16)
        kw_ref[0, h] = _head_lanes(kw, h, BF16)
        tail = lax.broadcasted_iota(jnp.int32, (V_ROWS - HEAD, vst.shape[1]), 0)
        ones_row = jnp.where(tail == 0, 1.0, 0.0).astype(BF16)
        vst_ref[0, h, 0:HEAD, :] = vst[sl, :].astype(BF16)
        vst_ref[0, h, HEAD:V_ROWS, :] = ones_row
        vwt_ref[0, h, 0:HEAD, :] = vwt[sl, :].astype(BF16)
        vwt_ref[0, h, HEAD:V_ROWS, :] = ones_row
    gt = _sigmoid(zg_ref[0]).T
    for h in range(NSA_KV_HEADS):
        gt_ref[0, h] = gt[16 * h:16 * (h + 1), :]


def _nsa_prep(z_q, z_kc, z_ks, z_vs, z_kw, z_vw, z_g, cos_t, sin_t, tt=512):
    B, T, _ = z_q.shape
    seq = lambda w: pl.BlockSpec((1, tt, w), lambda b, i: (b, i, 0))
    hm = lambda nh: pl.BlockSpec((1, nh, tt, LANE), lambda b, i: (b, 0, i, 0))
    tr = lambda rows: pl.BlockSpec((1, NSA_KV_HEADS, rows, tt), lambda b, i: (b, 0, 0, i))
    sds = jax.ShapeDtypeStruct
    return pl.pallas_call(
        _nsa_prep_body,
        grid=(B, T // tt),
        in_specs=[seq(NSA_WIDTH)] + [seq(NSA_KV)] * 6 + [seq(LANE), seq(LANE)],
        out_specs=(hm(NSA_Q_HEADS), seq(NSA_KV), hm(NSA_KV_HEADS), tr(V_ROWS), hm(NSA_KV_HEADS), tr(V_ROWS), tr(16)),
        out_shape=(sds((B, NSA_Q_HEADS, T, LANE), BF16), sds((B, T, NSA_KV), F32),
                   sds((B, NSA_KV_HEADS, T, LANE), BF16), sds((B, NSA_KV_HEADS, V_ROWS, T), BF16),
                   sds((B, NSA_KV_HEADS, T, LANE), BF16), sds((B, NSA_KV_HEADS, V_ROWS, T), BF16),
                   sds((B, NSA_KV_HEADS, 16, T), F32)),
        compiler_params=_cparams(("parallel", "parallel")),
        name="nsa_prep",
    )(z_q, z_kc, z_ks, z_vs, z_kw, z_vw, z_g, cos_t, sin_t)


def _nsa_cmp_body(xk_ref, xv_ref, pek_ref, pev_ref, wk_ref, wv_ref, kc_ref, vct_ref):
    half = NSA_KV
    n = kc_ref.shape[2]

    def compress(x_ref, pe_ref, w_ref):
        lo = hi = None
        for l in range(CMP_STRIDE):
            x = x_ref[0, pl.ds(l, n, stride=CMP_STRIDE), :]
            rows = slice(l * NSA_KV, (l + 1) * NSA_KV)
            d_lo = _dot((x + pe_ref[0:1, rows]).astype(BF16), w_ref[rows, 0:half])
            d_hi = _dot((x + pe_ref[1:2, rows]).astype(BF16), w_ref[rows, half:2 * half])
            lo = d_lo if lo is None else lo + d_lo
            hi = d_hi if hi is None else hi + d_hi
        hi_next = pltpu.roll(hi, n - 1, axis=0)
        row = lax.broadcasted_iota(jnp.int32, lo.shape, 0)
        return lo + jnp.where(row == n - 1, 0.0, hi_next)

    kc = compress(xk_ref, pek_ref, wk_ref)
    vct = compress(xv_ref, pev_ref, wv_ref).T
    for h in range(NSA_KV_HEADS):
        sl = slice(h * HEAD, (h + 1) * HEAD)
        kc_ref[0, h] = _head_lanes(kc, h, BF16)
        vct_ref[0, h] = vct[sl, :].astype(BF16)


def _nsa_compress(kc_tokens, vc_tokens, pe_k2, pe_v2, wk2, wv2):
    B, T, _ = kc_tokens.shape
    ncp, gw = T // CMP_STRIDE, CMP_STRIDE * NSA_KV
    grp = pl.BlockSpec((1, T, NSA_KV), lambda b: (b, 0, 0))
    pe = pl.BlockSpec((2, gw), lambda b: (0, 0))
    wspec = pl.BlockSpec((gw, 2 * NSA_KV), lambda b: (0, 0))
    sds = jax.ShapeDtypeStruct
    return pl.pallas_call(
        _nsa_cmp_body,
        grid=(B,),
        in_specs=[grp, grp, pe, pe, wspec, wspec],
        out_specs=(pl.BlockSpec((1, NSA_KV_HEADS, ncp, LANE), lambda b: (b, 0, 0, 0)),
                   pl.BlockSpec((1, NSA_KV_HEADS, HEAD, ncp), lambda b: (b, 0, 0, 0))),
        out_shape=(sds((B, NSA_KV_HEADS, ncp, LANE), BF16), sds((B, NSA_KV_HEADS, HEAD, ncp), BF16)),
        compiler_params=_cparams(("parallel",)),
        name="nsa_compress",
    )(kc_tokens, vc_tokens, pe_k2, pe_v2, wk2, wv2)


def _nsa_attn_body(q_ref, kc_ref, vct_ref, ks_ref, vst_ref, kw_ref, vwt_ref, gt_ref, mcs_ref, blk_ref, cmpb_ref,
                   winb_ref, o_ref, *score_refs):
    jb = pl.program_id(1)
    s0 = jb * Q_BLOCK
    cols = NSA_GQA * Q_BLOCK
    heads = range(NSA_KV_HEADS)
    q = [q_ref[0, NSA_GQA * h:NSA_GQA * (h + 1)].reshape(cols, LANE) for h in heads]
    ncp = kc_ref.shape[2]
    per_gqa = lambda x: jnp.concatenate([x] * NSA_GQA, axis=1)
    col_max = lambda x: jnp.max(x, axis=0, keepdims=True)

    cmp_mask = per_gqa(cmpb_ref[pl.ds(pl.multiple_of(ncp - s0 // CMP_STRIDE, 8), ncp), :])
    t_c = s0 + lax.broadcasted_iota(jnp.int32, (1, cols), 1) % Q_BLOCK
    any_valid = jnp.where(t_c >= CMP_BLOCK - 1, 1.0, 0.0)
    mcs = mcs_ref[...]
    sc = [_dot_nt(kc_ref[0, h], q[h]) + cmp_mask for h in heads]
    e = [jnp.exp2(x - col_max(x)) for x in sc]
    pc = [x * (any_valid / jnp.sum(x, axis=0, keepdims=True)) for x in e]
    o_cmp = [_dot(vct_ref[0, h], pc[h].astype(BF16)) for h in heads]

    def importance(p):
        psum = p[:, 0:Q_BLOCK]
        for g in range(1, NSA_GQA):
            psum = psum + p[:, g * Q_BLOCK:(g + 1) * Q_BLOCK]
        p_hi = psum.astype(BF16)
        rest = psum - p_hi.astype(F32)
        p_mid = rest.astype(BF16)
        p_lo = (rest - p_mid.astype(F32)).astype(BF16)
        return _dot(mcs, p_hi) + (_dot(mcs, p_mid) + _dot(mcs, p_lo))

    imp = [importance(p) for p in pc]

    wkeys = WINDOW + Q_BLOCK
    off_w = pl.multiple_of(s0, Q_BLOCK)
    tri_old, tri_new = per_gqa(winb_ref[0]), per_gqa(winb_ref[1])
    sw = [_dot_nt(kw_ref[0, h, pl.ds(off_w, wkeys), :], q[h]) for h in heads]
    sw = [jnp.concatenate([x[0:Q_BLOCK] + tri_old, x[Q_BLOCK:WINDOW], x[WINDOW:wkeys] + tri_new], axis=0) for x in sw]
    pw = [jnp.exp2((x - col_max(x)).astype(BF16)) for x in sw]
    acc_w = [_dot(vwt_ref[0, h, :, pl.ds(off_w, wkeys)], pw[h]) for h in heads]

    j_idx = lax.broadcasted_iota(jnp.int32, (NSEL_PAD, Q_BLOCK), 0)
    jc = (s0 + lax.broadcasted_iota(jnp.int32, (NSEL_PAD, Q_BLOCK), 1)) // SEL_BLOCK
    causal = j_idx <= jc
    taken = -jnp.inf
    forced = lambda x: jnp.where(j_idx == 0, taken, jnp.where(j_idx == jc, taken, jnp.where(j_idx == jc - 1, taken, x)))
    work = [jnp.where(causal, forced(x), NEG) for x in imp]
    j_f = j_idx.astype(F32)
    for _ in range(SEL_TOPN - 3):
        top = [col_max(w) for w in work]
        first = [jnp.min(jnp.where(w == t, j_f, float(NSEL_PAD)), axis=0, keepdims=True) for w, t in zip(work, top)]
        work = [jnp.where(j_f == f, -jnp.inf, w) for w, f in zip(work, first)]
    bias = [jnp.where(causal, jnp.where(w == -jnp.inf, 0.0, NEG), NEG) for w in work]

    q_aug = [jnp.concatenate([q[h], jnp.concatenate([bias[h].T.astype(BF16)] * NSA_GQA, axis=0)], axis=1)
             for h in heads]

    def sel_scores(h, off):
        k_aug = jnp.concatenate([ks_ref[0, h, pl.ds(off, KEY_GROUP), :], blk_ref[pl.ds(off, KEY_GROUP), :]], axis=1)
        return _dot_nt(k_aug, q_aug[h])

    def flash_update(v_t, st, carry):
        m, acc = carry
        m_new = jnp.maximum(m, col_max(st))
        p = jnp.exp2((st - m_new).astype(BF16))
        return m_new, acc * jnp.exp2(m - m_new) + _dot(v_t, p)

    init = (jnp.full((1, cols), NEG, F32), jnp.zeros((V_ROWS, cols), F32))
    n_groups = s0 // KEY_GROUP + 1
    t_q = s0 + lax.broadcasted_iota(jnp.int32, (Q_BLOCK, cols), 1) % Q_BLOCK
    r_q = lax.broadcasted_iota(jnp.int32, (Q_BLOCK, cols), 0)

    def put_scores(ref, h, g):
        off = pl.multiple_of(g * KEY_GROUP, KEY_GROUP)
        ref[...] = sel_scores(h, off)
        r = pl.multiple_of(jnp.clip(s0 - off, 0, KEY_GROUP - Q_BLOCK), Q_BLOCK)
        tile = ref[pl.ds(r, Q_BLOCK), :]
        ref[pl.ds(r, Q_BLOCK), :] = jnp.where(off + r + r_q <= t_q, tile, NEG)

    def consume(ref, h, g, carry):
        off = pl.multiple_of(g * KEY_GROUP, KEY_GROUP)
        return flash_update(vst_ref[0, h, :, pl.ds(off, KEY_GROUP)], ref[...], carry)

    last = n_groups - 1
    buf_a = [score_refs[2 * h] for h in heads]
    buf_b = [score_refs[2 * h + 1] for h in heads]
    for h in heads:
        put_scores(buf_a[h], h, 0)

    def two_groups(i, carries):
        g = 2 * i
        for h in heads:
            put_scores(buf_b[h], h, g + 1)
        carries = [consume(buf_a[h], h, g, carries[h]) for h in heads]
        for h in heads:
            put_scores(buf_a[h], h, jnp.minimum(g + 2, last))
        return tuple(consume(buf_b[h], h, g + 1, carries[h]) for h in heads)

    carries = lax.fori_loop(0, n_groups // 2, two_groups, tuple(init for _ in heads))
    carries = lax.fori_loop(0, n_groups % 2, lambda _, c: tuple(consume(buf_a[h], h, last, c[h]) for h in heads),
                            carries)

    for h in heads:
        _, acc_s = carries[h]

        def gate_row(branch):
            return jnp.concatenate([gt_ref[0, h, 3 * g + branch:3 * g + branch + 1, :] for g in range(NSA_GQA)], axis=1)

        o_t = (gate_row(0) * o_cmp[h] + (gate_row(1) / acc_s[HEAD:HEAD + 1]) * acc_s[0:HEAD]
               + (gate_row(2) / acc_w[h][HEAD:HEAD + 1]) * acc_w[h][0:HEAD])
        stacked = jnp.concatenate([o_t[:, g * Q_BLOCK:(g + 1) * Q_BLOCK] for g in range(NSA_GQA)], axis=0)
        o_ref[0, :, NSA_GQA * HEAD * h:NSA_GQA * HEAD * (h + 1)] = stacked.T


def _nsa_masks(seq_len):
    tq = np.arange(Q_BLOCK)[None, :]
    ncp = seq_len // CMP_STRIDE
    n_rel = np.arange(2 * ncp)[:, None] - ncp
    cmp_b = np.where(CMP_STRIDE * n_rel + CMP_BLOCK - 1 <= tq, 0.0, NEG).astype(np.float32)
    k_rel = np.arange(Q_BLOCK)[:, None]
    win_b = np.stack([np.where(k_rel > tq, 0.0, NEG), np.where(k_rel <= tq, 0.0, NEG)]).astype(np.float32)
    return jnp.asarray(cmp_b), jnp.asarray(win_b)


def _nsa_attention(q_r, kcmp, vcmp_t, ks_r, vs_t, kw_pad, vw_pad_t, g_t, mcs_t, blk_onehot, cmp_b, win_b):
    B, _, T, _ = q_r.shape
    ncp = kcmp.shape[2]
    nqb = T // Q_BLOCK
    tw = T + WINDOW
    nkv = NSA_KV_HEADS
    assert T % KEY_GROUP == 0 and T // SEL_BLOCK <= NSEL_PAD and WINDOW > Q_BLOCK and WINDOW % Q_BLOCK == 0
    kv_rows = lambda n: pl.BlockSpec((1, nkv, n, LANE), lambda b, j: (b, 0, 0, 0))
    kv_cols = lambda n, rows: pl.BlockSpec((1, nkv, rows, n), lambda b, j: (b, 0, 0, 0))
    const = lambda shp: pl.BlockSpec(shp, lambda b, j: (0,) * len(shp))
    return pl.pallas_call(
        _nsa_attn_body,
        grid=(B, nqb),
        in_specs=[pl.BlockSpec((1, NSA_Q_HEADS, Q_BLOCK, LANE), lambda b, j: (b, 0, j, 0)),
                  kv_rows(ncp), kv_cols(ncp, HEAD), kv_rows(T), kv_cols(T, V_ROWS), kv_rows(tw), kv_cols(tw, V_ROWS),
                  pl.BlockSpec((1, nkv, 16, Q_BLOCK), lambda b, j: (b, 0, 0, j)),
                  const((NSEL_PAD, ncp)), const((T, NSEL_PAD)), const((2 * ncp, Q_BLOCK)),
                  const((2, Q_BLOCK, Q_BLOCK))],
        out_specs=pl.BlockSpec((1, Q_BLOCK, NSA_WIDTH), lambda b, j: (b, j, 0)),
        out_shape=jax.ShapeDtypeStruct((B, T, NSA_WIDTH), F32),
        scratch_shapes=[pltpu.VMEM((KEY_GROUP, NSA_GQA * Q_BLOCK), F32)] * (2 * nkv),
        compiler_params=_cparams(("parallel", "arbitrary")),
        name="nsa_attention",
    )(q_r, kcmp, vcmp_t, ks_r, vs_t, kw_pad, vw_pad_t, g_t, mcs_t, blk_onehot, cmp_b, win_b)


def _gelu_tanh(x):
    return 0.5 * x * (1.0 + jnp.tanh(math.sqrt(2.0 / math.pi) * (x + 0.044715 * (x * x * x))))


def _tail_body(x_ref, ya_ref, yb_ref, yc_ref, p_ref, wa_ref, wb_ref, wc_ref, gffn_ref, wup_ref, cw_ref, cb_ref,
               wdn_ref, gple_ref, wg_ref, wp_ref, gfin_ref, o_ref, carry_ref, *, tiles_per_seq, f_chunk, final):
    i = pl.program_id(0)
    mix = _dot(ya_ref[...].astype(BF16), wa_ref[...])
    mix = mix + _dot(yb_ref[...].astype(BF16), wb_ref[...])
    mix = mix + _dot(yc_ref[...].astype(BF16), wc_ref[...])
    x = x_ref[...] + mix

    tm = x.shape[0]
    d_ff = wdn_ref.shape[0]
    h = _rms(x, gffn_ref[...]).astype(BF16)
    first = (i % tiles_per_seq) == 0
    row = lax.broadcasted_iota(jnp.int32, (tm, f_chunk), 0)

    def conv(col0):
        u = _dot(h, wup_ref[:, col0:col0 + f_chunk])
        p2 = jnp.where(first, 0.0, carry_ref[6:7, col0:col0 + f_chunk])
        p1 = jnp.where(first, 0.0, carry_ref[7:8, col0:col0 + f_chunk])
        u1 = jnp.where(row == 0, p1, pltpu.roll(u, 1, axis=0))
        u2 = jnp.where(row == 0, p2, jnp.where(row == 1, p1, pltpu.roll(u, 2, axis=0)))
        carry_ref[:, col0:col0 + f_chunk] = u[tm - 8:tm, :]
        cw = cw_ref[:, col0:col0 + f_chunk]
        return cw[0:1] * u2 + cw[1:2] * u1 + cw[2:3] * u + cb_ref[:, col0:col0 + f_chunk]

    for c0 in range(0, d_ff, f_chunk):
        act = _gelu_tanh(conv(c0)) * conv(d_ff + c0)
        x = x + _dot(act.astype(BF16), wdn_ref[c0:c0 + f_chunk, :])

    gate = _sigmoid(_dot(_rms(x, gple_ref[...]).astype(BF16), wg_ref[...]))
    y = x + _dot(p_ref[...].astype(BF16), wp_ref[...]) * gate
    o_ref[...] = _rms(y, gfin_ref[...]) if final else y


def _layer_tail(x2, ya, yb, yc, p2, w_out, g_ffn, w_up, conv_w, conv_b, w_down, g_ple, w_gate, w_proj, g_final,
                seq_len, final, tm=512, f_chunk=1408):
    m, d = x2.shape
    f2 = w_up.shape[1]
    d_ff = f2 // 2
    pd = p2.shape[1]
    assert d_ff % f_chunk == 0 and f_chunk % LANE == 0 and seq_len % tm == 0
    bf = lambda w: w.astype(BF16)
    wa, wb, wc = bf(w_out[:RW_WIDTH]), bf(w_out[RW_WIDTH:RW_WIDTH + POOL_WIDTH]), bf(w_out[RW_WIDTH + POOL_WIDTH:])
    tile = lambda w: pl.BlockSpec((tm, w), lambda i: (i, 0))
    full = lambda shp: pl.BlockSpec(shp, lambda i: (0,) * len(shp))
    return pl.pallas_call(
        functools.partial(_tail_body, tiles_per_seq=seq_len // tm, f_chunk=f_chunk, final=final),
        grid=(m // tm,),
        in_specs=[tile(d), tile(RW_WIDTH), tile(POOL_WIDTH), tile(NSA_WIDTH), tile(pd),
                  full((RW_WIDTH, d)), full((POOL_WIDTH, d)), full((NSA_WIDTH, d)),
                  full((1, d)), full((d, f2)), full((CONV_W, f2)), full((1, f2)), full((d_ff, d)),
                  full((1, d)), full((d, d)), full((pd, d)), full((1, d))],
        out_specs=tile(d),
        out_shape=jax.ShapeDtypeStruct((m, d), F32),
        scratch_shapes=[pltpu.VMEM((8, f2), F32)],
        compiler_params=_cparams(("arbitrary",)),
        name="layer_tail_final" if final else "layer_tail",
    )(x2, ya, yb, yc, p2, wa, wb, wc, g_ffn.reshape(1, d), bf(w_up), conv_w.reshape(CONV_W, f2),
      conv_b.reshape(1, f2), bf(w_down), g_ple.reshape(1, d), bf(w_gate), bf(w_proj), g_final.reshape(1, d))


def _overlap_matrix(seq_len):
    ncp = seq_len // CMP_STRIDE
    n_cmp = (seq_len - CMP_BLOCK) // CMP_STRIDE + 1
    cs = CMP_STRIDE * np.arange(n_cmp)
    ss = SEL_BLOCK * np.arange(seq_len // SEL_BLOCK)
    ov = (np.minimum(cs[:, None] + CMP_BLOCK - 1, ss[None] + SEL_BLOCK - 1) - np.maximum(cs[:, None], ss[None]) + 1)
    m = np.zeros((NSEL_PAD, ncp), np.float32)
    m[:ss.size, :n_cmp] = (np.clip(ov, 0, CMP_BLOCK).astype(np.float32) / CMP_BLOCK).T
    return jnp.asarray(m, dtype=BF16)


def _compress_weight(w):
    halves = w.reshape(2, CMP_STRIDE, HEAD, HEAD)
    eye = jnp.eye(NSA_KV_HEADS, dtype=w.dtype)
    w2 = jnp.einsum('sldf,hg->lhdsgf', halves, eye)
    return w2.reshape(CMP_STRIDE * NSA_KV, 2 * NSA_KV).astype(BF16)


def _compress_pe(pe):
    halves = pe.reshape(2, CMP_STRIDE, 1, HEAD)
    return jnp.broadcast_to(halves, (2, CMP_STRIDE, NSA_KV_HEADS, HEAD)).reshape(2, CMP_STRIDE * NSA_KV)


def _in_proj_layout(w_in):
    d = w_in.shape[0]
    gate0 = RW_IN + POOL_WIDTH + NSA_WIDTH + 6 * NSA_KV
    per_head = NSA_GQA * 3
    gates = jnp.zeros((d, LANE), w_in.dtype)
    for h in range(NSA_KV_HEADS):
        gates = gates.at[:, 16 * h:16 * h + per_head].set(w_in[:, gate0 + per_head * h:gate0 + per_head * (h + 1)])
    return jnp.concatenate([w_in[:, :gate0], gates], axis=1).astype(BF16)


_IN_SPLITS = ((0, RW_IN), (RW_IN, RW_IN + POOL_WIDTH), (RW_IN + POOL_WIDTH, RW_IN + POOL_WIDTH + NSA_WIDTH)) + tuple(
    (RW_IN + POOL_WIDTH + NSA_WIDTH + i * NSA_KV, RW_IN + POOL_WIDTH + NSA_WIDTH + (i + 1) * NSA_KV) for i in range(7))


def kernel(x, p, positions, g_mix, w_in, rw_mu, rw_w0, rw_w_up, rw_a0, rw_a_up, rw_g_up, rw_k_k, rw_k_a, rw_r_k, rw_gn_g, rw_gn_b, pool_w, pool_scale, nsa_pe_k, nsa_pe_v, nsa_w_ck, nsa_w_cv, w_out, g_ffn, ffn_w_up, ffn_conv_w, ffn_conv_b, ffn_w_down, g_ple, ple_w_gate, ple_w_proj, g_final):
    B, T, D = x.shape
    depth = w_in.shape[0]
    M = B * T

    half = HEAD // 2
    inv = ROPE_THETA ** (-jnp.arange(half, dtype=F32) / half)
    ang = positions.astype(F32)[..., None] * inv
    cos, sin = jnp.cos(ang), jnp.sin(ang)
    cos_t = jnp.concatenate([cos, cos] * (LANE // HEAD), axis=-1)
    sin_t = jnp.concatenate([-sin, sin] * (LANE // HEAD), axis=-1)
    mcs_t = _overlap_matrix(T)
    blk_onehot = (jnp.arange(T)[:, None] // SEL_BLOCK == jnp.arange(NSEL_PAD)[None, :]).astype(BF16)
    cmp_b, win_b = _nsa_masks(T)
    key_pad = (jnp.arange(LANE) == HEAD).astype(BF16)

    x2 = x.reshape(M, D)
    for i in range(depth):
        zs = _norm_matmul(x2, g_mix[i], _in_proj_layout(w_in[i]), _IN_SPLITS)
        z_rw, z_pool, z_q, z_kc, z_vc, z_ks, z_vs, z_kw, z_vw, z_g = (
            z.reshape(B, T, z.shape[1]) for z in zs)

        pm, qm, rh, y0, gate, bonus = _rw_chunks(z_rw, rw_mu[i], rw_w0[i], rw_w_up[i], rw_a0[i], rw_a_up[i],
                                                 rw_g_up[i], rw_k_k[i], rw_k_a[i], rw_r_k[i].reshape(-1))
        y_a = _rw_scan(pm, qm, rh, y0, gate, bonus, rw_gn_g[i], rw_gn_b[i])

        w_pool_bd = jax.scipy.linalg.block_diag(*[pool_w[i, gi] for gi in range(pool_w.shape[1])]).astype(BF16)
        y_b = _pool(z_pool, w_pool_bd, pool_scale[i])

        q_r, kc_r, ks_r, vs_t, kw_r, vw_t, g_t = _nsa_prep(z_q, z_kc, z_ks, z_vs, z_kw, z_vw, z_g, cos_t, sin_t)
        kcmp, vcmp_t = _nsa_compress(kc_r, z_vc, _compress_pe(nsa_pe_k[i]), _compress_pe(nsa_pe_v[i]),
                                     _compress_weight(nsa_w_ck[i]), _compress_weight(nsa_w_cv[i]))
        kw_pad = jnp.concatenate([jnp.broadcast_to(key_pad, (B, NSA_KV_HEADS, WINDOW, LANE)), kw_r], axis=2)
        vw_pad_t = jnp.pad(vw_t, ((0, 0), (0, 0), (0, 0), (WINDOW, 0)))
        y_c = _nsa_attention(q_r, kcmp, vcmp_t, ks_r, vs_t, kw_pad, vw_pad_t, g_t, mcs_t, blk_onehot, cmp_b, win_b)

        x2 = _layer_tail(x2, y_a.reshape(M, -1), y_b.reshape(M, -1), y_c.reshape(M, -1), p[i].reshape(M, -1),
                         w_out[i], g_ffn[i], ffn_w_up[i], ffn_conv_w[i], ffn_conv_b[i], ffn_w_down[i],
                         g_ple[i], ple_w_gate[i], ple_w_proj[i], g_final, T, final=(i == depth - 1))
    return x2.reshape(B, T, D)
```
